```python
import jax
import jax.numpy as jnp
from jax import lax
import numpy as np


D_MODEL = 1024
BATCH = 16
SEQ = 2048
DEPTH = 4

N_MIXERS = 2
RET_HEADS = 4
RET_QK_DIM = D_MODEL // RET_HEADS
RET_V_DIM = 2 * RET_QK_DIM
RET_CHUNK = 128
RET_IN_DIM = 2 * RET_HEADS * RET_QK_DIM + 2 * RET_HEADS * RET_V_DIM
NSA_HEADS = 16
NSA_GROUPS = 4
NSA_HPG = NSA_HEADS // NSA_GROUPS
NSA_HEAD_DIM = D_MODEL // NSA_HEADS
NSA_KV_DIM = NSA_GROUPS * NSA_HEAD_DIM
NSA_IN_DIM = D_MODEL + 6 * NSA_KV_DIM + 3 * NSA_HEADS
CMP_BLOCK = 32
CMP_STRIDE = 16
CMP_HIDDEN = 256
SEL_BLOCK = 64
SEL_TOP = 16
WINDOW = 512
Q_BLOCK = 128
FORCED_SCORE = 1e4
INVALID_SCORE = -1.0
D_FF = 2816
FFN_RES = 0.5
DN_ALPHA = (2 * DEPTH) ** 0.25
DN_BETA = (8 * DEPTH) ** -0.25
LN_EPS = 1e-5
GN_EPS = 1e-6
NEG_INF = -1e30
ADA_GAIN = 0.1
N_RET_LAYERS = (DEPTH + 1) // 2
N_NSA_LAYERS = DEPTH // 2

kernel_name = 'hybrid_retention_nsa_macaron_deepnorm'


def layer_norm(x, g, b):
    xf = x.astype(jnp.float32)
    mu = jnp.mean(xf, -1, keepdims=True)
    var = jnp.mean(jnp.square(xf - mu), -1, keepdims=True)
    return ((xf - mu) * lax.rsqrt(var + LN_EPS) * g + b).astype(x.dtype)


def modulate(x, shift, scale):
    return x * (1.0 + scale[:, None, :]) + shift[:, None, :]


def post_norm_residual(x, y, gate, res_w, g, b):
    return layer_norm(DN_ALPHA * x + res_w * (1.0 + gate[:, None, :]) * y, g, b)


def swiglu(h, w_in, w_out):
    a, u = jnp.split(h @ w_in, 2, axis=-1)
    return (jax.nn.silu(a) * u) @ w_out


def masked_softmax(s, mask, axis):
    p = jax.nn.softmax(jnp.where(mask, s, NEG_INF), axis=axis)
    return p * mask


def retention(h, w_in, w_out):
    B, S, _ = h.shape
    C = RET_CHUNK
    n = S // C
    qk = RET_HEADS * RET_QK_DIM
    q, k, v, g = jnp.split(h @ w_in, [qk, 2 * qk, 2 * qk + RET_HEADS * RET_V_DIM], axis=-1)

    def heads(t, d):
        return t.reshape(B, n, C, RET_HEADS, d).transpose(1, 0, 3, 2, 4).astype(jnp.float32)

    qc = heads(q, RET_QK_DIM)
    kc = heads(k, RET_QK_DIM) * (RET_QK_DIM ** -0.5)
    vc = heads(v, RET_V_DIM)
    log_g = jnp.log(1.0 - jnp.exp2(-5.0 - jnp.arange(RET_HEADS, dtype=jnp.float32)))
    i = jnp.arange(C, dtype=jnp.float32)
    diff = i[:, None] - i[None, :]
    decay_intra = jnp.where(diff >= 0, jnp.exp(log_g[:, None, None] * jnp.maximum(diff, 0.0)), 0.0)
    decay_q = jnp.exp(log_g[:, None] * (i + 1.0))[:, :, None]
    decay_k = jnp.exp(log_g[:, None] * (C - 1.0 - i))[:, :, None]
    decay_state = jnp.exp(log_g * C)[:, None, None]

    def step(state, inp):
        qi, ki, vi = inp
        scores = jnp.einsum('bhcd,bhkd->bhck', qi, ki) * decay_intra
        o = jnp.einsum('bhck,bhke->bhce', scores, vi) + jnp.einsum('bhcd,bhde->bhce', qi * decay_q, state)
        state = decay_state * state + jnp.einsum('bhcd,bhce->bhde', ki * decay_k, vi)
        return state, o

    state0 = jnp.zeros((B, RET_HEADS, RET_QK_DIM, RET_V_DIM), jnp.float32)
    _, o = lax.scan(step, state0, (qc, kc, vc))
    mu = jnp.mean(o, -1, keepdims=True)
    var = jnp.mean(jnp.square(o - mu), -1, keepdims=True)
    o = (o - mu) * lax.rsqrt(var + GN_EPS)
    o = o.transpose(1, 0, 3, 2, 4).reshape(B, S, RET_HEADS * RET_V_DIM)
    return (o * jax.nn.silu(g.astype(jnp.float32))).astype(h.dtype) @ w_out


def alibi_slopes(n):
    return jnp.exp2(-8.0 * jnp.arange(1, n + 1, dtype=jnp.float32) / n)


def cmp_to_sel_overlap(n_cmp, n_sel):
    c0 = np.arange(n_cmp) * CMP_STRIDE
    s0 = np.arange(n_sel) * SEL_BLOCK
    lo = np.maximum(c0[:, None], s0[None, :])
    hi = np.minimum(c0[:, None] + CMP_BLOCK, s0[None, :] + SEL_BLOCK)
    return (np.clip(hi - lo, 0, None) / CMP_BLOCK).astype(np.float32)


def compress(t, pe, w1, w2):
    B, S, G, dh = t.shape
    n_cmp = (S - CMP_BLOCK) // CMP_STRIDE + 1
    idx = jnp.arange(n_cmp)[:, None] * CMP_STRIDE + jnp.arange(CMP_BLOCK)[None, :]
    blocks = t[:, idx] + pe[None, None, :, None, :]
    blocks = blocks.transpose(0, 1, 3, 2, 4).reshape(B, n_cmp, G, CMP_BLOCK * dh)
    return jax.nn.gelu(blocks @ w1) @ w2


def nsa(h, w_in, w_out, pe_k, pe_v, ck_w1, ck_w2, cv_w1, cv_w2):
    B, S, D = h.shape
    G, HPG, dh, kv = NSA_GROUPS, NSA_HPG, NSA_HEAD_DIM, NSA_KV_DIM
    splits = [D + j * kv for j in range(7)]
    q, kc, vc, ks, vs, kw, vw, gates = jnp.split(h @ w_in, splits, axis=-1)
    q = q.reshape(B, S, G, HPG, dh) * (dh ** -0.5)
    kc = compress(kc.reshape(B, S, G, dh), pe_k, ck_w1, ck_w2)
    vc = compress(vc.reshape(B, S, G, dh), pe_v, cv_w1, cv_w2)
    n_cmp = kc.shape[1]
    n_sel = S // SEL_BLOCK
    top = min(SEL_TOP, n_sel)
    ks = ks.reshape(B, n_sel, SEL_BLOCK, G, dh).transpose(0, 3, 1, 2, 4)
    vs = vs.reshape(B, n_sel, SEL_BLOCK, G, dh).transpose(0, 3, 1, 2, 4)
    pad = ((0, 0), (WINDOW, 0), (0, 0), (0, 0))
    kw = jnp.pad(kw.reshape(B, S, G, dh), pad)
    vw = jnp.pad(vw.reshape(B, S, G, dh), pad)
    gates = jax.nn.sigmoid(gates.astype(jnp.float32)).reshape(B, S, G, HPG, 3)
    slopes = alibi_slopes(NSA_HEADS).reshape(G, HPG)
    cmp_end = jnp.arange(n_cmp) * CMP_STRIDE + CMP_BLOCK - 1
    overlap = jnp.asarray(cmp_to_sel_overlap(n_cmp, n_sel))
    blk = jnp.arange(n_sel)
    n_qb = S // Q_BLOCK
    span = WINDOW + Q_BLOCK

    def block_fn(bq):
        b = bq // n_qb
        q0 = (bq % n_qb) * Q_BLOCK
        t = q0 + jnp.arange(Q_BLOCK)
        qi = lax.dynamic_slice(q, (b, q0, 0, 0, 0), (1, Q_BLOCK, G, HPG, dh))[0]
        kcb, vcb = kc[b], vc[b]
        dist_c = t[:, None] - cmp_end[None, :]
        s = jnp.einsum('qghd,ngd->ghqn', qi, kcb).astype(jnp.float32) - slopes[:, :, None, None] * dist_c.astype(jnp.float32)
        p_cmp = masked_softmax(s, dist_c >= 0, -1)
        o_cmp = jnp.einsum('ghqn,ngd->qghd', p_cmp, vcb)
        imp = jnp.einsum('ghqn,nj->gqj', p_cmp, overlap)
        cur = (t // SEL_BLOCK)[:, None]
        forced = (blk == 0) | (blk == cur) | (blk == cur - 1)
        valid = blk[None, :] * SEL_BLOCK <= t[:, None]
        score = jnp.where(valid, jnp.where(forced, FORCED_SCORE, imp), INVALID_SCORE)
        _, sel = lax.top_k(score, top)
        k_g = jax.vmap(lambda kk, ii: kk[ii])(ks[b], sel)
        v_g = jax.vmap(lambda vv, ii: vv[ii])(vs[b], sel)
        pos = sel[..., None] * SEL_BLOCK + jnp.arange(SEL_BLOCK)
        dist_s = t[None, :, None, None] - pos
        s = jnp.einsum('qghd,gqnkd->ghqnk', qi, k_g).astype(jnp.float32) - slopes[:, :, None, None, None] * dist_s[:, None].astype(jnp.float32)
        p = masked_softmax(s, (dist_s >= 0)[:, None], (-2, -1))
        o_sel = jnp.einsum('ghqnk,gqnkd->qghd', p, v_g)
        kwb = lax.dynamic_slice(kw, (b, q0, 0, 0), (1, span, G, dh))[0]
        vwb = lax.dynamic_slice(vw, (b, q0, 0, 0), (1, span, G, dh))[0]
        s_pos = q0 - WINDOW + jnp.arange(span)
        dist_w = t[:, None] - s_pos[None, :]
        mask_w = (dist_w >= 0) & (dist_w < WINDOW) & (s_pos >= 0)[None, :]
        s = jnp.einsum('qghd,kgd->ghqk', qi, kwb).astype(jnp.float32) - slopes[:, :, None, None] * dist_w.astype(jnp.float32)
        p = masked_softmax(s, mask_w, -1)
        o_win = jnp.einsum('ghqk,kgd->qghd', p, vwb)
        gb = lax.dynamic_slice(gates, (b, q0, 0, 0, 0), (1, Q_BLOCK, G, HPG, 3))[0]
        o = gb[..., 0:1] * o_cmp + gb[..., 1:2] * o_sel + gb[..., 2:3] * o_win
        return o.astype(h.dtype)

    o = lax.map(block_fn, jnp.arange(B * n_qb))
    return o.reshape(B, S, D) @ w_out


def setup_inputs(seed: int = 0) -> dict:
    key = jax.random.key(seed)
    ks = jax.random.split(key, 20)

    def w(k, shape, fan_in, gain=1.0):
        return jax.random.normal(k, shape, jnp.float32) * (gain * fan_in ** -0.5)

    D = D_MODEL
    dh = NSA_HEAD_DIM
    return {
        'x': jax.random.normal(ks[0], (BATCH, SEQ, D), jnp.float32),
        'c': jax.random.normal(ks[1], (BATCH, D), jnp.float32),
        'ada_w': w(ks[2], (DEPTH, D, 9 * D), D, ADA_GAIN),
        'ada_b': 0.01 * jax.random.normal(ks[3], (DEPTH, 9 * D), jnp.float32),
        'ln_g': 1.0 + 0.02 * jax.random.normal(ks[4], (DEPTH, 3, D), jnp.float32),
        'ln_b': 0.02 * jax.random.normal(ks[5], (DEPTH, 3, D), jnp.float32),
        'ffn_w_in': w(ks[6], (DEPTH, 2, D, 2 * D_FF), D),
        'ffn_w_out': w(ks[7], (DEPTH, 2, D_FF, D), D_FF, DN_BETA),
        'ret_w_in': w(ks[8], (N_RET_LAYERS, D, RET_IN_DIM), D),
        'ret_w_out': w(ks[9], (N_RET_LAYERS, RET_HEADS * RET_V_DIM, D), RET_HEADS * RET_V_DIM, DN_BETA),
        'nsa_w_in': w(ks[10], (N_NSA_LAYERS, D, NSA_IN_DIM), D),
        'nsa_w_out': w(ks[11], (N_NSA_LAYERS, D, D), D, DN_BETA),
        'nsa_pe_k': 0.1 * jax.random.normal(ks[12], (N_NSA_LAYERS, CMP_BLOCK, dh), jnp.float32),
        'nsa_pe_v': 0.1 * jax.random.normal(ks[13], (N_NSA_LAYERS, CMP_BLOCK, dh), jnp.float32),
        'nsa_ck_w1': w(ks[14], (N_NSA_LAYERS, CMP_BLOCK * dh, CMP_HIDDEN), CMP_BLOCK * dh),
        'nsa_ck_w2': w(ks[15], (N_NSA_LAYERS, CMP_HIDDEN, dh), CMP_HIDDEN),
        'nsa_cv_w1': w(ks[16], (N_NSA_LAYERS, CMP_BLOCK * dh, CMP_HIDDEN), CMP_BLOCK * dh),
        'nsa_cv_w2': w(ks[17], (N_NSA_LAYERS, CMP_HIDDEN, dh), CMP_HIDDEN),
    }


def reference(x, c, ada_w, ada_b, ln_g, ln_b, ffn_w_in, ffn_w_out, ret_w_in, ret_w_out,
              nsa_w_in, nsa_w_out, nsa_pe_k, nsa_pe_v, nsa_ck_w1, nsa_ck_w2, nsa_cv_w1, nsa_cv_w2):
    c_act = jax.nn.silu(c)
    for i in range(DEPTH):
        mod = c_act @ ada_w[i] + ada_b[i]
        sh0, sc0, g0, sh1, sc1, g1, sh2, sc2, g2 = jnp.split(mod, 9, axis=-1)
        y = swiglu(modulate(x, sh0, sc0), ffn_w_in[i, 0], ffn_w_out[i, 0])
        x = post_norm_residual(x, y, g0, FFN_RES, ln_g[i, 0], ln_b[i, 0])
        h = modulate(x, sh1, sc1)
        j = i // N_MIXERS
        if i % N_MIXERS == 0:
            y = retention(h, ret_w_in[j], ret_w_out[j])
        else:
            y = nsa(h, nsa_w_in[j], nsa_w_out[j], nsa_pe_k[j], nsa_pe_v[j],
                    nsa_ck_w1[j], nsa_ck_w2[j], nsa_cv_w1[j], nsa_cv_w2[j])
        x = post_norm_residual(x, y, g1, 1.0, ln_g[i, 1], ln_b[i, 1])
        y = swiglu(modulate(x, sh2, sc2), ffn_w_in[i, 1], ffn_w_out[i, 1])
        x = post_norm_residual(x, y, g2, FFN_RES, ln_g[i, 2], ln_b[i, 2])
    return x
```

```python
import functools

import jax
import jax.numpy as jnp
from jax import lax
from jax.experimental import pallas as pl
from jax.experimental.pallas import tpu as pltpu

F32 = jnp.float32
BF16 = jnp.bfloat16

DEPTH = 4
N_MIXERS = 2
RET_HEADS = 4
NSA_HEADS = 16
NSA_GROUPS = 4
NSA_HPG = NSA_HEADS // NSA_GROUPS
CMP_BLOCK = 32
CMP_STRIDE = 16
SEL_BLOCK = 64
SEL_TOP = 16
WINDOW = 512
Q_BLOCK = 128
FORCED_SCORE = 1e4
INVALID_SCORE = -1.0
FFN_RES = 0.5
DN_ALPHA = (2 * DEPTH) ** 0.25
LN_EPS = 1e-5
GN_EPS = 1e-6
NEG_INF = -1e30

V7X_VMEM_LIMIT_BYTES = 56 * 1024 * 1024
MXU_WIDTH = 256
TOKEN_TILE = 512
RET_CHUNK = 256
SEL_KEY_TILE = 256
WIN_KEY_TILE = 128


def _dot(a, b):
    return jnp.dot(a, b, preferred_element_type=F32)


def _sigmoid(x):
    return 1.0 / (1.0 + jnp.exp(-x))


def _layer_norm(z, g, b):
    mu = jnp.mean(z, -1, keepdims=True)
    zc = z - mu
    var = jnp.mean(zc * zc, -1, keepdims=True)
    return zc * lax.rsqrt(var + LN_EPS) * g + b


def _params(*sem):
    return pltpu.CompilerParams(dimension_semantics=sem, vmem_limit_bytes=V7X_VMEM_LIMIT_BYTES)


def _resident(shape):
    return pl.BlockSpec(shape, lambda *_: (0,) * len(shape), pipeline_mode=pl.Buffered(1))


def _ada_kernel(c_ref, w_ref, b_ref, o_ref):
    c = c_ref[...]
    c_act = (c * _sigmoid(c)).astype(BF16)
    o_ref[0] = _dot(c_act, w_ref[0].astype(BF16)) + b_ref[0]


def _ada_mod(c, ada_w, ada_b):
    depth, d, n = ada_w.shape
    b = c.shape[0]
    tn = n // 8
    return pl.pallas_call(
        _ada_kernel,
        grid=(depth, n // tn),
        in_specs=[
            pl.BlockSpec((b, d), lambda l, j: (0, 0)),
            pl.BlockSpec((1, d, tn), lambda l, j: (l, 0, j)),
            pl.BlockSpec((1, 1, tn), lambda l, j: (l, 0, j)),
        ],
        out_specs=pl.BlockSpec((1, b, tn), lambda l, j: (l, 0, j)),
        out_shape=jax.ShapeDtypeStruct((depth, b, n), F32),
        compiler_params=_params("parallel", "parallel"),
        name="ada_mod",
    )(c, ada_w, ada_b.reshape(depth, 1, n))


def _ffn_kernel(x_ref, mod_ref, wa_ref, wu_ref, wo_ref, lng_ref, lnb_ref, o_ref, h_ref, acc_ref, *, row0, n_chunks):
    x = x_ref[...]
    shift = mod_ref[0, row0:row0 + 1, :]
    scale = mod_ref[0, row0 + 1:row0 + 2, :]
    gate = mod_ref[0, row0 + 2:row0 + 3, :]
    h_ref[...] = (x * (1.0 + scale) + shift).astype(BF16)
    acc_ref[...] = jnp.zeros_like(acc_ref)

    def chunk(j, carry):
        h = h_ref[...]
        a = _dot(h, wa_ref[j])
        u = _dot(h, wu_ref[j])
        act = (a * _sigmoid(a) * u).astype(BF16)
        acc_ref[...] += _dot(act, wo_ref[j])
        return carry

    lax.fori_loop(0, n_chunks, chunk, 0)
    z = DN_ALPHA * x + FFN_RES * (1.0 + gate) * acc_ref[...]
    o_ref[...] = _layer_norm(z, lng_ref[...], lnb_ref[...])


def _ffn(x, mod, row0, wa, wu, wo, ln_g, ln_b, seq):
    n_tok, d = x.shape
    n_chunks, _, fc = wa.shape
    tm = min(TOKEN_TILE, seq)
    tiles_per_seq = seq // tm
    return pl.pallas_call(
        functools.partial(_ffn_kernel, row0=row0, n_chunks=n_chunks),
        grid=(n_tok // tm,),
        in_specs=[
            pl.BlockSpec((tm, d), lambda i: (i, 0)),
            pl.BlockSpec((1, 9, d), lambda i: (i // tiles_per_seq, 0, 0)),
            _resident((n_chunks, d, fc)),
            _resident((n_chunks, d, fc)),
            _resident((n_chunks, fc, d)),
            _resident((1, d)),
            _resident((1, d)),
        ],
        out_specs=pl.BlockSpec((tm, d), lambda i: (i, 0)),
        out_shape=jax.ShapeDtypeStruct((n_tok, d), F32),
        scratch_shapes=[pltpu.VMEM((tm, d), BF16), pltpu.VMEM((tm, d), F32)],
        compiler_params=_params("parallel"),
        name="ffn",
    )(x, mod, wa, wu, wo, ln_g, ln_b)


def _ffn_weights(w_in, w_out):
    d, two_f = w_in.shape
    f = two_f // 2
    n_chunks = f // MXU_WIDTH
    w = w_in.astype(BF16).reshape(d, 2, n_chunks, MXU_WIDTH)
    wa = w[:, 0].transpose(1, 0, 2)
    wu = w[:, 1].transpose(1, 0, 2)
    wo = w_out.astype(BF16).reshape(n_chunks, MXU_WIDTH, d)
    return wa, wu, wo


def _proj_kernel(x_ref, mod_ref, w_ref, o_ref, *, row0, tn):
    x = x_ref[...]
    shift = mod_ref[0, row0:row0 + 1, :]
    scale = mod_ref[0, row0 + 1:row0 + 2, :]
    h = (x * (1.0 + scale) + shift).astype(BF16)
    for j in range(o_ref.shape[1] // tn):
        o_ref[:, j * tn:(j + 1) * tn] = _dot(h, w_ref[:, j * tn:(j + 1) * tn]).astype(o_ref.dtype)


def _proj(x, mod, row0, w, seq, tn):
    n_tok, d = x.shape
    n = w.shape[1]
    tm = min(TOKEN_TILE, seq)
    tiles_per_seq = seq // tm
    return pl.pallas_call(
        functools.partial(_proj_kernel, row0=row0, tn=tn),
        grid=(n_tok // tm,),
        in_specs=[
            pl.BlockSpec((tm, d), lambda i: (i, 0)),
            pl.BlockSpec((1, 9, d), lambda i: (i // tiles_per_seq, 0, 0)),
            _resident((d, n)),
        ],
        out_specs=pl.BlockSpec((tm, n), lambda i: (i, 0)),
        out_shape=jax.ShapeDtypeStruct((n_tok, n), BF16),
        compiler_params=_params("parallel"),
        name="mixer_in_proj",
    )(x, mod, w)


def _nsa_proj_kernel(x_ref, mod_ref, w_ref, wg_ref, o_ref, og_ref, *, row0, tn, d_q, q_scale):
    x = x_ref[...]
    shift = mod_ref[0, row0:row0 + 1, :]
    scale = mod_ref[0, row0 + 1:row0 + 2, :]
    h = (x * (1.0 + scale) + shift).astype(BF16)
    for j in range(o_ref.shape[1] // tn):
        y = _dot(h, w_ref[:, j * tn:(j + 1) * tn])
        if (j + 1) * tn <= d_q:
            y = y * q_scale
        o_ref[:, j * tn:(j + 1) * tn] = y.astype(o_ref.dtype)
    og_ref[...] = _sigmoid(_dot(h, wg_ref[...]))


def _nsa_proj(x, mod, row0, w, wg, seq, d_q, q_scale):
    n_tok, d = x.shape
    n = w.shape[1]
    ng = wg.shape[1]
    tm = min(TOKEN_TILE, seq)
    tiles_per_seq = seq // tm
    return pl.pallas_call(
        functools.partial(_nsa_proj_kernel, row0=row0, tn=MXU_WIDTH, d_q=d_q, q_scale=q_scale),
        grid=(n_tok // tm,),
        in_specs=[
            pl.BlockSpec((tm, d), lambda i: (i, 0)),
            pl.BlockSpec((1, 9, d), lambda i: (i // tiles_per_seq, 0, 0)),
            _resident((d, n)),
            _resident((d, ng)),
        ],
        out_specs=[pl.BlockSpec((tm, n), lambda i: (i, 0)), pl.BlockSpec((tm, ng), lambda i: (i, 0))],
        out_shape=[jax.ShapeDtypeStruct((n_tok, n), BF16), jax.ShapeDtypeStruct((n_tok, ng), F32)],
        compiler_params=_params("parallel"),
        name="nsa_in_proj",
    )(x, mod, w, wg)


def _out_kernel(y_ref, x_ref, mod_ref, w_ref, lng_ref, lnb_ref, o_ref, *, row_gate):
    gate = mod_ref[0, row_gate:row_gate + 1, :]
    y = _dot(y_ref[...], w_ref[...])
    z = DN_ALPHA * x_ref[...] + (1.0 + gate) * y
    o_ref[...] = _layer_norm(z, lng_ref[...], lnb_ref[...])


def _out_proj(y, x, mod, row_gate, w, ln_g, ln_b, seq):
    n_tok, d = x.shape
    k = y.shape[1]
    tm = min(TOKEN_TILE, seq)
    tiles_per_seq = seq // tm
    return pl.pallas_call(
        functools.partial(_out_kernel, row_gate=row_gate),
        grid=(n_tok // tm,),
        in_specs=[
            pl.BlockSpec((tm, k), lambda i: (i, 0)),
            pl.BlockSpec((tm, d), lambda i: (i, 0)),
            pl.BlockSpec((1, 9, d), lambda i: (i // tiles_per_seq, 0, 0)),
            _resident((k, d)),
            _resident((1, d)),
            _resident((1, d)),
        ],
        out_specs=pl.BlockSpec((tm, d), lambda i: (i, 0)),
        out_shape=jax.ShapeDtypeStruct((n_tok, d), F32),
        compiler_params=_params("parallel"),
        name="mixer_out_proj",
    )(y, x, mod, w, ln_g, ln_b)


def _ret_kernel(q_ref, k_ref, v_ref, g_ref, o_ref, state_ref, *, chunk, n_chunks, k_scale):
    head = pl.program_id(1).astype(F32)
    log_g = jnp.log(1.0 - jnp.exp2(-5.0 - (jnp.zeros((1, 1), F32) + head)))
    row = lax.broadcasted_iota(jnp.int32, (chunk, chunk), 0)
    col = lax.broadcasted_iota(jnp.int32, (chunk, chunk), 1)
    diff = (row - col).astype(F32)
    decay_intra = jnp.where(diff >= 0, jnp.exp(log_g * jnp.maximum(diff, 0.0)), 0.0)
    pos = lax.broadcasted_iota(jnp.int32, (chunk, 1), 0).astype(F32)
    decay_q = jnp.exp(log_g * (pos + 1.0))
    decay_k = jnp.exp(log_g * (chunk - 1.0 - pos))
    decay_state = jnp.exp(log_g * float(chunk))
    state_ref[...] = jnp.zeros_like(state_ref)

    def step(c, carry):
        rows = pl.ds(pl.multiple_of(c * chunk, chunk), chunk)
        q = q_ref[0, rows, :]
        k = k_ref[0, rows, :] * k_scale
        v = v_ref[0, rows, :]
        scores = lax.dot_general(q, k, (((1,), (1,)), ((), ())), preferred_element_type=F32) * decay_intra
        state = state_ref[...]
        o = _dot(scores.astype(BF16), v) + _dot((q.astype(F32) * decay_q).astype(BF16), state.astype(BF16))
        k_dec_t = (k.astype(F32) * decay_k).T.astype(BF16)
        state_ref[...] = decay_state * state + _dot(k_dec_t, v)
        mu = jnp.mean(o, -1, keepdims=True)
        oc = o - mu
        var = jnp.mean(oc * oc, -1, keepdims=True)
        o = oc * lax.rsqrt(var + GN_EPS)
        g = g_ref[0, rows, :].astype(F32)
        o_ref[0, rows, :] = (o * (g * _sigmoid(g))).astype(o_ref.dtype)
        return carry

    lax.fori_loop(0, n_chunks, step, 0)


def _retention_core(proj, batch, seq, dk, dv):
    heads = RET_HEADS
    chunk = min(RET_CHUNK, seq)
    k_blk0 = heads * dk // dk
    v_blk0 = 2 * heads * dk // dv
    g_blk0 = (2 * heads * dk + heads * dv) // dv
    return pl.pallas_call(
        functools.partial(_ret_kernel, chunk=chunk, n_chunks=seq // chunk, k_scale=dk ** -0.5),
        grid=(batch, heads),
        in_specs=[
            pl.BlockSpec((1, seq, dk), lambda b, h: (b, 0, h)),
            pl.BlockSpec((1, seq, dk), lambda b, h: (b, 0, k_blk0 + h)),
            pl.BlockSpec((1, seq, dv), lambda b, h: (b, 0, v_blk0 + h)),
            pl.BlockSpec((1, seq, dv), lambda b, h: (b, 0, g_blk0 + h)),
        ],
        out_specs=pl.BlockSpec((1, seq, dv), lambda b, h: (b, 0, h)),
        out_shape=jax.ShapeDtypeStruct((batch, seq, heads * dv), BF16),
        scratch_shapes=[pltpu.VMEM((dk, dv), F32)],
        compiler_params=_params("parallel", "parallel"),
        name="retention_core",
    )(proj, proj, proj, proj)


def _gelu_tanh(x):
    return 0.5 * x * (1.0 + jnp.tanh(0.7978845608028654 * (x + 0.044715 * (x * x * x))))


def _compress_kernel(yk_ref, yv_ref, pek_ref, pev_ref, k1_ref, k2_ref, v1_ref, v2_ref, ok_ref, ov_ref):
    def mlp(y_ref, pe_ref, w1_ref, w2_ref):
        y = y_ref[0, 0]
        half = y.shape[1]
        n_rows = y.shape[0]
        first = _dot(y, w1_ref[0:half, :])
        second = _dot(y, w1_ref[half:2 * half, :])
        bias = _dot(pe_ref[...].astype(BF16), w1_ref[...])
        hidden = first + pltpu.roll(second, n_rows - 1, 0) + bias
        return _dot(_gelu_tanh(hidden).astype(BF16), w2_ref[...])

    ok_ref[0, 0] = mlp(yk_ref, pek_ref, k1_ref, k2_ref).astype(ok_ref.dtype)
    ov_ref[0, 0] = mlp(yv_ref, pev_ref, v1_ref, v2_ref).astype(ov_ref.dtype)


def _compress(yk, yv, pe_k, pe_v, k1, k2, v1, v2):
    batch, groups, rows, width = yk.shape
    hidden = k1.shape[1]
    dh = k2.shape[1]
    blk = pl.BlockSpec((1, 1, rows, width), lambda b, g: (b, g, 0, 0))
    out_blk = pl.BlockSpec((1, 1, rows, dh), lambda b, g: (b, g, 0, 0))
    out_sds = jax.ShapeDtypeStruct((batch, groups, rows, dh), BF16)
    return pl.pallas_call(
        _compress_kernel,
        grid=(batch, groups),
        in_specs=[blk, blk, _resident((1, 2 * width)), _resident((1, 2 * width)),
                  _resident((2 * width, hidden)), _resident((hidden, dh)),
                  _resident((2 * width, hidden)), _resident((hidden, dh))],
        out_specs=[out_blk, out_blk],
        out_shape=[out_sds, out_sds],
        compiler_params=_params("parallel", "parallel"),
        name="nsa_compress",
    )(yk, yv, pe_k, pe_v, k1, k2, v1, v2)


def _nsa_attn_kernel(qt_ref, kc_ref, vct_ref, ks_ref, vst_ref, kw_ref, vwt_ref, gates_ref, o_ref, sel_ref,
                     *, n_sel, top):
    group = pl.program_id(1)
    q0 = pl.program_id(2) * Q_BLOCK
    dh, width = qt_ref.shape[3], qt_ref.shape[4]
    qt = qt_ref[0, 0, 0]

    lane = lax.broadcasted_iota(jnp.int32, (1, width), 1)
    t_row = q0 + lane % Q_BLOCK
    head = (group * NSA_HPG + lane // Q_BLOCK).astype(F32)
    slope = jnp.exp2(-8.0 * (head + 1.0) / NSA_HEADS)

    def masked_scores(k_tile, first_pos):
        n_keys = k_tile.shape[0]
        pos = first_pos + lax.broadcasted_iota(jnp.int32, (n_keys, width), 0)
        dist = t_row - pos
        return _dot(k_tile, qt) - slope * dist.astype(F32), dist

    kc = kc_ref[0, 0]
    n_cmp_pad = kc.shape[0]
    blk = lax.broadcasted_iota(jnp.int32, (n_cmp_pad, width), 0)
    dist_c = t_row - (blk * CMP_STRIDE + (CMP_BLOCK - 1))
    mask_c = dist_c >= 0
    s = jnp.where(mask_c, _dot(kc, qt) - slope * dist_c.astype(F32), NEG_INF)
    e = jnp.exp(s - jnp.max(s, 0, keepdims=True))
    p_cmp = jnp.where(mask_c, e / jnp.sum(e, 0, keepdims=True), 0.0)
    o_cmp = _dot(vct_ref[0, 0], p_cmp.astype(BF16))

    p_sum = p_cmp[:, 0:Q_BLOCK]
    for h in range(1, NSA_HPG):
        p_sum = p_sum + p_cmp[:, h * Q_BLOCK:(h + 1) * Q_BLOCK]
    sel_i = lax.broadcasted_iota(jnp.int32, (n_sel, n_cmp_pad), 0) * SEL_BLOCK
    cmp_i = lax.broadcasted_iota(jnp.int32, (n_sel, n_cmp_pad), 1) * CMP_STRIDE
    overlap = jnp.maximum(jnp.minimum(cmp_i + CMP_BLOCK, sel_i + SEL_BLOCK) - jnp.maximum(cmp_i, sel_i), 0)
    overlap = (overlap.astype(F32) * (1.0 / CMP_BLOCK)).astype(BF16)
    p_hi = p_sum.astype(BF16)
    rest = p_sum - p_hi.astype(F32)
    p_mid = rest.astype(BF16)
    p_lo = (rest - p_mid.astype(F32)).astype(BF16)
    imp = _dot(overlap, p_hi) + _dot(overlap, p_mid) + _dot(overlap, p_lo)

    j_idx = lax.broadcasted_iota(jnp.int32, (n_sel, Q_BLOCK), 0)
    t_sel = q0 + lax.broadcasted_iota(jnp.int32, (n_sel, Q_BLOCK), 1)
    cur = t_sel // SEL_BLOCK
    forced = (j_idx == 0) | (j_idx == cur) | (j_idx == cur - 1)
    valid = j_idx * SEL_BLOCK <= t_sel
    score = jnp.where(valid, jnp.where(forced, FORCED_SCORE, imp), INVALID_SCORE)
    rank = jnp.zeros((n_sel, Q_BLOCK), F32)
    for jp in range(n_sel):
        other = score[jp:jp + 1, :]
        ahead = (other > score) | ((other == score) & (j_idx > jp))
        rank = rank + jnp.where(ahead, 1.0, 0.0)
    sel_ref[...] = jnp.where(rank < top, 1.0, 0.0)

    def attend(k_ref, vt_ref, tile, lo, hi, mask_fn):
        def step(kt, carry):
            m, l, acc = carry
            rows = pl.ds(pl.multiple_of(kt * tile, tile), tile)
            sc, dist = masked_scores(k_ref[0, 0, rows, :], kt * tile)
            mask = mask_fn(kt, dist)
            sc = jnp.where(mask, sc, NEG_INF)
            m_new = jnp.maximum(m, jnp.max(sc, 0, keepdims=True))
            alpha = jnp.exp(m - m_new)
            pe = jnp.where(mask, jnp.exp(sc - m_new), 0.0)
            l = alpha * l + jnp.sum(pe, 0, keepdims=True)
            acc = alpha * acc + _dot(vt_ref[0, 0, kt], pe.astype(BF16))
            return m_new, l, acc

        init = (jnp.full((1, width), NEG_INF, F32), jnp.zeros((1, width), F32), jnp.zeros((dh, width), F32))
        _, l, acc = lax.fori_loop(lo, hi, step, init)
        return acc / l

    blocks_per_tile = SEL_KEY_TILE // SEL_BLOCK

    def sel_mask(kt, dist):
        parts = []
        for r in range(blocks_per_tile):
            chosen = sel_ref[pl.ds(kt * blocks_per_tile + r, 1), :]
            parts.append(jnp.broadcast_to(chosen, (SEL_BLOCK, Q_BLOCK)))
        per_query = jnp.concatenate(parts, 0)
        chosen_all = jnp.concatenate([per_query] * NSA_HPG, 1)
        return (chosen_all > 0.5) & (dist >= 0)

    n_sel_tiles = (q0 + Q_BLOCK + SEL_KEY_TILE - 1) // SEL_KEY_TILE
    o_sel = attend(ks_ref, vst_ref, SEL_KEY_TILE, 0, n_sel_tiles, sel_mask)

    def win_mask(kt, dist):
        return (dist >= 0) & (dist < WINDOW)

    first_tile = jnp.maximum(q0 - WINDOW, 0) // WIN_KEY_TILE
    last_tile = (q0 + Q_BLOCK) // WIN_KEY_TILE
    o_win = attend(kw_ref, vwt_ref, WIN_KEY_TILE, first_tile, last_tile, win_mask)

    gates = gates_ref[0, 0, 0]
    o = gates[0:1, :] * o_cmp + gates[1:2, :] * o_sel + gates[2:3, :] * o_win
    o_ref[0, 0, 0] = o.astype(o_ref.dtype)


def _nsa_attention(qt, kc, vct, ks, vst, kw, vwt, gates, seq):
    batch, groups, n_qb, dh, width = qt.shape
    n_sel = seq // SEL_BLOCK
    top = min(SEL_TOP, n_sel)
    per_bg = lambda shape: pl.BlockSpec((1, 1) + shape, lambda b, g, i: (b, g) + (0,) * len(shape))
    per_q = lambda shape: pl.BlockSpec((1, 1, 1) + shape, lambda b, g, i: (b, g, i) + (0,) * len(shape))
    return pl.pallas_call(
        functools.partial(_nsa_attn_kernel, n_sel=n_sel, top=top),
        grid=(batch, groups, n_qb),
        in_specs=[
            per_q((dh, width)),
            per_bg(kc.shape[2:]), per_bg(vct.shape[2:]),
            per_bg(ks.shape[2:]), per_bg(vst.shape[2:]),
            per_bg(kw.shape[2:]), per_bg(vwt.shape[2:]),
            per_q((3, width)),
        ],
        out_specs=per_q((dh, width)),
        out_shape=jax.ShapeDtypeStruct(qt.shape, BF16),
        scratch_shapes=[pltpu.VMEM((n_sel, Q_BLOCK), F32)],
        compiler_params=_params("parallel", "parallel", "arbitrary"),
        name="nsa_attention",
    )(qt, kc, vct, ks, vst, kw, vwt, gates)


def _nsa_mixer(x, mod, w_main, w_gate, pe_k, pe_v, k1, k2, v1, v2, batch, seq):
    d = x.shape[1]
    groups, hpg = NSA_GROUPS, NSA_HPG
    dh = d // NSA_HEADS
    kv = groups * dh
    main, gates = _nsa_proj(x, mod, 3, w_main, w_gate, seq, d, dh ** -0.5)
    main = main.reshape(batch, seq, -1)
    n_qb = seq // Q_BLOCK
    width = hpg * Q_BLOCK

    def part(j):
        lo = d + j * kv
        return main[:, :, lo:lo + kv]

    qt = main[:, :, :d].reshape(batch, n_qb, Q_BLOCK, groups, hpg, dh).transpose(0, 3, 1, 5, 4, 2)
    qt = qt.reshape(batch, groups, n_qb, dh, width)

    def cmp_rows(t):
        rows = seq // CMP_STRIDE
        return t.reshape(batch, rows, CMP_STRIDE, groups, dh).transpose(0, 3, 1, 2, 4).reshape(batch, groups, rows, CMP_STRIDE * dh)

    def key_rows(t):
        return t.reshape(batch, seq, groups, dh).transpose(0, 2, 1, 3)

    def value_cols(t, tile):
        return t.reshape(batch, seq // tile, tile, groups, dh).transpose(0, 3, 1, 4, 2)

    kc, vc = _compress(cmp_rows(part(0)), cmp_rows(part(1)), pe_k, pe_v, k1, k2, v1, v2)
    vct = vc.transpose(0, 1, 3, 2)
    g = gates[:, :NSA_HEADS * 3].reshape(batch, n_qb, Q_BLOCK, groups, hpg, 3).transpose(0, 3, 1, 5, 4, 2)
    g = g.reshape(batch, groups, n_qb, 3, width)
    ot = _nsa_attention(qt, kc, vct, key_rows(part(2)), value_cols(part(3), SEL_KEY_TILE),
                        key_rows(part(4)), value_cols(part(5), WIN_KEY_TILE), g, seq)
    o = ot.reshape(batch, groups, n_qb, dh, hpg, Q_BLOCK).transpose(0, 2, 5, 1, 4, 3)
    return o.reshape(batch * seq, d)


def kernel(x, c, ada_w, ada_b, ln_g, ln_b, ffn_w_in, ffn_w_out, ret_w_in, ret_w_out, nsa_w_in, nsa_w_out,
           nsa_pe_k, nsa_pe_v, nsa_ck_w1, nsa_ck_w2, nsa_cv_w1, nsa_cv_w2):
    batch, seq, d = x.shape
    depth = ada_w.shape[0]
    assert depth == DEPTH
    mod_all = _ada_mod(c, ada_w, ada_b).reshape(depth, batch, 9, d)
    xf = x.reshape(batch * seq, d)
    dk = d // RET_HEADS
    dv = 2 * dk
    dh = d // NSA_HEADS
    n_main = d + 6 * NSA_GROUPS * dh
    lane = 128
    for i in range(depth):
        mod = mod_all[i]
        ln = lambda k: (ln_g[i, k].reshape(1, d), ln_b[i, k].reshape(1, d))
        xf = _ffn(xf, mod, 0, *_ffn_weights(ffn_w_in[i, 0], ffn_w_out[i, 0]), *ln(0), seq)
        j = i // N_MIXERS
        if i % N_MIXERS == 0:
            proj = _proj(xf, mod, 3, ret_w_in[j].astype(BF16), seq, 2 * MXU_WIDTH)
            y = _retention_core(proj.reshape(batch, seq, -1), batch, seq, dk, dv).reshape(batch * seq, -1)
            w_out = ret_w_out[j]
        else:
            w_in = nsa_w_in[j].astype(BF16)
            n_gate = w_in.shape[1] - n_main
            w_gate = jnp.pad(w_in[:, n_main:], ((0, 0), (0, -n_gate % lane)))
            y = _nsa_mixer(xf, mod, w_in[:, :n_main], w_gate,
                           nsa_pe_k[j].reshape(1, -1), nsa_pe_v[j].reshape(1, -1),
                           nsa_ck_w1[j].astype(BF16), nsa_ck_w2[j].astype(BF16),
                           nsa_cv_w1[j].astype(BF16), nsa_cv_w2[j].astype(BF16), batch, seq)
            w_out = nsa_w_out[j]
        xf = _out_proj(y, xf, mod, 5, w_out.astype(BF16), *ln(1), seq)
        xf = _ffn(xf, mod, 6, *_ffn_weights(ffn_w_in[i, 1], ffn_w_out[i, 1]), *ln(2), seq)
    return xf.reshape(batch, seq, d)
```

```python
import functools

import numpy as np
import jax
import jax.numpy as jnp
from jax import lax
from jax.experimental import pallas as pl
from jax.experimental.pallas import tpu as pltpu

F32 = jnp.float32
BF16 = jnp.bfloat16

DEPTH = 4
N_MIXERS = 2
RET_HEADS = 4
NSA_HEADS = 16
NSA_GROUPS = 4
NSA_HPG = NSA_HEADS // NSA_GROUPS
CMP_BLOCK = 32
CMP_STRIDE = 16
SEL_BLOCK = 64
SEL_TOP = 16
WINDOW = 512
Q_BLOCK = 128
FORCED_SCORE = 1e4
INVALID_SCORE = -1.0
FFN_RES = 0.5
DN_ALPHA = (2 * DEPTH) ** 0.25
LN_EPS = 1e-5
GN_EPS = 1e-6
NEG_INF = -1e30

V7X_VMEM_LIMIT_BYTES = 56 * 1024 * 1024
MXU_WIDTH = 256
TOKEN_TILE = 512
RET_CHUNK = 256
SEL_KEY_TILE = 256
WIN_KEY_TILE = 128
KEY_WIDTH = 128
POS_ROWS = 16
LOG2_E = 1.4426950408889634


def _dot(a, b):
    return jnp.dot(a, b, preferred_element_type=F32)


def _sigmoid(x):
    return 1.0 / (1.0 + jnp.exp(-x))


def _layer_norm(z, g, b):
    mu = jnp.mean(z, -1, keepdims=True)
    zc = z - mu
    var = jnp.mean(zc * zc, -1, keepdims=True)
    return zc * lax.rsqrt(var + LN_EPS) * g + b


def _params(*sem):
    return pltpu.CompilerParams(dimension_semantics=sem, vmem_limit_bytes=V7X_VMEM_LIMIT_BYTES)


def _resident(shape):
    return pl.BlockSpec(shape, lambda *_: (0,) * len(shape), pipeline_mode=pl.Buffered(1))


def _ada_kernel(c_ref, w_ref, b_ref, o_ref):
    c = c_ref[...]
    c_act = (c * _sigmoid(c)).astype(BF16)
    o_ref[0] = _dot(c_act, w_ref[0].astype(BF16)) + b_ref[0]


def _ada_mod(c, ada_w, ada_b):
    depth, d, n = ada_w.shape
    b = c.shape[0]
    tn = n // 8
    return pl.pallas_call(
        _ada_kernel,
        grid=(depth, n // tn),
        in_specs=[
            pl.BlockSpec((b, d), lambda l, j: (0, 0)),
            pl.BlockSpec((1, d, tn), lambda l, j: (l, 0, j)),
            pl.BlockSpec((1, 1, tn), lambda l, j: (l, 0, j)),
        ],
        out_specs=pl.BlockSpec((1, b, tn), lambda l, j: (l, 0, j)),
        out_shape=jax.ShapeDtypeStruct((depth, b, n), F32),
        compiler_params=_params("parallel", "parallel"),
        name="ada_mod",
    )(c, ada_w, ada_b.reshape(depth, 1, n))


def _ffn_kernel(x_ref, mod_ref, wa_ref, wu_ref, wo_ref, lng_ref, lnb_ref, o_ref, h_ref, acc_ref, *, row0, n_chunks):
    x = x_ref[...]
    shift = mod_ref[0, row0:row0 + 1, :]
    scale = mod_ref[0, row0 + 1:row0 + 2, :]
    gate = mod_ref[0, row0 + 2:row0 + 3, :]
    h_ref[...] = (x * (1.0 + scale) + shift).astype(BF16)
    acc_ref[...] = jnp.zeros_like(acc_ref)

    def chunk(j, carry):
        h = h_ref[...]
        a = _dot(h, wa_ref[j])
        u = _dot(h, wu_ref[j])
        act = (a * _sigmoid(a) * u).astype(BF16)
        acc_ref[...] += _dot(act, wo_ref[j])
        return carry

    lax.fori_loop(0, n_chunks, chunk, 0)
    z = DN_ALPHA * x + FFN_RES * (1.0 + gate) * acc_ref[...]
    o_ref[...] = _layer_norm(z, lng_ref[...], lnb_ref[...])


def _ffn(x, mod, row0, wa, wu, wo, ln_g, ln_b, seq):
    n_tok, d = x.shape
    n_chunks, _, fc = wa.shape
    tm = min(TOKEN_TILE, seq)
    tiles_per_seq = seq // tm
    return pl.pallas_call(
        functools.partial(_ffn_kernel, row0=row0, n_chunks=n_chunks),
        grid=(n_tok // tm,),
        in_specs=[
            pl.BlockSpec((tm, d), lambda i: (i, 0)),
            pl.BlockSpec((1, 9, d), lambda i: (i // tiles_per_seq, 0, 0)),
            _resident((n_chunks, d, fc)),
            _resident((n_chunks, d, fc)),
            _resident((n_chunks, fc, d)),
            _resident((1, d)),
            _resident((1, d)),
        ],
        out_specs=pl.BlockSpec((tm, d), lambda i: (i, 0)),
        out_shape=jax.ShapeDtypeStruct((n_tok, d), F32),
        scratch_shapes=[pltpu.VMEM((tm, d), BF16), pltpu.VMEM((tm, d), F32)],
        compiler_params=_params("parallel"),
        name="ffn",
    )(x, mod, wa, wu, wo, ln_g, ln_b)


def _ffn_weights(w_in, w_out):
    d, two_f = w_in.shape
    f = two_f // 2
    n_chunks = f // MXU_WIDTH
    w = w_in.astype(BF16).reshape(d, 2, n_chunks, MXU_WIDTH)
    wa = w[:, 0].transpose(1, 0, 2)
    wu = w[:, 1].transpose(1, 0, 2)
    wo = w_out.astype(BF16).reshape(n_chunks, MXU_WIDTH, d)
    return wa, wu, wo


def _proj_kernel(x_ref, mod_ref, w_ref, o_ref, *, row0, tn):
    x = x_ref[...]
    shift = mod_ref[0, row0:row0 + 1, :]
    scale = mod_ref[0, row0 + 1:row0 + 2, :]
    h = (x * (1.0 + scale) + shift).astype(BF16)
    for j in range(o_ref.shape[1] // tn):
        o_ref[:, j * tn:(j + 1) * tn] = _dot(h, w_ref[:, j * tn:(j + 1) * tn]).astype(o_ref.dtype)


def _proj(x, mod, row0, w, seq, tn):
    n_tok, d = x.shape
    n = w.shape[1]
    tm = min(TOKEN_TILE, seq)
    tiles_per_seq = seq // tm
    return pl.pallas_call(
        functools.partial(_proj_kernel, row0=row0, tn=tn),
        grid=(n_tok // tm,),
        in_specs=[
            pl.BlockSpec((tm, d), lambda i: (i, 0)),
            pl.BlockSpec((1, 9, d), lambda i: (i // tiles_per_seq, 0, 0)),
            _resident((d, n)),
        ],
        out_specs=pl.BlockSpec((tm, n), lambda i: (i, 0)),
        out_shape=jax.ShapeDtypeStruct((n_tok, n), BF16),
        compiler_params=_params("parallel"),
        name="mixer_in_proj",
    )(x, mod, w)


def _nsa_proj_kernel(x_ref, mod_ref, w_ref, wg_ref, o_ref, og_ref, *, row0, tn, d_q, q_scale):
    x = x_ref[...]
    shift = mod_ref[0, row0:row0 + 1, :]
    scale = mod_ref[0, row0 + 1:row0 + 2, :]
    h = (x * (1.0 + scale) + shift).astype(BF16)
    for j in range(o_ref.shape[1] // tn):
        y = _dot(h, w_ref[:, j * tn:(j + 1) * tn])
        if (j + 1) * tn <= d_q:
            y = y * q_scale
        o_ref[:, j * tn:(j + 1) * tn] = y.astype(o_ref.dtype)
    og_ref[...] = _sigmoid(_dot(h, wg_ref[...]))


def _nsa_proj(x, mod, row0, w, wg, seq, d_q, q_scale):
    n_tok, d = x.shape
    n = w.shape[1]
    ng = wg.shape[1]
    tm = min(TOKEN_TILE, seq)
    tiles_per_seq = seq // tm
    return pl.pallas_call(
        functools.partial(_nsa_proj_kernel, row0=row0, tn=MXU_WIDTH, d_q=d_q, q_scale=q_scale),
        grid=(n_tok // tm,),
        in_specs=[
            pl.BlockSpec((tm, d), lambda i: (i, 0)),
            pl.BlockSpec((1, 9, d), lambda i: (i // tiles_per_seq, 0, 0)),
            _resident((d, n)),
            _resident((d, ng)),
        ],
        out_specs=[pl.BlockSpec((tm, n), lambda i: (i, 0)), pl.BlockSpec((tm, ng), lambda i: (i, 0))],
        out_shape=[jax.ShapeDtypeStruct((n_tok, n), BF16), jax.ShapeDtypeStruct((n_tok, ng), F32)],
        compiler_params=_params("parallel"),
        name="nsa_in_proj",
    )(x, mod, w, wg)


def _out_kernel(y_ref, x_ref, mod_ref, w_ref, lng_ref, lnb_ref, o_ref, *, row_gate):
    gate = mod_ref[0, row_gate:row_gate + 1, :]
    y = _dot(y_ref[...], w_ref[...])
    z = DN_ALPHA * x_ref[...] + (1.0 + gate) * y
    o_ref[...] = _layer_norm(z, lng_ref[...], lnb_ref[...])


def _out_proj(y, x, mod, row_gate, w, ln_g, ln_b, seq):
    n_tok, d = x.shape
    k = y.shape[1]
    tm = min(TOKEN_TILE, seq)
    tiles_per_seq = seq // tm
    return pl.pallas_call(
        functools.partial(_out_kernel, row_gate=row_gate),
        grid=(n_tok // tm,),
        in_specs=[
            pl.BlockSpec((tm, k), lambda i: (i, 0)),
            pl.BlockSpec((tm, d), lambda i: (i, 0)),
            pl.BlockSpec((1, 9, d), lambda i: (i // tiles_per_seq, 0, 0)),
            _resident((k, d)),
            _resident((1, d)),
            _resident((1, d)),
        ],
        out_specs=pl.BlockSpec((tm, d), lambda i: (i, 0)),
        out_shape=jax.ShapeDtypeStruct((n_tok, d), F32),
        compiler_params=_params("parallel"),
        name="mixer_out_proj",
    )(y, x, mod, w, ln_g, ln_b)


def _ret_kernel(q_ref, k_ref, v_ref, g_ref, o_ref, state_ref, *, chunk, n_chunks, k_scale):
    head = pl.program_id(1).astype(F32)
    log_g = jnp.log(1.0 - jnp.exp2(-5.0 - (jnp.zeros((1, 1), F32) + head)))
    row = lax.broadcasted_iota(jnp.int32, (chunk, chunk), 0)
    col = lax.broadcasted_iota(jnp.int32, (chunk, chunk), 1)
    diff = (row - col).astype(F32)
    decay_intra = jnp.where(diff >= 0, jnp.exp(log_g * jnp.maximum(diff, 0.0)), 0.0)
    pos = lax.broadcasted_iota(jnp.int32, (chunk, 1), 0).astype(F32)
    decay_q = jnp.exp(log_g * (pos + 1.0))
    decay_k = jnp.exp(log_g * (chunk - 1.0 - pos))
    decay_state = jnp.exp(log_g * float(chunk))
    state_ref[...] = jnp.zeros_like(state_ref)

    def step(c, carry):
        rows = pl.ds(pl.multiple_of(c * chunk, chunk), chunk)
        q = q_ref[0, rows, :]
        k = k_ref[0, rows, :] * k_scale
        v = v_ref[0, rows, :]
        scores = lax.dot_general(q, k, (((1,), (1,)), ((), ())), preferred_element_type=F32) * decay_intra
        state = state_ref[...]
        o = _dot(scores.astype(BF16), v) + _dot((q.astype(F32) * decay_q).astype(BF16), state.astype(BF16))
        k_dec_t = (k.astype(F32) * decay_k).T.astype(BF16)
        state_ref[...] = decay_state * state + _dot(k_dec_t, v)
        mu = jnp.mean(o, -1, keepdims=True)
        oc = o - mu
        var = jnp.mean(oc * oc, -1, keepdims=True)
        o = oc * lax.rsqrt(var + GN_EPS)
        g = g_ref[0, rows, :].astype(F32)
        o_ref[0, rows, :] = (o * (g * _sigmoid(g))).astype(o_ref.dtype)
        return carry

    lax.fori_loop(0, n_chunks, step, 0)


def _retention_core(proj, batch, seq, dk, dv):
    heads = RET_HEADS
    chunk = min(RET_CHUNK, seq)
    k_blk0 = heads * dk // dk
    v_blk0 = 2 * heads * dk // dv
    g_blk0 = (2 * heads * dk + heads * dv) // dv
    return pl.pallas_call(
        functools.partial(_ret_kernel, chunk=chunk, n_chunks=seq // chunk, k_scale=dk ** -0.5),
        grid=(batch, heads),
        in_specs=[
            pl.BlockSpec((1, seq, dk), lambda b, h: (b, 0, h)),
            pl.BlockSpec((1, seq, dk), lambda b, h: (b, 0, k_blk0 + h)),
            pl.BlockSpec((1, seq, dv), lambda b, h: (b, 0, v_blk0 + h)),
            pl.BlockSpec((1, seq, dv), lambda b, h: (b, 0, g_blk0 + h)),
        ],
        out_specs=pl.BlockSpec((1, seq, dv), lambda b, h: (b, 0, h)),
        out_shape=jax.ShapeDtypeStruct((batch, seq, heads * dv), BF16),
        scratch_shapes=[pltpu.VMEM((dk, dv), F32)],
        compiler_params=_params("parallel", "parallel"),
        name="retention_core",
    )(proj, proj, proj, proj)


def _gelu_tanh(x):
    return 0.5 * x * (1.0 + jnp.tanh(0.7978845608028654 * (x + 0.044715 * (x * x * x))))


def _compress_kernel(yk_ref, yv_ref, pek_ref, pev_ref, k1_ref, k2_ref, v1_ref, v2_ref, ok_ref, ov_ref):
    def mlp(y_ref, pe_ref, w1_ref, w2_ref):
        y = y_ref[0, 0]
        half = y.shape[1]
        n_rows = y.shape[0]
        first = _dot(y, w1_ref[0:half, :])
        second = _dot(y, w1_ref[half:2 * half, :])
        bias = _dot(pe_ref[...].astype(BF16), w1_ref[...])
        hidden = first + pltpu.roll(second, n_rows - 1, 0) + bias
        return _dot(_gelu_tanh(hidden).astype(BF16), w2_ref[...])

    ok_ref[0, 0] = mlp(yk_ref, pek_ref, k1_ref, k2_ref).astype(ok_ref.dtype)
    ov_ref[0, 0] = mlp(yv_ref, pev_ref, v1_ref, v2_ref).astype(ov_ref.dtype)


def _compress(yk, yv, pe_k, pe_v, k1, k2, v1, v2):
    batch, groups, rows, width = yk.shape
    hidden = k1.shape[1]
    dh = k2.shape[1]
    blk = pl.BlockSpec((1, 1, rows, width), lambda b, g: (b, g, 0, 0))
    out_blk = pl.BlockSpec((1, 1, rows, dh), lambda b, g: (b, g, 0, 0))
    out_sds = jax.ShapeDtypeStruct((batch, groups, rows, dh), BF16)
    return pl.pallas_call(
        _compress_kernel,
        grid=(batch, groups),
        in_specs=[blk, blk, _resident((1, 2 * width)), _resident((1, 2 * width)),
                  _resident((2 * width, hidden)), _resident((hidden, dh)),
                  _resident((2 * width, hidden)), _resident((hidden, dh))],
        out_specs=[out_blk, out_blk],
        out_shape=[out_sds, out_sds],
        compiler_params=_params("parallel", "parallel"),
        name="nsa_compress",
    )(yk, yv, pe_k, pe_v, k1, k2, v1, v2)


def _nsa_attn_kernel(qt_ref, kc_ref, vct_ref, ks_ref, vst_ref, kw_ref, vwt_ref, gates_ref, o_ref,
                     qa_ref, s_sel_ref, s_win_ref, *, n_sel, top):
    q0 = pl.program_id(1) * Q_BLOCK
    groups, dh, width = qt_ref.shape[1], qt_ref.shape[3], qt_ref.shape[4]
    sel_row0 = KEY_WIDTH - n_sel
    all_groups = range(groups)

    lane = lax.broadcasted_iota(jnp.int32, (1, width), 1)
    t_row = q0 + lane % Q_BLOCK
    piece_row = lax.broadcasted_iota(jnp.int32, (POS_ROWS, width), 0)

    for g in all_groups:
        head = (g * NSA_HPG + lane // Q_BLOCK).astype(F32)
        slope = jnp.exp2(-8.0 * (head + 1.0) / NSA_HEADS) * LOG2_E
        hi = slope.astype(BF16).astype(F32)
        rest = slope - hi
        mid = rest.astype(BF16).astype(F32)
        lo = (rest - mid).astype(BF16).astype(F32)
        feat = jnp.zeros((POS_ROWS, width), F32)
        for idx, piece in enumerate((SEL_BLOCK * hi, SEL_BLOCK * mid, SEL_BLOCK * lo, hi, mid, lo)):
            feat = jnp.where(piece_row == idx, piece, feat)
        qa_ref[g, 0:dh, :] = qt_ref[0, g, 0]
        qa_ref[g, dh:dh + POS_ROWS, :] = feat.astype(BF16)
        qa_ref[g, dh + POS_ROWS:KEY_WIDTH, :] = jnp.zeros((KEY_WIDTH - dh - POS_ROWS, width), BF16)

    n_cmp_pad = kc_ref.shape[2]
    blk = lax.broadcasted_iota(jnp.int32, (n_cmp_pad, width), 0)
    mask_c = blk * CMP_STRIDE + (CMP_BLOCK - 1) <= t_row
    p_cmp, o_cmp = [], []
    for g in all_groups:
        s = jnp.where(mask_c, _dot(kc_ref[0, g], qa_ref[g]), NEG_INF)
        e = jnp.exp2(s - jnp.max(s, 0, keepdims=True))
        p = jnp.where(mask_c, e * (1.0 / jnp.sum(e, 0, keepdims=True)), 0.0)
        p_cmp.append(p)
        o_cmp.append(_dot(vct_ref[0, g], p.astype(BF16)))

    sel_i = lax.broadcasted_iota(jnp.int32, (n_sel, n_cmp_pad), 0) * SEL_BLOCK
    cmp_i = lax.broadcasted_iota(jnp.int32, (n_sel, n_cmp_pad), 1) * CMP_STRIDE
    overlap = jnp.maximum(jnp.minimum(cmp_i + CMP_BLOCK, sel_i + SEL_BLOCK) - jnp.maximum(cmp_i, sel_i), 0)
    overlap = (overlap.astype(F32) * (1.0 / CMP_BLOCK)).astype(BF16)
    j_idx = lax.broadcasted_iota(jnp.int32, (n_sel, Q_BLOCK), 0)
    t_sel = q0 + lax.broadcasted_iota(jnp.int32, (n_sel, Q_BLOCK), 1)
    cur = t_sel // SEL_BLOCK
    forced = (j_idx == 0) | (j_idx == cur) | (j_idx == cur - 1)
    valid = j_idx * SEL_BLOCK <= t_sel
    for g in all_groups:
        p_sum = p_cmp[g][:, 0:Q_BLOCK]
        for h in range(1, NSA_HPG):
            p_sum = p_sum + p_cmp[g][:, h * Q_BLOCK:(h + 1) * Q_BLOCK]
        p_hi = p_sum.astype(BF16)
        rest = p_sum - p_hi.astype(F32)
        p_mid = rest.astype(BF16)
        p_lo = (rest - p_mid.astype(F32)).astype(BF16)
        imp = _dot(overlap, p_hi) + _dot(overlap, p_mid) + _dot(overlap, p_lo)
        score = jnp.where(valid, jnp.where(forced, FORCED_SCORE, imp), INVALID_SCORE)
        tiles = [score[r:r + 8, :] for r in range(0, n_sel, 8)]
        ranks = [jnp.zeros((8, Q_BLOCK), F32) for _ in tiles]
        for jp in range(n_sel):
            other = jnp.broadcast_to(score[jp:jp + 1, :], (8, Q_BLOCK))
            for v, tile_scores in enumerate(tiles):
                if jp < 8 * v:
                    ahead = other >= tile_scores
                elif jp >= 8 * v + 7:
                    ahead = other > tile_scores
                else:
                    ahead = (other > tile_scores) | ((other == tile_scores) & (j_idx[0:8, :] > jp - 8 * v))
                ranks[v] = ranks[v] + jnp.where(ahead, 1.0, 0.0)
        rank = jnp.concatenate(ranks, 0)
        block_bias = jnp.where(rank < top, 0.0, NEG_INF)
        qa_ref[g, sel_row0:KEY_WIDTH, :] = jnp.concatenate([block_bias] * NSA_HPG, 1).astype(BF16)

    def row_max(sc):
        return jnp.max(sc.reshape(sc.shape[0] // 8, 8, width), 0)

    def softmax_values(s_ref, vt_ref, first_tile, n_tiles, m8):
        m = [jnp.max(m8[g], 0, keepdims=True) for g in all_groups]

        def step(i, carry):
            out = []
            for g in all_groups:
                l8, acc = carry[g]
                pe = jnp.exp2(s_ref[g, i] - m[g])
                l8 = l8 + jnp.sum(pe.reshape(pe.shape[0] // 8, 8, width), 0)
                out.append((l8, acc + _dot(vt_ref[0, g, first_tile + i], pe.astype(BF16))))
            return tuple(out)

        init = tuple((jnp.zeros((8, width), F32), jnp.zeros((dh, width), F32)) for _ in all_groups)
        res = lax.fori_loop(0, n_tiles, step, init)
        return [acc * (1.0 / jnp.sum(l8, 0, keepdims=True)) for l8, acc in res]

    m8_init = tuple(jnp.full((8, width), NEG_INF, F32) for _ in all_groups)

    tile = SEL_KEY_TILE

    def sel_scores(g, kt):
        rows = pl.ds(pl.multiple_of(kt * tile, tile), tile)
        return _dot(ks_ref[0, g, rows, :], qa_ref[g])

    def sel_pass(kt, m8):
        out = []
        for g in all_groups:
            sc = sel_scores(g, kt)
            s_sel_ref[g, kt] = sc
            out.append(jnp.maximum(m8[g], row_max(sc)))
        return tuple(out)

    last = (q0 + Q_BLOCK - 1) // tile
    m8 = list(lax.fori_loop(0, last, sel_pass, m8_init))
    visible = last * tile + lax.broadcasted_iota(jnp.int32, (tile, width), 0) <= t_row
    for g in all_groups:
        sc = jnp.where(visible, sel_scores(g, last), NEG_INF)
        s_sel_ref[g, last] = sc
        m8[g] = jnp.maximum(m8[g], row_max(sc))
    o_sel = softmax_values(s_sel_ref, vst_ref, 0, last + 1, m8)

    tile = WIN_KEY_TILE
    first = jnp.maximum(q0 - WINDOW, 0) // tile
    n_win = (q0 + Q_BLOCK - 1) // tile - first + 1
    key_row = lax.broadcasted_iota(jnp.int32, (tile, width), 0)

    def win_pass(i, m8):
        kt = first + i
        rows = pl.ds(pl.multiple_of(kt * tile, tile), tile)
        dist = (t_row - kt * tile) - key_row
        in_window = (dist >= 0) & (dist < WINDOW)
        out = []
        for g in all_groups:
            sc = jnp.where(in_window, _dot(kw_ref[0, g, rows, :], qa_ref[g]), NEG_INF)
            s_win_ref[g, i] = sc
            out.append(jnp.maximum(m8[g], row_max(sc)))
        return tuple(out)

    m8 = lax.fori_loop(0, n_win, win_pass, m8_init)
    o_win = softmax_values(s_win_ref, vwt_ref, first, n_win, m8)

    for g in all_groups:
        gates = gates_ref[0, g, 0]
        o = gates[0:1, :] * o_cmp[g] + gates[1:2, :] * o_sel[g] + gates[2:3, :] * o_win[g]
        o_ref[0, g, 0] = o.astype(o_ref.dtype)


def _nsa_attention(qt, kc, vct, ks, vst, kw, vwt, gates, seq):
    batch, groups, n_qb, dh, width = qt.shape
    n_sel = seq // SEL_BLOCK
    top = min(SEL_TOP, n_sel)
    assert dh + POS_ROWS + n_sel <= KEY_WIDTH
    per_b = lambda shape: pl.BlockSpec((1,) + shape, lambda b, i: (b,) + (0,) * len(shape))
    per_q = lambda shape: pl.BlockSpec((1, groups, 1) + shape, lambda b, i: (b, 0, i) + (0,) * len(shape))
    n_win_tiles = min(WINDOW + Q_BLOCK, seq) // WIN_KEY_TILE
    return pl.pallas_call(
        functools.partial(_nsa_attn_kernel, n_sel=n_sel, top=top),
        grid=(batch, n_qb),
        in_specs=[
            per_q((dh, width)),
            per_b(kc.shape[1:]), per_b(vct.shape[1:]),
            per_b(ks.shape[1:]), per_b(vst.shape[1:]),
            per_b(kw.shape[1:]), per_b(vwt.shape[1:]),
            per_q((3, width)),
        ],
        out_specs=per_q((dh, width)),
        out_shape=jax.ShapeDtypeStruct(qt.shape, BF16),
        scratch_shapes=[pltpu.VMEM((groups, KEY_WIDTH, width), BF16),
                        pltpu.VMEM((groups, seq // SEL_KEY_TILE, SEL_KEY_TILE, width), F32),
                        pltpu.VMEM((groups, n_win_tiles, WIN_KEY_TILE, width), F32)],
        compiler_params=_params("parallel", "arbitrary"),
        name="nsa_attention",
    )(qt, kc, vct, ks, vst, kw, vwt, gates)


def _key_features(pos, dh, n_sel):
    block, offset = pos // SEL_BLOCK, pos % SEL_BLOCK
    feat = np.zeros((pos.shape[0], KEY_WIDTH - dh), np.float32)
    feat[:, 0:3] = block[:, None]
    feat[:, 3:6] = offset[:, None]
    if n_sel:
        feat[:, KEY_WIDTH - dh - n_sel:] = block[:, None] == np.arange(n_sel)[None, :]
    return jnp.asarray(feat, BF16)


def _nsa_mixer(x, mod, w_main, w_gate, pe_k, pe_v, k1, k2, v1, v2, batch, seq):
    d = x.shape[1]
    groups, hpg = NSA_GROUPS, NSA_HPG
    dh = d // NSA_HEADS
    kv = groups * dh
    main, gates = _nsa_proj(x, mod, 3, w_main, w_gate, seq, d, dh ** -0.5 * LOG2_E)
    main = main.reshape(batch, seq, -1)
    n_qb = seq // Q_BLOCK
    n_sel = seq // SEL_BLOCK
    width = hpg * Q_BLOCK

    def part(j):
        lo = d + j * kv
        return main[:, :, lo:lo + kv]

    qt = main[:, :, :d].reshape(batch, n_qb, Q_BLOCK, groups, hpg, dh).transpose(0, 3, 1, 5, 4, 2)
    qt = qt.reshape(batch, groups, n_qb, dh, width)

    def cmp_rows(t):
        rows = seq // CMP_STRIDE
        return t.reshape(batch, rows, CMP_STRIDE, groups, dh).transpose(0, 3, 1, 2, 4).reshape(batch, groups, rows, CMP_STRIDE * dh)

    def with_features(keys, feat):
        return jnp.concatenate([keys, jnp.broadcast_to(feat, keys.shape[:2] + feat.shape)], -1)

    def key_rows(t, feat):
        return with_features(t.reshape(batch, seq, groups, dh).transpose(0, 2, 1, 3), feat)

    def value_cols(t, tile):
        return t.reshape(batch, seq // tile, tile, groups, dh).transpose(0, 3, 1, 4, 2)

    kc, vc = _compress(cmp_rows(part(0)), cmp_rows(part(1)), pe_k, pe_v, k1, k2, v1, v2)
    cmp_end = np.arange(seq // CMP_STRIDE) * CMP_STRIDE + CMP_BLOCK - 1
    kc = with_features(kc, _key_features(cmp_end, dh, 0))
    vct = vc.transpose(0, 1, 3, 2)
    g = gates[:, :NSA_HEADS * 3].reshape(batch, n_qb, Q_BLOCK, groups, hpg, 3).transpose(0, 3, 1, 5, 4, 2)
    g = g.reshape(batch, groups, n_qb, 3, width)
    positions = np.arange(seq)
    ot = _nsa_attention(qt, kc, vct,
                        key_rows(part(2), _key_features(positions, dh, n_sel)), value_cols(part(3), SEL_KEY_TILE),
                        key_rows(part(4), _key_features(positions, dh, 0)), value_cols(part(5), WIN_KEY_TILE), g, seq)
    o = ot.reshape(batch, groups, n_qb, dh, hpg, Q_BLOCK).transpose(0, 2, 5, 1, 4, 3)
    return o.reshape(batch * seq, d)


def kernel(x, c, ada_w, ada_b, ln_g, ln_b, ffn_w_in, ffn_w_out, ret_w_in, ret_w_out, nsa_w_in, nsa_w_out,
           nsa_pe_k, nsa_pe_v, nsa_ck_w1, nsa_ck_w2, nsa_cv_w1, nsa_cv_w2):
    batch, seq, d = x.shape
    depth = ada_w.shape[0]
    assert depth == DEPTH
    mod_all = _ada_mod(c, ada_w, ada_b).reshape(depth, batch, 9, d)
    xf = x.reshape(batch * seq, d)
    dk = d // RET_HEADS
    dv = 2 * dk
    dh = d // NSA_HEADS
    n_main = d + 6 * NSA_GROUPS * dh
    lane = 128
    for i in range(depth):
        mod = mod_all[i]
        ln = lambda k: (ln_g[i, k].reshape(1, d), ln_b[i, k].reshape(1, d))
        xf = _ffn(xf, mod, 0, *_ffn_weights(ffn_w_in[i, 0], ffn_w_out[i, 0]), *ln(0), seq)
        j = i // N_MIXERS
        if i % N_MIXERS == 0:
            proj = _proj(xf, mod, 3, ret_w_in[j].astype(BF16), seq, 2 * MXU_WIDTH)
            y = _retention_core(proj.reshape(batch, seq, -1), batch, seq, dk, dv).reshape(batch * seq, -1)
            w_out = ret_w_out[j]
        else:
            w_in = nsa_w_in[j].astype(BF16)
            n_gate = w_in.shape[1] - n_main
            w_gate = jnp.pad(w_in[:, n_main:], ((0, 0), (0, -n_gate % lane)))
            y = _nsa_mixer(xf, mod, w_in[:, :n_main], w_gate,
                           nsa_pe_k[j].reshape(1, -1), nsa_pe_v[j].reshape(1, -1),
                           nsa_ck_w1[j].astype(BF16), nsa_ck_w2[j].astype(BF16),
                           nsa_cv_w1[j].astype(BF16), nsa_cv_w2[j].astype(BF16), batch, seq)
            w_out = nsa_w_out[j]
        xf = _out_proj(y, xf, mod, 5, w_out.astype(BF16), *ln(1), seq)
        xf = _ffn(xf, mod, 6, *_ffn_weights(ffn_w_in[i, 1], ffn_w_out[i, 1]), *ln(2), seq)
    return xf.reshape(batch, seq, d)
```

```python
import functools

import numpy as np
import jax
import jax.numpy as jnp
from jax import lax
from jax.experimental import pallas as pl
from jax.experimental.pallas import tpu as pltpu

F32 = jnp.float32
BF16 = jnp.bfloat16

DEPTH = 4
N_MIXERS = 2
RET_HEADS = 4
NSA_HEADS = 16
NSA_GROUPS = 4
NSA_HPG = NSA_HEADS // NSA_GROUPS
CMP_BLOCK = 32
CMP_STRIDE = 16
SEL_BLOCK = 64
SEL_TOP = 16
WINDOW = 512
Q_BLOCK = 128
FORCED_SCORE = 1e4
INVALID_SCORE = -1.0
FFN_RES = 0.5
DN_ALPHA = (2 * DEPTH) ** 0.25
LN_EPS = 1e-5
GN_EPS = 1e-6
NEG_INF = -1e30

V7X_VMEM_LIMIT_BYTES = 56 * 1024 * 1024
MXU_WIDTH = 256
TOKEN_TILE = 512
RET_CHUNK = 256
SEL_KEY_TILE = 256
WIN_KEY_TILE = 128
KEY_WIDTH = 128
POS_ROWS = 16
LOG2_E = 1.4426950408889634


def _dot(a, b):
    return jnp.dot(a, b, preferred_element_type=F32)


def _sigmoid(x):
    return 1.0 / (1.0 + jnp.exp(-x))


def _layer_norm(z, g, b):
    mu = jnp.mean(z, -1, keepdims=True)
    zc = z - mu
    var = jnp.mean(zc * zc, -1, keepdims=True)
    return zc * lax.rsqrt(var + LN_EPS) * g + b


def _params(*sem):
    return pltpu.CompilerParams(dimension_semantics=sem, vmem_limit_bytes=V7X_VMEM_LIMIT_BYTES)


def _resident(shape):
    return pl.BlockSpec(shape, lambda *_: (0,) * len(shape), pipeline_mode=pl.Buffered(1))


def _ada_kernel(c_ref, w_ref, b_ref, o_ref):
    c = c_ref[...]
    c_act = (c * _sigmoid(c)).astype(BF16)
    o_ref[0] = _dot(c_act, w_ref[0].astype(BF16)) + b_ref[0]


def _ada_mod(c, ada_w, ada_b):
    depth, d, n = ada_w.shape
    b = c.shape[0]
    tn = n // 8
    return pl.pallas_call(
        _ada_kernel,
        grid=(depth, n // tn),
        in_specs=[
            pl.BlockSpec((b, d), lambda l, j: (0, 0)),
            pl.BlockSpec((1, d, tn), lambda l, j: (l, 0, j)),
            pl.BlockSpec((1, 1, tn), lambda l, j: (l, 0, j)),
        ],
        out_specs=pl.BlockSpec((1, b, tn), lambda l, j: (l, 0, j)),
        out_shape=jax.ShapeDtypeStruct((depth, b, n), F32),
        compiler_params=_params("parallel", "parallel"),
        name="ada_mod",
    )(c, ada_w, ada_b.reshape(depth, 1, n))


def _ffn_kernel(x_ref, mod_ref, wa_ref, wu_ref, wo_ref, lng_ref, lnb_ref, o_ref, h_ref, act_ref, *, row0, n_chunks):
    x = x_ref[...]
    shift = mod_ref[0, row0:row0 + 1, :]
    scale = mod_ref[0, row0 + 1:row0 + 2, :]
    gate = mod_ref[0, row0 + 2:row0 + 3, :]
    h_ref[...] = (x * (1.0 + scale) + shift).astype(BF16)
    fc = wa_ref.shape[2]
    for j in range(n_chunks):
        h = h_ref[...]
        a = _dot(h, wa_ref[j])
        u = _dot(h, wu_ref[j])
        act_ref[:, j * fc:(j + 1) * fc] = (a * _sigmoid(a) * u).astype(BF16)
    y = _dot(act_ref[...], wo_ref[...])
    z = DN_ALPHA * x + FFN_RES * (1.0 + gate) * y
    o_ref[...] = _layer_norm(z, lng_ref[...], lnb_ref[...])


def _ffn(x, mod, row0, wa, wu, wo, ln_g, ln_b, seq):
    n_tok, d = x.shape
    n_chunks, _, fc = wa.shape
    tm = min(TOKEN_TILE, seq)
    tiles_per_seq = seq // tm
    return pl.pallas_call(
        functools.partial(_ffn_kernel, row0=row0, n_chunks=n_chunks),
        grid=(n_tok // tm,),
        in_specs=[
            pl.BlockSpec((tm, d), lambda i: (i, 0)),
            pl.BlockSpec((1, 9, d), lambda i: (i // tiles_per_seq, 0, 0)),
            _resident((n_chunks, d, fc)),
            _resident((n_chunks, d, fc)),
            _resident((n_chunks * fc, d)),
            _resident((1, d)),
            _resident((1, d)),
        ],
        out_specs=pl.BlockSpec((tm, d), lambda i: (i, 0)),
        out_shape=jax.ShapeDtypeStruct((n_tok, d), F32),
        scratch_shapes=[pltpu.VMEM((tm, d), BF16), pltpu.VMEM((tm, n_chunks * fc), BF16)],
        compiler_params=_params("parallel"),
        name="ffn",
    )(x, mod, wa, wu, wo, ln_g, ln_b)


def _ffn_weights(w_in, w_out):
    d, two_f = w_in.shape
    f = two_f // 2
    n_chunks = f // MXU_WIDTH
    w = w_in.astype(BF16).reshape(d, 2, n_chunks, MXU_WIDTH)
    wa = w[:, 0].transpose(1, 0, 2)
    wu = w[:, 1].transpose(1, 0, 2)
    return wa, wu, w_out.astype(BF16)


def _proj_kernel(x_ref, mod_ref, w_ref, o_ref, *, row0, tn):
    x = x_ref[...]
    shift = mod_ref[0, row0:row0 + 1, :]
    scale = mod_ref[0, row0 + 1:row0 + 2, :]
    h = (x * (1.0 + scale) + shift).astype(BF16)
    for j in range(o_ref.shape[1] // tn):
        o_ref[:, j * tn:(j + 1) * tn] = _dot(h, w_ref[:, j * tn:(j + 1) * tn]).astype(o_ref.dtype)


def _proj(x, mod, row0, w, seq, tn):
    n_tok, d = x.shape
    n = w.shape[1]
    tm = min(TOKEN_TILE, seq)
    tiles_per_seq = seq // tm
    return pl.pallas_call(
        functools.partial(_proj_kernel, row0=row0, tn=tn),
        grid=(n_tok // tm,),
        in_specs=[
            pl.BlockSpec((tm, d), lambda i: (i, 0)),
            pl.BlockSpec((1, 9, d), lambda i: (i // tiles_per_seq, 0, 0)),
            _resident((d, n)),
        ],
        out_specs=pl.BlockSpec((tm, n), lambda i: (i, 0)),
        out_shape=jax.ShapeDtypeStruct((n_tok, n), BF16),
        compiler_params=_params("parallel"),
        name="mixer_in_proj",
    )(x, mod, w)


def _nsa_proj_kernel(x_ref, mod_ref, w_ref, wg_ref, o_ref, og_ref, *, row0, tn, d_q, q_scale):
    x = x_ref[...]
    shift = mod_ref[0, row0:row0 + 1, :]
    scale = mod_ref[0, row0 + 1:row0 + 2, :]
    h = (x * (1.0 + scale) + shift).astype(BF16)
    for j in range(o_ref.shape[1] // tn):
        y = _dot(h, w_ref[:, j * tn:(j + 1) * tn])
        if (j + 1) * tn <= d_q:
            y = y * q_scale
        o_ref[:, j * tn:(j + 1) * tn] = y.astype(o_ref.dtype)
    og_ref[...] = _sigmoid(_dot(h, wg_ref[...]))


def _nsa_proj(x, mod, row0, w, wg, seq, d_q, q_scale):
    n_tok, d = x.shape
    n = w.shape[1]
    ng = wg.shape[1]
    tm = min(TOKEN_TILE, seq)
    tiles_per_seq = seq // tm
    return pl.pallas_call(
        functools.partial(_nsa_proj_kernel, row0=row0, tn=MXU_WIDTH, d_q=d_q, q_scale=q_scale),
        grid=(n_tok // tm,),
        in_specs=[
            pl.BlockSpec((tm, d), lambda i: (i, 0)),
            pl.BlockSpec((1, 9, d), lambda i: (i // tiles_per_seq, 0, 0)),
            _resident((d, n)),
            _resident((d, ng)),
        ],
        out_specs=[pl.BlockSpec((tm, n), lambda i: (i, 0)), pl.BlockSpec((tm, ng), lambda i: (i, 0))],
        out_shape=[jax.ShapeDtypeStruct((n_tok, n), BF16), jax.ShapeDtypeStruct((n_tok, ng), F32)],
        compiler_params=_params("parallel"),
        name="nsa_in_proj",
    )(x, mod, w, wg)


def _out_kernel(y_ref, x_ref, mod_ref, w_ref, lng_ref, lnb_ref, o_ref, *, row_gate):
    gate = mod_ref[0, row_gate:row_gate + 1, :]
    y = _dot(y_ref[...], w_ref[...])
    z = DN_ALPHA * x_ref[...] + (1.0 + gate) * y
    o_ref[...] = _layer_norm(z, lng_ref[...], lnb_ref[...])


def _out_proj(y, x, mod, row_gate, w, ln_g, ln_b, seq):
    n_tok, d = x.shape
    k = y.shape[1]
    tm = min(TOKEN_TILE, seq)
    tiles_per_seq = seq // tm
    return pl.pallas_call(
        functools.partial(_out_kernel, row_gate=row_gate),
        grid=(n_tok // tm,),
        in_specs=[
            pl.BlockSpec((tm, k), lambda i: (i, 0)),
            pl.BlockSpec((tm, d), lambda i: (i, 0)),
            pl.BlockSpec((1, 9, d), lambda i: (i // tiles_per_seq, 0, 0)),
            _resident((k, d)),
            _resident((1, d)),
            _resident((1, d)),
        ],
        out_specs=pl.BlockSpec((tm, d), lambda i: (i, 0)),
        out_shape=jax.ShapeDtypeStruct((n_tok, d), F32),
        compiler_params=_params("parallel"),
        name="mixer_out_proj",
    )(y, x, mod, w, ln_g, ln_b)


def _ret_kernel(q_ref, k_ref, v_ref, g_ref, o_ref, state_ref, *, chunk, n_chunks, k_scale):
    head = pl.program_id(1).astype(F32)
    log_g = jnp.log(1.0 - jnp.exp2(-5.0 - (jnp.zeros((1, 1), F32) + head)))
    row = lax.broadcasted_iota(jnp.int32, (chunk, chunk), 0)
    col = lax.broadcasted_iota(jnp.int32, (chunk, chunk), 1)
    diff = (row - col).astype(F32)
    decay_intra = jnp.where(diff >= 0, jnp.exp(log_g * jnp.maximum(diff, 0.0)), 0.0)
    pos = lax.broadcasted_iota(jnp.int32, (chunk, 1), 0).astype(F32)
    decay_q = jnp.exp(log_g * (pos + 1.0))
    decay_k = jnp.exp(log_g * (chunk - 1.0 - pos))
    decay_state = jnp.exp(log_g * float(chunk))
    state_ref[...] = jnp.zeros_like(state_ref)

    def step(c, carry):
        rows = pl.ds(pl.multiple_of(c * chunk, chunk), chunk)
        q = q_ref[0, rows, :]
        k = k_ref[0, rows, :] * k_scale
        v = v_ref[0, rows, :]
        scores = lax.dot_general(q, k, (((1,), (1,)), ((), ())), preferred_element_type=F32) * decay_intra
        state = state_ref[...]
        o = _dot(scores.astype(BF16), v) + _dot((q.astype(F32) * decay_q).astype(BF16), state.astype(BF16))
        k_dec_t = (k.astype(F32) * decay_k).T.astype(BF16)
        state_ref[...] = decay_state * state + _dot(k_dec_t, v)
        mu = jnp.mean(o, -1, keepdims=True)
        oc = o - mu
        var = jnp.mean(oc * oc, -1, keepdims=True)
        o = oc * lax.rsqrt(var + GN_EPS)
        g = g_ref[0, rows, :].astype(F32)
        o_ref[0, rows, :] = (o * (g * _sigmoid(g))).astype(o_ref.dtype)
        return carry

    lax.fori_loop(0, n_chunks, step, 0)


def _retention_core(proj, batch, seq, dk, dv):
    heads = RET_HEADS
    chunk = min(RET_CHUNK, seq)
    k_blk0 = heads * dk // dk
    v_blk0 = 2 * heads * dk // dv
    g_blk0 = (2 * heads * dk + heads * dv) // dv
    return pl.pallas_call(
        functools.partial(_ret_kernel, chunk=chunk, n_chunks=seq // chunk, k_scale=dk ** -0.5),
        grid=(batch, heads),
        in_specs=[
            pl.BlockSpec((1, seq, dk), lambda b, h: (b, 0, h)),
            pl.BlockSpec((1, seq, dk), lambda b, h: (b, 0, k_blk0 + h)),
            pl.BlockSpec((1, seq, dv), lambda b, h: (b, 0, v_blk0 + h)),
            pl.BlockSpec((1, seq, dv), lambda b, h: (b, 0, g_blk0 + h)),
        ],
        out_specs=pl.BlockSpec((1, seq, dv), lambda b, h: (b, 0, h)),
        out_shape=jax.ShapeDtypeStruct((batch, seq, heads * dv), BF16),
        scratch_shapes=[pltpu.VMEM((dk, dv), F32)],
        compiler_params=_params("parallel", "parallel"),
        name="retention_core",
    )(proj, proj, proj, proj)


def _gelu_tanh(x):
    return 0.5 * x * (1.0 + jnp.tanh(0.7978845608028654 * (x + 0.044715 * (x * x * x))))


def _compress_kernel(yk_ref, yv_ref, pek_ref, pev_ref, k1_ref, k2_ref, v1_ref, v2_ref, ok_ref, ov_ref):
    def mlp(y_ref, pe_ref, w1_ref, w2_ref):
        y = y_ref[0, 0]
        half = y.shape[1]
        n_rows = y.shape[0]
        first = _dot(y, w1_ref[0:half, :])
        second = _dot(y, w1_ref[half:2 * half, :])
        bias = _dot(pe_ref[...].astype(BF16), w1_ref[...])
        hidden = first + pltpu.roll(second, n_rows - 1, 0) + bias
        return _dot(_gelu_tanh(hidden).astype(BF16), w2_ref[...])

    ok_ref[0, 0] = mlp(yk_ref, pek_ref, k1_ref, k2_ref).astype(ok_ref.dtype)
    ov_ref[0, 0] = mlp(yv_ref, pev_ref, v1_ref, v2_ref).astype(ov_ref.dtype)


def _compress(yk, yv, pe_k, pe_v, k1, k2, v1, v2):
    batch, groups, rows, width = yk.shape
    hidden = k1.shape[1]
    dh = k2.shape[1]
    blk = pl.BlockSpec((1, 1, rows, width), lambda b, g: (b, g, 0, 0))
    out_blk = pl.BlockSpec((1, 1, rows, dh), lambda b, g: (b, g, 0, 0))
    out_sds = jax.ShapeDtypeStruct((batch, groups, rows, dh), BF16)
    return pl.pallas_call(
        _compress_kernel,
        grid=(batch, groups),
        in_specs=[blk, blk, _resident((1, 2 * width)), _resident((1, 2 * width)),
                  _resident((2 * width, hidden)), _resident((hidden, dh)),
                  _resident((2 * width, hidden)), _resident((hidden, dh))],
        out_specs=[out_blk, out_blk],
        out_shape=[out_sds, out_sds],
        compiler_params=_params("parallel", "parallel"),
        name="nsa_compress",
    )(yk, yv, pe_k, pe_v, k1, k2, v1, v2)


def _nsa_attn_kernel(qt_ref, kc_ref, vct_ref, ks_ref, vst_ref, kw_ref, vwt_ref, gates_ref, o_ref,
                     qa_ref, s_sel_ref, s_win_ref, *, n_sel, top):
    q0 = pl.program_id(1) * Q_BLOCK
    groups, dh, width = qt_ref.shape[1], qt_ref.shape[3], qt_ref.shape[4]
    sel_row0 = KEY_WIDTH - n_sel
    all_groups = range(groups)

    lane = lax.broadcasted_iota(jnp.int32, (1, width), 1)
    t_row = q0 + lane % Q_BLOCK
    piece_row = lax.broadcasted_iota(jnp.int32, (POS_ROWS, width), 0)

    for g in all_groups:
        head = (g * NSA_HPG + lane // Q_BLOCK).astype(F32)
        slope = jnp.exp2(-8.0 * (head + 1.0) / NSA_HEADS) * LOG2_E
        hi = slope.astype(BF16).astype(F32)
        rest = slope - hi
        mid = rest.astype(BF16).astype(F32)
        lo = (rest - mid).astype(BF16).astype(F32)
        feat = jnp.zeros((POS_ROWS, width), F32)
        for idx, piece in enumerate((SEL_BLOCK * hi, SEL_BLOCK * mid, SEL_BLOCK * lo, hi, mid, lo)):
            feat = jnp.where(piece_row == idx, piece, feat)
        qa_ref[g, 0:dh, :] = qt_ref[0, g, 0]
        qa_ref[g, dh:dh + POS_ROWS, :] = feat.astype(BF16)
        qa_ref[g, dh + POS_ROWS:KEY_WIDTH, :] = jnp.zeros((KEY_WIDTH - dh - POS_ROWS, width), BF16)

    n_cmp_pad = kc_ref.shape[2]
    blk = lax.broadcasted_iota(jnp.int32, (n_cmp_pad, width), 0)
    mask_c = blk * CMP_STRIDE + (CMP_BLOCK - 1) <= t_row
    p_cmp, o_cmp = [], []
    for g in all_groups:
        s = jnp.where(mask_c, _dot(kc_ref[0, g], qa_ref[g]), NEG_INF)
        e = jnp.exp2(s - jnp.max(s, 0, keepdims=True))
        p = jnp.where(mask_c, e * (1.0 / jnp.sum(e, 0, keepdims=True)), 0.0)
        p_cmp.append(p)
        o_cmp.append(_dot(vct_ref[0, g], p.astype(BF16)))

    sel_i = lax.broadcasted_iota(jnp.int32, (n_sel, n_cmp_pad), 0) * SEL_BLOCK
    cmp_i = lax.broadcasted_iota(jnp.int32, (n_sel, n_cmp_pad), 1) * CMP_STRIDE
    overlap = jnp.maximum(jnp.minimum(cmp_i + CMP_BLOCK, sel_i + SEL_BLOCK) - jnp.maximum(cmp_i, sel_i), 0)
    overlap = (overlap.astype(F32) * (1.0 / CMP_BLOCK)).astype(BF16)
    j_idx = lax.broadcasted_iota(jnp.int32, (n_sel, Q_BLOCK), 0)
    t_sel = q0 + lax.broadcasted_iota(jnp.int32, (n_sel, Q_BLOCK), 1)
    cur = t_sel // SEL_BLOCK
    forced = (j_idx == 0) | (j_idx == cur) | (j_idx == cur - 1)
    valid = j_idx * SEL_BLOCK <= t_sel
    for g in all_groups:
        p_sum = p_cmp[g][:, 0:Q_BLOCK]
        for h in range(1, NSA_HPG):
            p_sum = p_sum + p_cmp[g][:, h * Q_BLOCK:(h + 1) * Q_BLOCK]
        p_hi = p_sum.astype(BF16)
        rest = p_sum - p_hi.astype(F32)
        p_mid = rest.astype(BF16)
        p_lo = (rest - p_mid.astype(F32)).astype(BF16)
        imp = _dot(overlap, p_hi) + _dot(overlap, p_mid) + _dot(overlap, p_lo)
        score = jnp.where(valid, jnp.where(forced, FORCED_SCORE, imp), INVALID_SCORE)
        tiles = [score[r:r + 8, :] for r in range(0, n_sel, 8)]
        ranks = [jnp.zeros((8, Q_BLOCK), F32) for _ in tiles]
        for jp in range(n_sel):
            other = jnp.broadcast_to(score[jp:jp + 1, :], (8, Q_BLOCK))
            for v, tile_scores in enumerate(tiles):
                if jp < 8 * v:
                    ahead = other >= tile_scores
                elif jp >= 8 * v + 7:
                    ahead = other > tile_scores
                else:
                    ahead = (other > tile_scores) | ((other == tile_scores) & (j_idx[0:8, :] > jp - 8 * v))
                ranks[v] = ranks[v] + jnp.where(ahead, 1.0, 0.0)
        rank = jnp.concatenate(ranks, 0)
        block_bias = jnp.where(rank < top, 0.0, NEG_INF)
        qa_ref[g, sel_row0:KEY_WIDTH, :] = jnp.concatenate([block_bias] * NSA_HPG, 1).astype(BF16)

    def row_max(sc):
        return jnp.max(sc.reshape(sc.shape[0] // 8, 8, width), 0)

    def softmax_values(s_ref, vt_ref, first_tile, n_tiles, m8):
        m = [jnp.max(m8[g], 0, keepdims=True) for g in all_groups]

        def step(i, carry):
            out = []
            for g in all_groups:
                l8, acc = carry[g]
                pe = jnp.exp2(s_ref[g, i] - m[g])
                l8 = l8 + jnp.sum(pe.reshape(pe.shape[0] // 8, 8, width), 0)
                out.append((l8, acc + _dot(vt_ref[0, g, first_tile + i], pe.astype(BF16))))
            return tuple(out)

        init = tuple((jnp.zeros((8, width), F32), jnp.zeros((dh, width), F32)) for _ in all_groups)
        res = lax.fori_loop(0, n_tiles, step, init)
        return [acc * (1.0 / jnp.sum(l8, 0, keepdims=True)) for l8, acc in res]

    m8_init = tuple(jnp.full((8, width), NEG_INF, F32) for _ in all_groups)

    tile = SEL_KEY_TILE

    def sel_scores(g, kt):
        rows = pl.ds(pl.multiple_of(kt * tile, tile), tile)
        return _dot(ks_ref[0, g, rows, :], qa_ref[g])

    def sel_pass(kt, m8):
        out = []
        for g in all_groups:
            sc = sel_scores(g, kt)
            s_sel_ref[g, kt] = sc
            out.append(jnp.maximum(m8[g], row_max(sc)))
        return tuple(out)

    last = (q0 + Q_BLOCK - 1) // tile
    m8 = list(lax.fori_loop(0, last, sel_pass, m8_init))
    visible = last * tile + lax.broadcasted_iota(jnp.int32, (tile, width), 0) <= t_row
    for g in all_groups:
        sc = jnp.where(visible, sel_scores(g, last), NEG_INF)
        s_sel_ref[g, last] = sc
        m8[g] = jnp.maximum(m8[g], row_max(sc))
    o_sel = softmax_values(s_sel_ref, vst_ref, 0, last + 1, m8)

    tile = WIN_KEY_TILE
    first = jnp.maximum(q0 - WINDOW, 0) // tile
    n_win = (q0 + Q_BLOCK - 1) // tile - first + 1
    key_row = lax.broadcasted_iota(jnp.int32, (tile, width), 0)

    def win_pass(i, m8):
        kt = first + i
        rows = pl.ds(pl.multiple_of(kt * tile, tile), tile)
        dist = (t_row - kt * tile) - key_row
        in_window = (dist >= 0) & (dist < WINDOW)
        out = []
        for g in all_groups:
            sc = jnp.where(in_window, _dot(kw_ref[0, g, rows, :], qa_ref[g]), NEG_INF)
            s_win_ref[g, i] = sc
            out.append(jnp.maximum(m8[g], row_max(sc)))
        return tuple(out)

    m8 = lax.fori_loop(0, n_win, win_pass, m8_init)
    o_win = softmax_values(s_win_ref, vwt_ref, first, n_win, m8)

    for g in all_groups:
        gates = gates_ref[0, g, 0]
        o = gates[0:1, :] * o_cmp[g] + gates[1:2, :] * o_sel[g] + gates[2:3, :] * o_win[g]
        o_ref[0, g, 0] = o.astype(o_ref.dtype)


def _nsa_attention(qt, kc, vct, ks, vst, kw, vwt, gates, seq):
    batch, groups, n_qb, dh, width = qt.shape
    n_sel = seq // SEL_BLOCK
    top = min(SEL_TOP, n_sel)
    assert dh + POS_ROWS + n_sel <= KEY_WIDTH
    per_b = lambda shape: pl.BlockSpec((1,) + shape, lambda b, i: (b,) + (0,) * len(shape))
    per_q = lambda shape: pl.BlockSpec((1, groups, 1) + shape, lambda b, i: (b, 0, i) + (0,) * len(shape))
    n_win_tiles = min(WINDOW + Q_BLOCK, seq) // WIN_KEY_TILE
    return pl.pallas_call(
        functools.partial(_nsa_attn_kernel, n_sel=n_sel, top=top),
        grid=(batch, n_qb),
        in_specs=[
            per_q((dh, width)),
            per_b(kc.shape[1:]), per_b(vct.shape[1:]),
            per_b(ks.shape[1:]), per_b(vst.shape[1:]),
            per_b(kw.shape[1:]), per_b(vwt.shape[1:]),
            per_q((3, width)),
        ],
        out_specs=per_q((dh, width)),
        out_shape=jax.ShapeDtypeStruct(qt.shape, BF16),
        scratch_shapes=[pltpu.VMEM((groups, KEY_WIDTH, width), BF16),
                        pltpu.VMEM((groups, seq // SEL_KEY_TILE, SEL_KEY_TILE, width), F32),
                        pltpu.VMEM((groups, n_win_tiles, WIN_KEY_TILE, width), F32)],
        compiler_params=_params("parallel", "arbitrary"),
        name="nsa_attention",
    )(qt, kc, vct, ks, vst, kw, vwt, gates)


def _key_features(pos, dh, n_sel):
    block, offset = pos // SEL_BLOCK, pos % SEL_BLOCK
    feat = np.zeros((pos.shape[0], KEY_WIDTH - dh), np.float32)
    feat[:, 0:3] = block[:, None]
    feat[:, 3:6] = offset[:, None]
    if n_sel:
        feat[:, KEY_WIDTH - dh - n_sel:] = block[:, None] == np.arange(n_sel)[None, :]
    return jnp.asarray(feat, BF16)


def _nsa_mixer(x, mod, w_main, w_gate, pe_k, pe_v, k1, k2, v1, v2, batch, seq):
    d = x.shape[1]
    groups, hpg = NSA_GROUPS, NSA_HPG
    dh = d // NSA_HEADS
    kv = groups * dh
    main, gates = _nsa_proj(x, mod, 3, w_main, w_gate, seq, d, dh ** -0.5 * LOG2_E)
    main = main.reshape(batch, seq, -1)
    n_qb = seq // Q_BLOCK
    n_sel = seq // SEL_BLOCK
    width = hpg * Q_BLOCK

    def part(j):
        lo = d + j * kv
        return main[:, :, lo:lo + kv]

    qt = main[:, :, :d].reshape(batch, n_qb, Q_BLOCK, groups, hpg, dh).transpose(0, 3, 1, 5, 4, 2)
    qt = qt.reshape(batch, groups, n_qb, dh, width)

    def cmp_rows(t):
        rows = seq // CMP_STRIDE
        return t.reshape(batch, rows, CMP_STRIDE, groups, dh).transpose(0, 3, 1, 2, 4).reshape(batch, groups, rows, CMP_STRIDE * dh)

    def with_features(keys, feat):
        return jnp.concatenate([keys, jnp.broadcast_to(feat, keys.shape[:2] + feat.shape)], -1)

    def key_rows(t, feat):
        return with_features(t.reshape(batch, seq, groups, dh).transpose(0, 2, 1, 3), feat)

    def value_cols(t, tile):
        return t.reshape(batch, seq // tile, tile, groups, dh).transpose(0, 3, 1, 4, 2)

    kc, vc = _compress(cmp_rows(part(0)), cmp_rows(part(1)), pe_k, pe_v, k1, k2, v1, v2)
    cmp_end = np.arange(seq // CMP_STRIDE) * CMP_STRIDE + CMP_BLOCK - 1
    kc = with_features(kc, _key_features(cmp_end, dh, 0))
    vct = vc.transpose(0, 1, 3, 2)
    g = gates[:, :NSA_HEADS * 3].reshape(batch, n_qb, Q_BLOCK, groups, hpg, 3).transpose(0, 3, 1, 5, 4, 2)
    g = g.reshape(batch, groups, n_qb, 3, width)
    positions = np.arange(seq)
    ot = _nsa_attention(qt, kc, vct,
                        key_rows(part(2), _key_features(positions, dh, n_sel)), value_cols(part(3), SEL_KEY_TILE),
                        key_rows(part(4), _key_features(positions, dh, 0)), value_cols(part(5), WIN_KEY_TILE), g, seq)
    o = ot.reshape(batch, groups, n_qb, dh, hpg, Q_BLOCK).transpose(0, 2, 5, 1, 4, 3)
    return o.reshape(batch * seq, d)


def kernel(x, c, ada_w, ada_b, ln_g, ln_b, ffn_w_in, ffn_w_out, ret_w_in, ret_w_out, nsa_w_in, nsa_w_out,
           nsa_pe_k, nsa_pe_v, nsa_ck_w1, nsa_ck_w2, nsa_cv_w1, nsa_cv_w2):
    batch, seq, d = x.shape
    depth = ada_w.shape[0]
    assert depth == DEPTH
    mod_all = _ada_mod(c, ada_w, ada_b).reshape(depth, batch, 9, d)
    xf = x.reshape(batch * seq, d)
    dk = d // RET_HEADS
    dv = 2 * dk
    dh = d // NSA_HEADS
    n_main = d + 6 * NSA_GROUPS * dh
    lane = 128
    for i in range(depth):
        mod = mod_all[i]
        ln = lambda k: (ln_g[i, k].reshape(1, d), ln_b[i, k].reshape(1, d))
        xf = _ffn(xf, mod, 0, *_ffn_weights(ffn_w_in[i, 0], ffn_w_out[i, 0]), *ln(0), seq)
        j = i // N_MIXERS
        if i % N_MIXERS == 0:
            proj = _proj(xf, mod, 3, ret_w_in[j].astype(BF16), seq, 2 * MXU_WIDTH)
            y = _retention_core(proj.reshape(batch, seq, -1), batch, seq, dk, dv).reshape(batch * seq, -1)
            w_out = ret_w_out[j]
        else:
            w_in = nsa_w_in[j].astype(BF16)
            n_gate = w_in.shape[1] - n_main
            w_gate = jnp.pad(w_in[:, n_main:], ((0, 0), (0, -n_gate % lane)))
            y = _nsa_mixer(xf, mod, w_in[:, :n_main], w_gate,
                           nsa_pe_k[j].reshape(1, -1), nsa_pe_v[j].reshape(1, -1),
                           nsa_ck_w1[j].astype(BF16), nsa_ck_w2[j].astype(BF16),
                           nsa_cv_w1[j].astype(BF16), nsa_cv_w2[j].astype(BF16), batch, seq)
            w_out = nsa_w_out[j]
        xf = _out_proj(y, xf, mod, 5, w_out.astype(BF16), *ln(1), seq)
        xf = _ffn(xf, mod, 6, *_ffn_weights(ffn_w_in[i, 1], ffn_w_out[i, 1]), *ln(2), seq)
    return xf.reshape(batch, seq, d)
```

```python
import functools

import numpy as np
import jax
import jax.numpy as jnp
from jax import lax
from jax.experimental import pallas as pl
from jax.experimental.pallas import tpu as pltpu

F32 = jnp.float32
BF16 = jnp.bfloat16

DEPTH = 4
N_MIXERS = 2
RET_HEADS = 4
NSA_HEADS = 16
NSA_GROUPS = 4
NSA_HPG = NSA_HEADS // NSA_GROUPS
CMP_BLOCK = 32
CMP_STRIDE = 16
SEL_BLOCK = 64
SEL_TOP = 16
WINDOW = 512
Q_BLOCK = 128
FORCED_SCORE = 1e4
INVALID_SCORE = -1.0
FFN_RES = 0.5
DN_ALPHA = (2 * DEPTH) ** 0.25
LN_EPS = 1e-5
GN_EPS = 1e-6
NEG_INF = -1e30

V7X_VMEM_LIMIT_BYTES = 56 * 1024 * 1024
MXU_WIDTH = 256
TOKEN_TILE = 512
RET_CHUNK = 256
SEL_KEY_TILE = 256
WIN_KEY_TILE = 128
KEY_WIDTH = 128
POS_ROWS = 16
LOG2_E = 1.4426950408889634


def _dot(a, b):
    return jnp.dot(a, b, preferred_element_type=F32)


def _sigmoid(x):
    return 1.0 / (1.0 + jnp.exp(-x))


def _layer_norm(z, g, b):
    mu = jnp.mean(z, -1, keepdims=True)
    zc = z - mu
    var = jnp.mean(zc * zc, -1, keepdims=True)
    return zc * lax.rsqrt(var + LN_EPS) * g + b


def _params(*sem):
    return pltpu.CompilerParams(dimension_semantics=sem, vmem_limit_bytes=V7X_VMEM_LIMIT_BYTES)


def _resident(shape):
    return pl.BlockSpec(shape, lambda *_: (0,) * len(shape), pipeline_mode=pl.Buffered(1))


def _ada_kernel(c_ref, w_ref, b_ref, o_ref):
    c = c_ref[...]
    c_act = (c * _sigmoid(c)).astype(BF16)
    o_ref[0] = _dot(c_act, w_ref[0].astype(BF16)) + b_ref[0]


def _ada_mod(c, ada_w, ada_b):
    depth, d, n = ada_w.shape
    b = c.shape[0]
    tn = n // 8
    return pl.pallas_call(
        _ada_kernel,
        grid=(depth, n // tn),
        in_specs=[
            pl.BlockSpec((b, d), lambda l, j: (0, 0)),
            pl.BlockSpec((1, d, tn), lambda l, j: (l, 0, j)),
            pl.BlockSpec((1, 1, tn), lambda l, j: (l, 0, j)),
        ],
        out_specs=pl.BlockSpec((1, b, tn), lambda l, j: (l, 0, j)),
        out_shape=jax.ShapeDtypeStruct((depth, b, n), F32),
        compiler_params=_params("parallel", "parallel"),
        name="ada_mod",
    )(c, ada_w, ada_b.reshape(depth, 1, n))


def _ffn_kernel(x_ref, mod_ref, wa_ref, wu_ref, wo_ref, lng_ref, lnb_ref, o_ref, h_ref, act_ref, *, row0, n_chunks):
    x = x_ref[...]
    shift = mod_ref[0, row0:row0 + 1, :]
    scale = mod_ref[0, row0 + 1:row0 + 2, :]
    gate = mod_ref[0, row0 + 2:row0 + 3, :]
    h_ref[...] = (x * (1.0 + scale) + shift).astype(BF16)
    fc = wa_ref.shape[2]
    for j in range(n_chunks):
        h = h_ref[...]
        a = _dot(h, wa_ref[j])
        u = _dot(h, wu_ref[j])
        act_ref[:, j * fc:(j + 1) * fc] = (a * _sigmoid(a) * u).astype(BF16)
    y = _dot(act_ref[...], wo_ref[...])
    z = DN_ALPHA * x + FFN_RES * (1.0 + gate) * y
    o_ref[...] = _layer_norm(z, lng_ref[...], lnb_ref[...])


def _ffn(x, mod, row0, wa, wu, wo, ln_g, ln_b, seq):
    n_tok, d = x.shape
    n_chunks, _, fc = wa.shape
    tm = min(TOKEN_TILE, seq)
    tiles_per_seq = seq // tm
    return pl.pallas_call(
        functools.partial(_ffn_kernel, row0=row0, n_chunks=n_chunks),
        grid=(n_tok // tm,),
        in_specs=[
            pl.BlockSpec((tm, d), lambda i: (i, 0)),
            pl.BlockSpec((1, 9, d), lambda i: (i // tiles_per_seq, 0, 0)),
            _resident((n_chunks, d, fc)),
            _resident((n_chunks, d, fc)),
            _resident((n_chunks * fc, d)),
            _resident((1, d)),
            _resident((1, d)),
        ],
        out_specs=pl.BlockSpec((tm, d), lambda i: (i, 0)),
        out_shape=jax.ShapeDtypeStruct((n_tok, d), F32),
        scratch_shapes=[pltpu.VMEM((tm, d), BF16), pltpu.VMEM((tm, n_chunks * fc), BF16)],
        compiler_params=_params("parallel"),
        name="ffn",
    )(x, mod, wa, wu, wo, ln_g, ln_b)


def _ffn_weights(w_in, w_out):
    d, two_f = w_in.shape
    f = two_f // 2
    n_chunks = f // MXU_WIDTH
    w = w_in.astype(BF16).reshape(d, 2, n_chunks, MXU_WIDTH)
    wa = w[:, 0].transpose(1, 0, 2)
    wu = w[:, 1].transpose(1, 0, 2)
    return wa, wu, w_out.astype(BF16)


def _proj_kernel(x_ref, mod_ref, w_ref, o_ref, *, row0, tn):
    x = x_ref[...]
    shift = mod_ref[0, row0:row0 + 1, :]
    scale = mod_ref[0, row0 + 1:row0 + 2, :]
    h = (x * (1.0 + scale) + shift).astype(BF16)
    for j in range(o_ref.shape[1] // tn):
        o_ref[:, j * tn:(j + 1) * tn] = _dot(h, w_ref[:, j * tn:(j + 1) * tn]).astype(o_ref.dtype)


def _proj(x, mod, row0, w, seq, tn):
    n_tok, d = x.shape
    n = w.shape[1]
    tm = min(TOKEN_TILE, seq)
    tiles_per_seq = seq // tm
    return pl.pallas_call(
        functools.partial(_proj_kernel, row0=row0, tn=tn),
        grid=(n_tok // tm,),
        in_specs=[
            pl.BlockSpec((tm, d), lambda i: (i, 0)),
            pl.BlockSpec((1, 9, d), lambda i: (i // tiles_per_seq, 0, 0)),
            _resident((d, n)),
        ],
        out_specs=pl.BlockSpec((tm, n), lambda i: (i, 0)),
        out_shape=jax.ShapeDtypeStruct((n_tok, n), BF16),
        compiler_params=_params("parallel"),
        name="mixer_in_proj",
    )(x, mod, w)


def _nsa_proj_kernel(x_ref, mod_ref, w_ref, feat_ref, q_ref, kcv_ref, keys_ref, vals_ref, gates_ref, *, row0, q_scale):
    x = x_ref[...]
    shift = mod_ref[0, row0:row0 + 1, :]
    scale = mod_ref[0, row0 + 1:row0 + 2, :]
    h = (x * (1.0 + scale) + shift).astype(BF16)
    half = keys_ref.shape[1] // 2
    col = 0
    for ref in (q_ref, kcv_ref, keys_ref, vals_ref, gates_ref):
        n = ref.shape[1]
        step = min(n, MXU_WIDTH)
        for j in range(0, n, step):
            y = _dot(h, w_ref[:, col + j:col + j + step])
            if ref is q_ref:
                y = y * q_scale
            elif ref is keys_ref:
                f = feat_ref[:, 0:KEY_WIDTH] if j < half else feat_ref[:, KEY_WIDTH:2 * KEY_WIDTH]
                y = y + jnp.concatenate([f] * (step // KEY_WIDTH), 1).astype(F32)
            elif ref is gates_ref:
                y = _sigmoid(y)
            ref[:, j:j + step] = y.astype(ref.dtype)
        col += n


def _nsa_proj(x, mod, row0, w, feat, seq, d, kv, q_scale):
    n_tok = x.shape[0]
    groups = NSA_GROUPS
    tm = min(TOKEN_TILE, seq)
    tiles_per_seq = seq // tm
    widths = (d, 2 * kv, 2 * groups * KEY_WIDTH, 2 * kv, w.shape[1] - d - 4 * kv - 2 * groups * KEY_WIDTH)
    dtypes = (BF16, F32, BF16, BF16, F32)
    return pl.pallas_call(
        functools.partial(_nsa_proj_kernel, row0=row0, q_scale=q_scale),
        grid=(n_tok // tm,),
        in_specs=[
            pl.BlockSpec((tm, d), lambda i: (i, 0)),
            pl.BlockSpec((1, 9, d), lambda i: (i // tiles_per_seq, 0, 0)),
            _resident(w.shape),
            pl.BlockSpec((tm, 2 * KEY_WIDTH), lambda i: (i % tiles_per_seq, 0)),
        ],
        out_specs=[pl.BlockSpec((tm, n), lambda i: (i, 0)) for n in widths],
        out_shape=[jax.ShapeDtypeStruct((n_tok, n), dt) for n, dt in zip(widths, dtypes)],
        compiler_params=_params("parallel"),
        name="nsa_in_proj",
    )(x, mod, w, feat)


def _out_kernel(y_ref, x_ref, mod_ref, w_ref, lng_ref, lnb_ref, o_ref, *, row_gate):
    gate = mod_ref[0, row_gate:row_gate + 1, :]
    y = _dot(y_ref[...], w_ref[...])
    z = DN_ALPHA * x_ref[...] + (1.0 + gate) * y
    o_ref[...] = _layer_norm(z, lng_ref[...], lnb_ref[...])


def _out_proj(y, x, mod, row_gate, w, ln_g, ln_b, seq):
    n_tok, d = x.shape
    k = y.shape[1]
    tm = min(TOKEN_TILE, seq)
    tiles_per_seq = seq // tm
    return pl.pallas_call(
        functools.partial(_out_kernel, row_gate=row_gate),
        grid=(n_tok // tm,),
        in_specs=[
            pl.BlockSpec((tm, k), lambda i: (i, 0)),
            pl.BlockSpec((tm, d), lambda i: (i, 0)),
            pl.BlockSpec((1, 9, d), lambda i: (i // tiles_per_seq, 0, 0)),
            _resident((k, d)),
            _resident((1, d)),
            _resident((1, d)),
        ],
        out_specs=pl.BlockSpec((tm, d), lambda i: (i, 0)),
        out_shape=jax.ShapeDtypeStruct((n_tok, d), F32),
        compiler_params=_params("parallel"),
        name="mixer_out_proj",
    )(y, x, mod, w, ln_g, ln_b)


def _ret_kernel(q_ref, k_ref, v_ref, g_ref, o_ref, state_ref, *, chunk, n_chunks, k_scale):
    head = pl.program_id(1).astype(F32)
    log_g = jnp.log(1.0 - jnp.exp2(-5.0 - (jnp.zeros((1, 1), F32) + head)))
    row = lax.broadcasted_iota(jnp.int32, (chunk, chunk), 0)
    col = lax.broadcasted_iota(jnp.int32, (chunk, chunk), 1)
    diff = (row - col).astype(F32)
    decay_intra = jnp.where(diff >= 0, jnp.exp(log_g * jnp.maximum(diff, 0.0)), 0.0)
    pos = lax.broadcasted_iota(jnp.int32, (chunk, 1), 0).astype(F32)
    decay_q = jnp.exp(log_g * (pos + 1.0))
    decay_k = jnp.exp(log_g * (chunk - 1.0 - pos))
    decay_state = jnp.exp(log_g * float(chunk))
    state_ref[...] = jnp.zeros_like(state_ref)

    def step(c, carry):
        rows = pl.ds(pl.multiple_of(c * chunk, chunk), chunk)
        q = q_ref[0, rows, :]
        k = k_ref[0, rows, :] * k_scale
        v = v_ref[0, rows, :]
        scores = lax.dot_general(q, k, (((1,), (1,)), ((), ())), preferred_element_type=F32) * decay_intra
        state = state_ref[...]
        o = _dot(scores.astype(BF16), v) + _dot((q.astype(F32) * decay_q).astype(BF16), state.astype(BF16))
        k_dec_t = (k.astype(F32) * decay_k).T.astype(BF16)
        state_ref[...] = decay_state * state + _dot(k_dec_t, v)
        mu = jnp.mean(o, -1, keepdims=True)
        oc = o - mu
        var = jnp.mean(oc * oc, -1, keepdims=True)
        o = oc * lax.rsqrt(var + GN_EPS)
        g = g_ref[0, rows, :].astype(F32)
        o_ref[0, rows, :] = (o * (g * _sigmoid(g))).astype(o_ref.dtype)
        return carry

    lax.fori_loop(0, n_chunks, step, 0)


def _retention_core(proj, batch, seq, dk, dv):
    heads = RET_HEADS
    chunk = min(RET_CHUNK, seq)
    k_blk0 = heads * dk // dk
    v_blk0 = 2 * heads * dk // dv
    g_blk0 = (2 * heads * dk + heads * dv) // dv
    return pl.pallas_call(
        functools.partial(_ret_kernel, chunk=chunk, n_chunks=seq // chunk, k_scale=dk ** -0.5),
        grid=(batch, heads),
        in_specs=[
            pl.BlockSpec((1, seq, dk), lambda b, h: (b, 0, h)),
            pl.BlockSpec((1, seq, dk), lambda b, h: (b, 0, k_blk0 + h)),
            pl.BlockSpec((1, seq, dv), lambda b, h: (b, 0, v_blk0 + h)),
            pl.BlockSpec((1, seq, dv), lambda b, h: (b, 0, g_blk0 + h)),
        ],
        out_specs=pl.BlockSpec((1, seq, dv), lambda b, h: (b, 0, h)),
        out_shape=jax.ShapeDtypeStruct((batch, seq, heads * dv), BF16),
        scratch_shapes=[pltpu.VMEM((dk, dv), F32)],
        compiler_params=_params("parallel", "parallel"),
        name="retention_core",
    )(proj, proj, proj, proj)


def _gelu_tanh(x):
    return 0.5 * x * (1.0 + jnp.tanh(0.7978845608028654 * (x + 0.044715 * (x * x * x))))


def _compress_kernel(x_ref, pek_ref, pev_ref, k1_ref, k2_ref, v1_ref, v2_ref, ok_ref, ov_ref, y_ref):
    groups, n_rows, half = y_ref.shape
    dh = half // CMP_STRIDE
    kv = groups * dh

    lane_tiles = x_ref.shape[1] // (n_rows * CMP_STRIDE)
    per_tile = x_ref.shape[2] // dh

    def mlp(lane0, pe_ref, w1_ref, w2_ref, out_ref):
        for r in range(CMP_STRIDE):
            for c in range(groups // per_tile):
                first_row = r * lane_tiles + lane0 // x_ref.shape[2] + c
                piece = x_ref[0, pl.ds(first_row, n_rows, stride=CMP_STRIDE * lane_tiles), :]
                for k in range(per_tile):
                    y_ref[c * per_tile + k, :, r * dh:(r + 1) * dh] = piece[:, k * dh:(k + 1) * dh].astype(BF16)
        bias = _dot(pe_ref[...].astype(BF16), w1_ref[...])
        for g in range(groups):
            y = y_ref[g]
            first = _dot(y, w1_ref[0:half, :])
            second = _dot(y, w1_ref[half:2 * half, :])
            hidden = first + pltpu.roll(second, n_rows - 1, 0) + bias
            out_ref[0, g] = _dot(_gelu_tanh(hidden).astype(BF16), w2_ref[...]).astype(out_ref.dtype)

    mlp(0, pek_ref, k1_ref, k2_ref, ok_ref)
    mlp(kv, pev_ref, v1_ref, v2_ref, ov_ref)


def _compress(kcv, pe_k, pe_v, k1, k2, v1, v2, groups):
    batch, seq, two_kv = kcv.shape
    lane = 128
    kcv = kcv.reshape(batch, seq * two_kv // lane, lane)
    rows = seq // CMP_STRIDE
    hidden = k1.shape[1]
    dh = k2.shape[1]
    out_blk = pl.BlockSpec((1, groups, rows, dh), lambda b: (b, 0, 0, 0))
    out_sds = jax.ShapeDtypeStruct((batch, groups, rows, dh), BF16)
    return pl.pallas_call(
        _compress_kernel,
        grid=(batch,),
        in_specs=[pl.BlockSpec((1,) + kcv.shape[1:], lambda b: (b, 0, 0)),
                  _resident(pe_k.shape), _resident(pe_v.shape),
                  _resident(k1.shape), _resident(k2.shape), _resident(v1.shape), _resident(v2.shape)],
        out_specs=[out_blk, out_blk],
        out_shape=[out_sds, out_sds],
        scratch_shapes=[pltpu.VMEM((groups, rows, CMP_STRIDE * dh), BF16)],
        compiler_params=_params("parallel"),
        name="nsa_compress",
    )(kcv, pe_k, pe_v, k1, k2, v1, v2)


def _nsa_attn_kernel(q_ref, kc_ref, vct_ref, keys_ref, vals_ref, gates_ref, o_ref,
                     qa_ref, vst_ref, vwt_ref, s_sel_ref, s_win_ref, *, n_sel, top):
    qb = pl.program_id(1)
    q0 = qb * Q_BLOCK
    groups, dh, width = qa_ref.shape[0], vct_ref.shape[2], qa_ref.shape[2]
    kv = groups * dh
    sel_row0 = KEY_WIDTH - n_sel
    all_groups = range(groups)
    pair = 2 * dh
    assert pair == Q_BLOCK == KEY_WIDTH

    @pl.when(qb == 0)
    def _():
        def sel_tile(i, carry):
            rows = pl.ds(pl.multiple_of(i * SEL_KEY_TILE, SEL_KEY_TILE), SEL_KEY_TILE)
            vt = vals_ref[0, rows, 0:kv].astype(F32).T
            for g in all_groups:
                vst_ref[g, i] = vt[g * dh:(g + 1) * dh, :].astype(BF16)
            return carry

        def win_tile(i, carry):
            rows = pl.ds(pl.multiple_of(i * WIN_KEY_TILE, WIN_KEY_TILE), WIN_KEY_TILE)
            vt = vals_ref[0, rows, kv:2 * kv].astype(F32).T
            for g in all_groups:
                vwt_ref[g, i] = vt[g * dh:(g + 1) * dh, :].astype(BF16)
            return carry

        lax.fori_loop(0, vst_ref.shape[1], sel_tile, 0)
        lax.fori_loop(0, vwt_ref.shape[1], win_tile, 0)

    lane = lax.broadcasted_iota(jnp.int32, (1, width), 1)
    t_row = q0 + lane % Q_BLOCK
    piece_row = lax.broadcasted_iota(jnp.int32, (POS_ROWS, width), 0)

    for g in all_groups:
        head = (g * NSA_HPG + lane // Q_BLOCK).astype(F32)
        slope = jnp.exp2(-8.0 * (head + 1.0) / NSA_HEADS) * LOG2_E
        hi = slope.astype(BF16).astype(F32)
        rest = slope - hi
        mid = rest.astype(BF16).astype(F32)
        lo = (rest - mid).astype(BF16).astype(F32)
        feat = jnp.zeros((POS_ROWS, width), F32)
        for idx, piece in enumerate((SEL_BLOCK * hi, SEL_BLOCK * mid, SEL_BLOCK * lo, hi, mid, lo)):
            feat = jnp.where(piece_row == idx, piece, feat)
        for p in range(NSA_HPG // 2):
            lanes0 = (g * NSA_HPG + 2 * p) * dh
            two_heads = q_ref[0, :, lanes0:lanes0 + pair].astype(F32).T
            qa_ref[g, 0:dh, (2 * p) * Q_BLOCK:(2 * p + 1) * Q_BLOCK] = two_heads[0:dh].astype(BF16)
            qa_ref[g, 0:dh, (2 * p + 1) * Q_BLOCK:(2 * p + 2) * Q_BLOCK] = two_heads[dh:pair].astype(BF16)
        qa_ref[g, dh:dh + POS_ROWS, :] = feat.astype(BF16)
        qa_ref[g, dh + POS_ROWS:KEY_WIDTH, :] = jnp.zeros((KEY_WIDTH - dh - POS_ROWS, width), BF16)

    n_cmp_pad = kc_ref.shape[2]
    blk = lax.broadcasted_iota(jnp.int32, (n_cmp_pad, width), 0)
    mask_c = blk * CMP_STRIDE + (CMP_BLOCK - 1) <= t_row
    p_cmp, o_cmp = [], []
    for g in all_groups:
        s = jnp.where(mask_c, _dot(kc_ref[0, g], qa_ref[g]), NEG_INF)
        e = jnp.exp2(s - jnp.max(s, 0, keepdims=True))
        p = jnp.where(mask_c, e * (1.0 / jnp.sum(e, 0, keepdims=True)), 0.0)
        p_cmp.append(p)
        o_cmp.append(_dot(vct_ref[0, g], p.astype(BF16)))

    sel_i = lax.broadcasted_iota(jnp.int32, (n_sel, n_cmp_pad), 0) * SEL_BLOCK
    cmp_i = lax.broadcasted_iota(jnp.int32, (n_sel, n_cmp_pad), 1) * CMP_STRIDE
    overlap = jnp.maximum(jnp.minimum(cmp_i + CMP_BLOCK, sel_i + SEL_BLOCK) - jnp.maximum(cmp_i, sel_i), 0)
    overlap = (overlap.astype(F32) * (1.0 / CMP_BLOCK)).astype(BF16)
    j_idx = lax.broadcasted_iota(jnp.int32, (n_sel, Q_BLOCK), 0)
    t_sel = q0 + lax.broadcasted_iota(jnp.int32, (n_sel, Q_BLOCK), 1)
    cur = t_sel // SEL_BLOCK
    forced = (j_idx == 0) | (j_idx == cur) | (j_idx == cur - 1)
    valid = j_idx * SEL_BLOCK <= t_sel
    for g in all_groups:
        p_sum = p_cmp[g][:, 0:Q_BLOCK]
        for h in range(1, NSA_HPG):
            p_sum = p_sum + p_cmp[g][:, h * Q_BLOCK:(h + 1) * Q_BLOCK]
        p_hi = p_sum.astype(BF16)
        rest = p_sum - p_hi.astype(F32)
        p_mid = rest.astype(BF16)
        p_lo = (rest - p_mid.astype(F32)).astype(BF16)
        imp = _dot(overlap, p_hi) + _dot(overlap, p_mid) + _dot(overlap, p_lo)
        score = jnp.where(valid, jnp.where(forced, FORCED_SCORE, imp), INVALID_SCORE)
        tiles = [score[r:r + 8, :] for r in range(0, n_sel, 8)]
        ranks = [jnp.zeros((8, Q_BLOCK), F32) for _ in tiles]
        for jp in range(n_sel):
            other = jnp.broadcast_to(score[jp:jp + 1, :], (8, Q_BLOCK))
            for v, tile_scores in enumerate(tiles):
                if jp < 8 * v:
                    ahead = other >= tile_scores
                elif jp >= 8 * v + 7:
                    ahead = other > tile_scores
                else:
                    ahead = (other > tile_scores) | ((other == tile_scores) & (j_idx[0:8, :] > jp - 8 * v))
                ranks[v] = ranks[v] + jnp.where(ahead, 1.0, 0.0)
        rank = jnp.concatenate(ranks, 0)
        block_bias = jnp.where(rank < top, 0.0, NEG_INF)
        qa_ref[g, sel_row0:KEY_WIDTH, :] = jnp.concatenate([block_bias] * NSA_HPG, 1).astype(BF16)

    def row_max(sc):
        return jnp.max(sc.reshape(sc.shape[0] // 8, 8, width), 0)

    def softmax_values(s_ref, vt_ref, first_tile, n_tiles, m8):
        m = [jnp.max(m8[g], 0, keepdims=True) for g in all_groups]

        def step(i, carry):
            out = []
            for g in all_groups:
                l8, acc = carry[g]
                pe = jnp.exp2(s_ref[g, i] - m[g])
                l8 = l8 + jnp.sum(pe.reshape(pe.shape[0] // 8, 8, width), 0)
                out.append((l8, acc + _dot(vt_ref[g, first_tile + i], pe.astype(BF16))))
            return tuple(out)

        init = tuple((jnp.zeros((8, width), F32), jnp.zeros((dh, width), F32)) for _ in all_groups)
        res = lax.fori_loop(0, n_tiles, step, init)
        return [acc * (1.0 / jnp.sum(l8, 0, keepdims=True)) for l8, acc in res]

    m8_init = tuple(jnp.full((8, width), NEG_INF, F32) for _ in all_groups)

    tile = SEL_KEY_TILE

    def sel_scores(g, kt):
        rows = pl.ds(pl.multiple_of(kt * tile, tile), tile)
        return _dot(keys_ref[0, rows, g * KEY_WIDTH:(g + 1) * KEY_WIDTH], qa_ref[g])

    def sel_pass(kt, m8):
        out = []
        for g in all_groups:
            sc = sel_scores(g, kt)
            s_sel_ref[g, kt] = sc
            out.append(jnp.maximum(m8[g], row_max(sc)))
        return tuple(out)

    last = (q0 + Q_BLOCK - 1) // tile
    m8 = list(lax.fori_loop(0, last, sel_pass, m8_init))
    visible = last * tile + lax.broadcasted_iota(jnp.int32, (tile, width), 0) <= t_row
    for g in all_groups:
        sc = jnp.where(visible, sel_scores(g, last), NEG_INF)
        s_sel_ref[g, last] = sc
        m8[g] = jnp.maximum(m8[g], row_max(sc))
    o_sel = softmax_values(s_sel_ref, vst_ref, 0, last + 1, m8)

    tile = WIN_KEY_TILE
    first = jnp.maximum(q0 - WINDOW, 0) // tile
    n_win = (q0 + Q_BLOCK - 1) // tile - first + 1
    key_row = lax.broadcasted_iota(jnp.int32, (tile, width), 0)

    def win_pass(i, m8):
        kt = first + i
        rows = pl.ds(pl.multiple_of(kt * tile, tile), tile)
        dist = (t_row - kt * tile) - key_row
        in_window = (dist >= 0) & (dist < WINDOW)
        out = []
        for g in all_groups:
            k_tile = keys_ref[0, rows, (groups + g) * KEY_WIDTH:(groups + g + 1) * KEY_WIDTH]
            sc = jnp.where(in_window, _dot(k_tile, qa_ref[g]), NEG_INF)
            s_win_ref[g, i] = sc
            out.append(jnp.maximum(m8[g], row_max(sc)))
        return tuple(out)

    m8 = lax.fori_loop(0, n_win, win_pass, m8_init)
    o_win = softmax_values(s_win_ref, vwt_ref, first, n_win, m8)

    gates_t = gates_ref[0].T
    for g in all_groups:
        per_head = []
        for h in range(NSA_HPG):
            lanes = slice(h * Q_BLOCK, (h + 1) * Q_BLOCK)
            row = (g * NSA_HPG + h) * 3
            per_head.append(gates_t[row:row + 1, :] * o_cmp[g][:, lanes]
                            + gates_t[row + 1:row + 2, :] * o_sel[g][:, lanes]
                            + gates_t[row + 2:row + 3, :] * o_win[g][:, lanes])
        for p in range(NSA_HPG // 2):
            lanes0 = (g * NSA_HPG + 2 * p) * dh
            two_heads = jnp.concatenate(per_head[2 * p:2 * p + 2], 0)
            o_ref[0, :, lanes0:lanes0 + pair] = two_heads.T.astype(o_ref.dtype)


def _nsa_attention(q, kc, vct, keys, vals, gates, groups):
    batch, seq, d = q.shape
    dh = vct.shape[2]
    width = NSA_HPG * Q_BLOCK
    n_qb = seq // Q_BLOCK
    n_sel = seq // SEL_BLOCK
    top = min(SEL_TOP, n_sel)
    assert dh + POS_ROWS + n_sel <= KEY_WIDTH
    per_b = lambda shape: pl.BlockSpec((1,) + shape, lambda b, i: (b,) + (0,) * len(shape))
    per_q = lambda lanes: pl.BlockSpec((1, Q_BLOCK, lanes), lambda b, i: (b, i, 0))
    n_win_tiles = min(WINDOW + Q_BLOCK, seq) // WIN_KEY_TILE
    return pl.pallas_call(
        functools.partial(_nsa_attn_kernel, n_sel=n_sel, top=top),
        grid=(batch, n_qb),
        in_specs=[
            per_q(d),
            per_b(kc.shape[1:]), per_b(vct.shape[1:]),
            per_b(keys.shape[1:]), per_b(vals.shape[1:]),
            per_q(gates.shape[2]),
        ],
        out_specs=per_q(d),
        out_shape=jax.ShapeDtypeStruct((batch, seq, d), BF16),
        scratch_shapes=[pltpu.VMEM((groups, KEY_WIDTH, width), BF16),
                        pltpu.VMEM((groups, seq // SEL_KEY_TILE, dh, SEL_KEY_TILE), BF16),
                        pltpu.VMEM((groups, seq // WIN_KEY_TILE, dh, WIN_KEY_TILE), BF16),
                        pltpu.VMEM((groups, seq // SEL_KEY_TILE, SEL_KEY_TILE, width), F32),
                        pltpu.VMEM((groups, n_win_tiles, WIN_KEY_TILE, width), F32)],
        compiler_params=_params("parallel", "arbitrary"),
        name="nsa_attention",
    )(q, kc, vct, keys, vals, gates)


def _key_features(pos, dh, n_sel):
    block, offset = pos // SEL_BLOCK, pos % SEL_BLOCK
    feat = np.zeros((pos.shape[0], KEY_WIDTH), np.float32)
    feat[:, dh:dh + 3] = block[:, None]
    feat[:, dh + 3:dh + 6] = offset[:, None]
    if n_sel:
        feat[:, KEY_WIDTH - n_sel:] = block[:, None] == np.arange(n_sel)[None, :]
    return feat


def _nsa_weights(w_in, d, kv, groups):
    dh = kv // groups
    part = lambda j: w_in[:, d + j * kv:d + (j + 1) * kv]
    pad_keys = lambda w: jnp.pad(w.reshape(-1, groups, dh), ((0, 0), (0, 0), (0, KEY_WIDTH - dh))).reshape(-1, groups * KEY_WIDTH)
    gate_w = w_in[:, d + 6 * kv:]
    gate_w = jnp.pad(gate_w, ((0, 0), (0, -gate_w.shape[1] % 128)))
    cols = [w_in[:, :d], part(0), part(1), pad_keys(part(2)), pad_keys(part(4)), part(3), part(5), gate_w]
    return jnp.concatenate(cols, 1).astype(BF16)


def _nsa_mixer(x, mod, w_in, pe_k, pe_v, k1, k2, v1, v2, batch, seq):
    d = x.shape[1]
    groups = NSA_GROUPS
    dh = d // NSA_HEADS
    kv = groups * dh
    n_sel = seq // SEL_BLOCK
    positions = np.arange(seq)
    feat = jnp.asarray(np.concatenate([_key_features(positions, dh, n_sel), _key_features(positions, dh, 0)], 1), BF16)
    q, kcv, keys, vals, gates = _nsa_proj(x, mod, 3, _nsa_weights(w_in, d, kv, groups), feat, seq, d, kv,
                                          dh ** -0.5 * LOG2_E)
    kc, vc = _compress(kcv.reshape(batch, seq, -1), pe_k, pe_v, k1, k2, v1, v2, groups)
    cmp_end = np.arange(seq // CMP_STRIDE) * CMP_STRIDE + CMP_BLOCK - 1
    cmp_feat = jnp.asarray(_key_features(cmp_end, dh, 0)[:, dh:], BF16)
    kc = jnp.concatenate([kc, jnp.broadcast_to(cmp_feat, kc.shape[:2] + cmp_feat.shape)], -1)
    o = _nsa_attention(q.reshape(batch, seq, d), kc, vc.transpose(0, 1, 3, 2), keys.reshape(batch, seq, -1),
                       vals.reshape(batch, seq, -1), gates.reshape(batch, seq, -1), groups)
    return o.reshape(batch * seq, d)


def kernel(x, c, ada_w, ada_b, ln_g, ln_b, ffn_w_in, ffn_w_out, ret_w_in, ret_w_out, nsa_w_in, nsa_w_out,
           nsa_pe_k, nsa_pe_v, nsa_ck_w1, nsa_ck_w2, nsa_cv_w1, nsa_cv_w2):
    batch, seq, d = x.shape
    depth = ada_w.shape[0]
    assert depth == DEPTH
    mod_all = _ada_mod(c, ada_w, ada_b).reshape(depth, batch, 9, d)
    xf = x.reshape(batch * seq, d)
    dk = d // RET_HEADS
    dv = 2 * dk
    for i in range(depth):
        mod = mod_all[i]
        ln = lambda k: (ln_g[i, k].reshape(1, d), ln_b[i, k].reshape(1, d))
        xf = _ffn(xf, mod, 0, *_ffn_weights(ffn_w_in[i, 0], ffn_w_out[i, 0]), *ln(0), seq)
        j = i // N_MIXERS
        if i % N_MIXERS == 0:
            proj = _proj(xf, mod, 3, ret_w_in[j].astype(BF16), seq, 2 * MXU_WIDTH)
            y = _retention_core(proj.reshape(batch, seq, -1), batch, seq, dk, dv).reshape(batch * seq, -1)
            w_out = ret_w_out[j]
        else:
            y = _nsa_mixer(xf, mod, nsa_w_in[j], nsa_pe_k[j].reshape(1, -1), nsa_pe_v[j].reshape(1, -1),
                           nsa_ck_w1[j].astype(BF16), nsa_ck_w2[j].astype(BF16),
                           nsa_cv_w1[j].astype(BF16), nsa_cv_w2[j].astype(BF16), batch, seq)
            w_out = nsa_w_out[j]
        xf = _out_proj(y, xf, mod, 5, w_out.astype(BF16), *ln(1), seq)
        xf = _ffn(xf, mod, 6, *_ffn_weights(ffn_w_in[i, 1], ffn_w_out[i, 1]), *ln(2), seq)
    return xf.reshape(batch, seq, d)
```

```python
import functools

import numpy as np
import jax
import jax.numpy as jnp
from jax import lax
from jax.experimental import pallas as pl
from jax.experimental.pallas import tpu as pltpu

F32 = jnp.float32
BF16 = jnp.bfloat16

DEPTH = 4
N_MIXERS = 2
RET_HEADS = 4
NSA_HEADS = 16
NSA_GROUPS = 4
NSA_HPG = NSA_HEADS // NSA_GROUPS
CMP_BLOCK = 32
CMP_STRIDE = 16
SEL_BLOCK = 64
SEL_TOP = 16
WINDOW = 512
Q_BLOCK = 128
FORCED_SCORE = 1e4
INVALID_SCORE = -1.0
FFN_RES = 0.5
DN_ALPHA = (2 * DEPTH) ** 0.25
LN_EPS = 1e-5
GN_EPS = 1e-6
NEG_INF = -1e30

V7X_VMEM_LIMIT_BYTES = 56 * 1024 * 1024
MXU_WIDTH = 256
TOKEN_TILE = 512
RET_CHUNK = 256
SEL_KEY_TILE = 256
WIN_KEY_TILE = 128
KEY_WIDTH = 128
POS_ROWS = 16
LOG2_E = 1.4426950408889634


def _dot(a, b):
    return jnp.dot(a, b, preferred_element_type=F32)


def _sigmoid(x):
    return 1.0 / (1.0 + jnp.exp(-x))


def _layer_norm(z, g, b):
    mu = jnp.mean(z, -1, keepdims=True)
    zc = z - mu
    var = jnp.mean(zc * zc, -1, keepdims=True)
    return zc * lax.rsqrt(var + LN_EPS) * g + b


def _params(*sem):
    return pltpu.CompilerParams(dimension_semantics=sem, vmem_limit_bytes=V7X_VMEM_LIMIT_BYTES)


def _resident(shape):
    return pl.BlockSpec(shape, lambda *_: (0,) * len(shape), pipeline_mode=pl.Buffered(1))


def _ada_kernel(c_ref, w_ref, b_ref, o_ref):
    c = c_ref[...]
    c_act = (c * _sigmoid(c)).astype(BF16)
    o_ref[0] = _dot(c_act, w_ref[0].astype(BF16)) + b_ref[0]


def _ada_mod(c, ada_w, ada_b):
    depth, d, n = ada_w.shape
    b = c.shape[0]
    tn = n // 8
    return pl.pallas_call(
        _ada_kernel,
        grid=(depth, n // tn),
        in_specs=[
            pl.BlockSpec((b, d), lambda l, j: (0, 0)),
            pl.BlockSpec((1, d, tn), lambda l, j: (l, 0, j)),
            pl.BlockSpec((1, 1, tn), lambda l, j: (l, 0, j)),
        ],
        out_specs=pl.BlockSpec((1, b, tn), lambda l, j: (l, 0, j)),
        out_shape=jax.ShapeDtypeStruct((depth, b, n), F32),
        compiler_params=_params("parallel", "parallel"),
        name="ada_mod",
    )(c, ada_w, ada_b.reshape(depth, 1, n))


def _ffn_kernel(x_ref, mod_ref, wi_ref, wo_ref, lng_ref, lnb_ref, o_ref, h_ref, act_ref, *, row0, n_chunks):
    x = x_ref[...]
    shift = mod_ref[0, row0:row0 + 1, :]
    scale = mod_ref[0, row0 + 1:row0 + 2, :]
    gate = mod_ref[0, row0 + 2:row0 + 3, :]
    h_ref[...] = (x * (1.0 + scale) + shift).astype(BF16)
    f = wo_ref.shape[0]
    fc = f // n_chunks
    for j in range(n_chunks):
        h = h_ref[...]
        a = _dot(h, wi_ref[:, j * fc:(j + 1) * fc])
        u = _dot(h, wi_ref[:, f + j * fc:f + (j + 1) * fc])
        act_ref[:, j * fc:(j + 1) * fc] = (a * _sigmoid(a) * u).astype(BF16)
    y = _dot(act_ref[...], wo_ref[...])
    z = DN_ALPHA * x + FFN_RES * (1.0 + gate) * y
    o_ref[...] = _layer_norm(z, lng_ref[...], lnb_ref[...])


def _ffn(x, mod, row0, w_in, w_out, ln_g, ln_b, seq):
    n_tok, d = x.shape
    f = w_out.shape[0]
    tm = min(TOKEN_TILE, seq)
    tiles_per_seq = seq // tm
    return pl.pallas_call(
        functools.partial(_ffn_kernel, row0=row0, n_chunks=f // MXU_WIDTH),
        grid=(n_tok // tm,),
        in_specs=[
            pl.BlockSpec((tm, d), lambda i: (i, 0)),
            pl.BlockSpec((1, 9, d), lambda i: (i // tiles_per_seq, 0, 0)),
            _resident(w_in.shape),
            _resident(w_out.shape),
            _resident((1, d)),
            _resident((1, d)),
        ],
        out_specs=pl.BlockSpec((tm, d), lambda i: (i, 0)),
        out_shape=jax.ShapeDtypeStruct((n_tok, d), F32),
        scratch_shapes=[pltpu.VMEM((tm, d), BF16), pltpu.VMEM((tm, f), BF16)],
        compiler_params=_params("parallel"),
        name="ffn",
    )(x, mod, w_in, w_out, ln_g, ln_b)


def _proj_kernel(x_ref, mod_ref, w_ref, o_ref, *, row0, tn):
    x = x_ref[...]
    shift = mod_ref[0, row0:row0 + 1, :]
    scale = mod_ref[0, row0 + 1:row0 + 2, :]
    h = (x * (1.0 + scale) + shift).astype(BF16)
    for j in range(o_ref.shape[1] // tn):
        o_ref[:, j * tn:(j + 1) * tn] = _dot(h, w_ref[:, j * tn:(j + 1) * tn]).astype(o_ref.dtype)


def _proj(x, mod, row0, w, seq, tn):
    n_tok, d = x.shape
    n = w.shape[1]
    tm = min(TOKEN_TILE, seq)
    tiles_per_seq = seq // tm
    return pl.pallas_call(
        functools.partial(_proj_kernel, row0=row0, tn=tn),
        grid=(n_tok // tm,),
        in_specs=[
            pl.BlockSpec((tm, d), lambda i: (i, 0)),
            pl.BlockSpec((1, 9, d), lambda i: (i // tiles_per_seq, 0, 0)),
            _resident((d, n)),
        ],
        out_specs=pl.BlockSpec((tm, n), lambda i: (i, 0)),
        out_shape=jax.ShapeDtypeStruct((n_tok, n), BF16),
        compiler_params=_params("parallel"),
        name="mixer_in_proj",
    )(x, mod, w)


def _nsa_proj_kernel(x_ref, mod_ref, w_ref, feat_ref, q_ref, kcv_ref, keys_ref, vals_ref, gates_ref, *, row0, q_scale):
    x = x_ref[...]
    shift = mod_ref[0, row0:row0 + 1, :]
    scale = mod_ref[0, row0 + 1:row0 + 2, :]
    h = (x * (1.0 + scale) + shift).astype(BF16)
    half = keys_ref.shape[1] // 2
    lane_tile = kcv_ref.shape[2]
    col = 0
    for ref in (q_ref, kcv_ref, keys_ref, vals_ref, gates_ref):
        n = ref.shape[1] if ref is not kcv_ref else kcv_ref.shape[0] * lane_tile
        step = min(n, MXU_WIDTH)
        for j in range(0, n, step):
            y = _dot(h, w_ref[:, col + j:col + j + step])
            if ref is q_ref:
                y = y * q_scale
            elif ref is keys_ref:
                f = feat_ref[:, 0:KEY_WIDTH] if j < half else feat_ref[:, KEY_WIDTH:2 * KEY_WIDTH]
                y = y + jnp.concatenate([f] * (step // KEY_WIDTH), 1).astype(F32)
            elif ref is gates_ref:
                y = _sigmoid(y)
            if ref is kcv_ref:
                for c in range(step // lane_tile):
                    kcv_ref[(j + c * lane_tile) // lane_tile] = y[:, c * lane_tile:(c + 1) * lane_tile]
            else:
                ref[:, j:j + step] = y.astype(ref.dtype)
        col += n


def _nsa_proj(x, mod, row0, w, feat, seq, d, kv, q_scale):
    n_tok = x.shape[0]
    groups = NSA_GROUPS
    tm = min(TOKEN_TILE, seq)
    tiles_per_seq = seq // tm
    widths = (d, 2 * kv, 2 * groups * KEY_WIDTH, 2 * kv, w.shape[1] - d - 4 * kv - 2 * groups * KEY_WIDTH)
    dtypes = (BF16, F32, BF16, BF16, F32)
    lane = 128
    return pl.pallas_call(
        functools.partial(_nsa_proj_kernel, row0=row0, q_scale=q_scale),
        grid=(n_tok // tm,),
        in_specs=[
            pl.BlockSpec((tm, d), lambda i: (i, 0)),
            pl.BlockSpec((1, 9, d), lambda i: (i // tiles_per_seq, 0, 0)),
            _resident(w.shape),
            pl.BlockSpec((tm, 2 * KEY_WIDTH), lambda i: (i % tiles_per_seq, 0)),
        ],
        out_specs=[pl.BlockSpec((tm, n), lambda i: (i, 0)) if k != 1 else
                   pl.BlockSpec((n // lane, tm, lane), lambda i: (0, i, 0)) for k, n in enumerate(widths)],
        out_shape=[jax.ShapeDtypeStruct((n_tok, n) if k != 1 else (n // lane, n_tok, lane), dt)
                   for k, (n, dt) in enumerate(zip(widths, dtypes))],
        compiler_params=_params("parallel"),
        name="nsa_in_proj",
    )(x, mod, w, feat)


def _out_kernel(y_ref, x_ref, mod_ref, w_ref, lng_ref, lnb_ref, o_ref, *, row_gate):
    gate = mod_ref[0, row_gate:row_gate + 1, :]
    y = _dot(y_ref[...], w_ref[...])
    z = DN_ALPHA * x_ref[...] + (1.0 + gate) * y
    o_ref[...] = _layer_norm(z, lng_ref[...], lnb_ref[...])


def _out_proj(y, x, mod, row_gate, w, ln_g, ln_b, seq):
    n_tok, d = x.shape
    k = y.shape[1]
    tm = min(TOKEN_TILE, seq)
    tiles_per_seq = seq // tm
    return pl.pallas_call(
        functools.partial(_out_kernel, row_gate=row_gate),
        grid=(n_tok // tm,),
        in_specs=[
            pl.BlockSpec((tm, k), lambda i: (i, 0)),
            pl.BlockSpec((tm, d), lambda i: (i, 0)),
            pl.BlockSpec((1, 9, d), lambda i: (i // tiles_per_seq, 0, 0)),
            _resident((k, d)),
            _resident((1, d)),
            _resident((1, d)),
        ],
        out_specs=pl.BlockSpec((tm, d), lambda i: (i, 0)),
        out_shape=jax.ShapeDtypeStruct((n_tok, d), F32),
        compiler_params=_params("parallel"),
        name="mixer_out_proj",
    )(y, x, mod, w, ln_g, ln_b)


def _ret_kernel(q_ref, k_ref, v_ref, g_ref, o_ref, state_ref, *, chunk, n_chunks, k_scale):
    head = pl.program_id(1).astype(F32)
    log_g = jnp.log(1.0 - jnp.exp2(-5.0 - (jnp.zeros((1, 1), F32) + head)))
    row = lax.broadcasted_iota(jnp.int32, (chunk, chunk), 0)
    col = lax.broadcasted_iota(jnp.int32, (chunk, chunk), 1)
    diff = (row - col).astype(F32)
    decay_intra = jnp.where(diff >= 0, jnp.exp(log_g * jnp.maximum(diff, 0.0)), 0.0)
    pos = lax.broadcasted_iota(jnp.int32, (chunk, 1), 0).astype(F32)
    decay_q = jnp.exp(log_g * (pos + 1.0))
    decay_k = jnp.exp(log_g * (chunk - 1.0 - pos))
    decay_state = jnp.exp(log_g * float(chunk))
    state_ref[...] = jnp.zeros_like(state_ref)

    def step(c, carry):
        rows = pl.ds(pl.multiple_of(c * chunk, chunk), chunk)
        q = q_ref[0, rows, :]
        k = k_ref[0, rows, :] * k_scale
        v = v_ref[0, rows, :]
        scores = lax.dot_general(q, k, (((1,), (1,)), ((), ())), preferred_element_type=F32) * decay_intra
        state = state_ref[...]
        o = _dot(scores.astype(BF16), v) + _dot((q.astype(F32) * decay_q).astype(BF16), state.astype(BF16))
        k_dec_t = (k.astype(F32) * decay_k).T.astype(BF16)
        state_ref[...] = decay_state * state + _dot(k_dec_t, v)
        mu = jnp.mean(o, -1, keepdims=True)
        oc = o - mu
        var = jnp.mean(oc * oc, -1, keepdims=True)
        o = oc * lax.rsqrt(var + GN_EPS)
        g = g_ref[0, rows, :].astype(F32)
        o_ref[0, rows, :] = (o * (g * _sigmoid(g))).astype(o_ref.dtype)
        return carry

    lax.fori_loop(0, n_chunks, step, 0)


def _retention_core(proj, batch, seq, dk, dv):
    heads = RET_HEADS
    chunk = min(RET_CHUNK, seq)
    k_blk0 = heads * dk // dk
    v_blk0 = 2 * heads * dk // dv
    g_blk0 = (2 * heads * dk + heads * dv) // dv
    return pl.pallas_call(
        functools.partial(_ret_kernel, chunk=chunk, n_chunks=seq // chunk, k_scale=dk ** -0.5),
        grid=(batch, heads),
        in_specs=[
            pl.BlockSpec((1, seq, dk), lambda b, h: (b, 0, h)),
            pl.BlockSpec((1, seq, dk), lambda b, h: (b, 0, k_blk0 + h)),
            pl.BlockSpec((1, seq, dv), lambda b, h: (b, 0, v_blk0 + h)),
            pl.BlockSpec((1, seq, dv), lambda b, h: (b, 0, g_blk0 + h)),
        ],
        out_specs=pl.BlockSpec((1, seq, dv), lambda b, h: (b, 0, h)),
        out_shape=jax.ShapeDtypeStruct((batch, seq, heads * dv), BF16),
        scratch_shapes=[pltpu.VMEM((dk, dv), F32)],
        compiler_params=_params("parallel", "parallel"),
        name="retention_core",
    )(proj, proj, proj, proj)


def _gelu_tanh(x):
    return 0.5 * x * (1.0 + jnp.tanh(0.7978845608028654 * (x + 0.044715 * (x * x * x))))


def _compress_kernel(x_ref, pek_ref, pev_ref, k1_ref, k2_ref, v1_ref, v2_ref, ok_ref, ov_ref, y_ref):
    groups, n_rows, half = y_ref.shape
    dh = half // CMP_STRIDE
    kv = groups * dh
    per_tile = x_ref.shape[2] // dh

    def mlp(lane0, pe_ref, w1_ref, w2_ref, out_ref):
        for r in range(CMP_STRIDE):
            for c in range(groups // per_tile):
                piece = x_ref[lane0 // x_ref.shape[2] + c, pl.ds(r, n_rows, stride=CMP_STRIDE), :]
                for k in range(per_tile):
                    y_ref[c * per_tile + k, :, r * dh:(r + 1) * dh] = piece[:, k * dh:(k + 1) * dh].astype(BF16)
        bias = _dot(pe_ref[...].astype(BF16), w1_ref[...])
        for g in range(groups):
            y = y_ref[g]
            first = _dot(y, w1_ref[0:half, :])
            second = _dot(y, w1_ref[half:2 * half, :])
            hidden = first + pltpu.roll(second, n_rows - 1, 0) + bias
            out_ref[0, g] = _dot(_gelu_tanh(hidden).astype(BF16), w2_ref[...]).astype(out_ref.dtype)

    mlp(0, pek_ref, k1_ref, k2_ref, ok_ref)
    mlp(kv, pev_ref, v1_ref, v2_ref, ov_ref)


def _compress(kcv, pe_k, pe_v, k1, k2, v1, v2, groups, batch):
    lane_tiles, n_tok, lane = kcv.shape
    seq = n_tok // batch
    rows = seq // CMP_STRIDE
    hidden = k1.shape[1]
    dh = k2.shape[1]
    out_blk = pl.BlockSpec((1, groups, rows, dh), lambda b: (b, 0, 0, 0))
    out_sds = jax.ShapeDtypeStruct((batch, groups, rows, dh), BF16)
    return pl.pallas_call(
        _compress_kernel,
        grid=(batch,),
        in_specs=[pl.BlockSpec((lane_tiles, seq, lane), lambda b: (0, b, 0)),
                  _resident(pe_k.shape), _resident(pe_v.shape),
                  _resident(k1.shape), _resident(k2.shape), _resident(v1.shape), _resident(v2.shape)],
        out_specs=[out_blk, out_blk],
        out_shape=[out_sds, out_sds],
        scratch_shapes=[pltpu.VMEM((groups, rows, CMP_STRIDE * dh), BF16)],
        compiler_params=_params("parallel"),
        name="nsa_compress",
    )(kcv, pe_k, pe_v, k1, k2, v1, v2)


def _nsa_attn_kernel(q_ref, kc_ref, vct_ref, keys_ref, vals_ref, gates_ref, o_ref,
                     qa_ref, vst_ref, vwt_ref, s_sel_ref, s_win_ref, *, n_sel, top):
    qb = pl.program_id(1)
    q0 = qb * Q_BLOCK
    groups, dh, width = qa_ref.shape[0], vct_ref.shape[2], qa_ref.shape[2]
    kv = groups * dh
    sel_row0 = KEY_WIDTH - n_sel
    all_groups = range(groups)
    pair = 2 * dh
    assert pair == Q_BLOCK == KEY_WIDTH

    @pl.when(qb == 0)
    def _():
        def sel_tile(i, carry):
            rows = pl.ds(pl.multiple_of(i * SEL_KEY_TILE, SEL_KEY_TILE), SEL_KEY_TILE)
            vt = vals_ref[0, rows, 0:kv].astype(F32).T
            for g in all_groups:
                vst_ref[g, i] = vt[g * dh:(g + 1) * dh, :].astype(BF16)
            return carry

        def win_tile(i, carry):
            rows = pl.ds(pl.multiple_of(i * WIN_KEY_TILE, WIN_KEY_TILE), WIN_KEY_TILE)
            vt = vals_ref[0, rows, kv:2 * kv].astype(F32).T
            for g in all_groups:
                vwt_ref[g, i] = vt[g * dh:(g + 1) * dh, :].astype(BF16)
            return carry

        lax.fori_loop(0, vst_ref.shape[1], sel_tile, 0)
        lax.fori_loop(0, vwt_ref.shape[1], win_tile, 0)

    lane = lax.broadcasted_iota(jnp.int32, (1, width), 1)
    t_row = q0 + lane % Q_BLOCK
    piece_row = lax.broadcasted_iota(jnp.int32, (POS_ROWS, width), 0)

    for g in all_groups:
        head = (g * NSA_HPG + lane // Q_BLOCK).astype(F32)
        slope = jnp.exp2(-8.0 * (head + 1.0) / NSA_HEADS) * LOG2_E
        hi = slope.astype(BF16).astype(F32)
        rest = slope - hi
        mid = rest.astype(BF16).astype(F32)
        lo = (rest - mid).astype(BF16).astype(F32)
        feat = jnp.zeros((POS_ROWS, width), F32)
        for idx, piece in enumerate((SEL_BLOCK * hi, SEL_BLOCK * mid, SEL_BLOCK * lo, hi, mid, lo)):
            feat = jnp.where(piece_row == idx, piece, feat)
        for p in range(NSA_HPG // 2):
            lanes0 = (g * NSA_HPG + 2 * p) * dh
            two_heads = q_ref[0, :, lanes0:lanes0 + pair].astype(F32).T
            qa_ref[g, 0:dh, (2 * p) * Q_BLOCK:(2 * p + 1) * Q_BLOCK] = two_heads[0:dh].astype(BF16)
            qa_ref[g, 0:dh, (2 * p + 1) * Q_BLOCK:(2 * p + 2) * Q_BLOCK] = two_heads[dh:pair].astype(BF16)
        qa_ref[g, dh:dh + POS_ROWS, :] = feat.astype(BF16)
        qa_ref[g, dh + POS_ROWS:KEY_WIDTH, :] = jnp.zeros((KEY_WIDTH - dh - POS_ROWS, width), BF16)

    n_cmp_pad = kc_ref.shape[2]
    blk = lax.broadcasted_iota(jnp.int32, (n_cmp_pad, width), 0)
    mask_c = blk * CMP_STRIDE + (CMP_BLOCK - 1) <= t_row
    p_cmp, o_cmp = [], []
    for g in all_groups:
        s = jnp.where(mask_c, _dot(kc_ref[0, g], qa_ref[g]), NEG_INF)
        e = jnp.exp2(s - jnp.max(s, 0, keepdims=True))
        p = jnp.where(mask_c, e * (1.0 / jnp.sum(e, 0, keepdims=True)), 0.0)
        p_cmp.append(p)
        o_cmp.append(_dot(vct_ref[0, g], p.astype(BF16)))

    sel_i = lax.broadcasted_iota(jnp.int32, (n_sel, n_cmp_pad), 0) * SEL_BLOCK
    cmp_i = lax.broadcasted_iota(jnp.int32, (n_sel, n_cmp_pad), 1) * CMP_STRIDE
    overlap = jnp.maximum(jnp.minimum(cmp_i + CMP_BLOCK, sel_i + SEL_BLOCK) - jnp.maximum(cmp_i, sel_i), 0)
    overlap = (overlap.astype(F32) * (1.0 / CMP_BLOCK)).astype(BF16)
    j_idx = lax.broadcasted_iota(jnp.int32, (n_sel, Q_BLOCK), 0)
    t_sel = q0 + lax.broadcasted_iota(jnp.int32, (n_sel, Q_BLOCK), 1)
    cur = t_sel // SEL_BLOCK
    forced = (j_idx == 0) | (j_idx == cur) | (j_idx == cur - 1)
    valid = j_idx * SEL_BLOCK <= t_sel
    for g in all_groups:
        p_sum = p_cmp[g][:, 0:Q_BLOCK]
        for h in range(1, NSA_HPG):
            p_sum = p_sum + p_cmp[g][:, h * Q_BLOCK:(h + 1) * Q_BLOCK]
        p_hi = p_sum.astype(BF16)
        rest = p_sum - p_hi.astype(F32)
        p_mid = rest.astype(BF16)
        p_lo = (rest - p_mid.astype(F32)).astype(BF16)
        imp = _dot(overlap, p_hi) + _dot(overlap, p_mid) + _dot(overlap, p_lo)
        score = jnp.where(valid, jnp.where(forced, FORCED_SCORE, imp), INVALID_SCORE)
        tiles = [score[r:r + 8, :] for r in range(0, n_sel, 8)]
        ranks = [jnp.zeros((8, Q_BLOCK), F32) for _ in tiles]
        for jp in range(n_sel):
            other = jnp.broadcast_to(score[jp:jp + 1, :], (8, Q_BLOCK))
            for v, tile_scores in enumerate(tiles):
                if jp < 8 * v:
                    ahead = other >= tile_scores
                elif jp >= 8 * v + 7:
                    ahead = other > tile_scores
                else:
                    ahead = (other > tile_scores) | ((other == tile_scores) & (j_idx[0:8, :] > jp - 8 * v))
                ranks[v] = ranks[v] + jnp.where(ahead, 1.0, 0.0)
        rank = jnp.concatenate(ranks, 0)
        block_bias = jnp.where(rank < top, 0.0, NEG_INF)
        qa_ref[g, sel_row0:KEY_WIDTH, :] = jnp.concatenate([block_bias] * NSA_HPG, 1).astype(BF16)

    def row_max(sc):
        return jnp.max(sc.reshape(sc.shape[0] // 8, 8, width), 0)

    def for_tiles(n, body, init):
        carry = lax.fori_loop(0, n // 2, lambda j, c: body(2 * j + 1, body(2 * j, c)), init)
        return lax.fori_loop(n // 2 * 2, n, body, carry)

    def softmax_values(s_ref, vt_ref, first_tile, n_tiles, m8):
        m = [jnp.max(m8[g], 0, keepdims=True) for g in all_groups]

        def step(i, carry):
            out = []
            for g in all_groups:
                l8, acc = carry[g]
                pe = jnp.exp2(s_ref[g, i] - m[g])
                l8 = l8 + jnp.sum(pe.reshape(pe.shape[0] // 8, 8, width), 0)
                out.append((l8, acc + _dot(vt_ref[g, first_tile + i], pe.astype(BF16))))
            return tuple(out)

        init = tuple((jnp.zeros((8, width), F32), jnp.zeros((dh, width), F32)) for _ in all_groups)
        res = for_tiles(n_tiles, step, init)
        return [acc * (1.0 / jnp.sum(l8, 0, keepdims=True)) for l8, acc in res]

    m8_init = tuple(jnp.full((8, width), NEG_INF, F32) for _ in all_groups)

    tile = SEL_KEY_TILE

    def sel_scores(g, kt):
        rows = pl.ds(pl.multiple_of(kt * tile, tile), tile)
        return _dot(keys_ref[0, rows, g * KEY_WIDTH:(g + 1) * KEY_WIDTH], qa_ref[g])

    def sel_pass(kt, m8):
        out = []
        for g in all_groups:
            sc = sel_scores(g, kt)
            s_sel_ref[g, kt] = sc
            out.append(jnp.maximum(m8[g], row_max(sc)))
        return tuple(out)

    last = (q0 + Q_BLOCK - 1) // tile
    m8 = list(for_tiles(last, sel_pass, m8_init))
    visible = last * tile + lax.broadcasted_iota(jnp.int32, (tile, width), 0) <= t_row
    for g in all_groups:
        sc = jnp.where(visible, sel_scores(g, last), NEG_INF)
        s_sel_ref[g, last] = sc
        m8[g] = jnp.maximum(m8[g], row_max(sc))
    o_sel = softmax_values(s_sel_ref, vst_ref, 0, last + 1, m8)

    tile = WIN_KEY_TILE
    first = jnp.maximum(q0 - WINDOW, 0) // tile
    n_win = (q0 + Q_BLOCK - 1) // tile - first + 1
    key_row = lax.broadcasted_iota(jnp.int32, (tile, width), 0)

    def win_pass(i, m8):
        kt = first + i
        rows = pl.ds(pl.multiple_of(kt * tile, tile), tile)
        dist = (t_row - kt * tile) - key_row
        in_window = (dist >= 0) & (dist < WINDOW)
        out = []
        for g in all_groups:
            k_tile = keys_ref[0, rows, (groups + g) * KEY_WIDTH:(groups + g + 1) * KEY_WIDTH]
            sc = jnp.where(in_window, _dot(k_tile, qa_ref[g]), NEG_INF)
            s_win_ref[g, i] = sc
            out.append(jnp.maximum(m8[g], row_max(sc)))
        return tuple(out)

    m8 = for_tiles(n_win, win_pass, m8_init)
    o_win = softmax_values(s_win_ref, vwt_ref, first, n_win, m8)

    gates_t = gates_ref[0].T
    for g in all_groups:
        per_head = []
        for h in range(NSA_HPG):
            lanes = slice(h * Q_BLOCK, (h + 1) * Q_BLOCK)
            row = (g * NSA_HPG + h) * 3
            per_head.append(gates_t[row:row + 1, :] * o_cmp[g][:, lanes]
                            + gates_t[row + 1:row + 2, :] * o_sel[g][:, lanes]
                            + gates_t[row + 2:row + 3, :] * o_win[g][:, lanes])
        for p in range(NSA_HPG // 2):
            lanes0 = (g * NSA_HPG + 2 * p) * dh
            two_heads = jnp.concatenate(per_head[2 * p:2 * p + 2], 0)
            o_ref[0, :, lanes0:lanes0 + pair] = two_heads.T.astype(o_ref.dtype)


def _nsa_attention(q, kc, vct, keys, vals, gates, groups):
    batch, seq, d = q.shape
    dh = vct.shape[2]
    width = NSA_HPG * Q_BLOCK
    n_qb = seq // Q_BLOCK
    n_sel = seq // SEL_BLOCK
    top = min(SEL_TOP, n_sel)
    assert dh + POS_ROWS + n_sel <= KEY_WIDTH
    per_b = lambda shape: pl.BlockSpec((1,) + shape, lambda b, i: (b,) + (0,) * len(shape))
    per_q = lambda lanes: pl.BlockSpec((1, Q_BLOCK, lanes), lambda b, i: (b, i, 0))
    n_win_tiles = min(WINDOW + Q_BLOCK, seq) // WIN_KEY_TILE
    return pl.pallas_call(
        functools.partial(_nsa_attn_kernel, n_sel=n_sel, top=top),
        grid=(batch, n_qb),
        in_specs=[
            per_q(d),
            per_b(kc.shape[1:]), per_b(vct.shape[1:]),
            per_b(keys.shape[1:]), per_b(vals.shape[1:]),
            per_q(gates.shape[2]),
        ],
        out_specs=per_q(d),
        out_shape=jax.ShapeDtypeStruct((batch, seq, d), BF16),
        scratch_shapes=[pltpu.VMEM((groups, KEY_WIDTH, width), BF16),
                        pltpu.VMEM((groups, seq // SEL_KEY_TILE, dh, SEL_KEY_TILE), BF16),
                        pltpu.VMEM((groups, seq // WIN_KEY_TILE, dh, WIN_KEY_TILE), BF16),
                        pltpu.VMEM((groups, seq // SEL_KEY_TILE, SEL_KEY_TILE, width), F32),
                        pltpu.VMEM((groups, n_win_tiles, WIN_KEY_TILE, width), F32)],
        compiler_params=_params("parallel", "arbitrary"),
        name="nsa_attention",
    )(q, kc, vct, keys, vals, gates)


def _key_features(pos, dh, n_sel):
    block, offset = pos // SEL_BLOCK, pos % SEL_BLOCK
    feat = np.zeros((pos.shape[0], KEY_WIDTH), np.float32)
    feat[:, dh:dh + 3] = block[:, None]
    feat[:, dh + 3:dh + 6] = offset[:, None]
    if n_sel:
        feat[:, KEY_WIDTH - n_sel:] = block[:, None] == np.arange(n_sel)[None, :]
    return feat


def _nsa_weights(w_in, d, kv, groups):
    dh = kv // groups
    part = lambda j: w_in[:, d + j * kv:d + (j + 1) * kv]
    pad_keys = lambda w: jnp.pad(w.reshape(-1, groups, dh), ((0, 0), (0, 0), (0, KEY_WIDTH - dh))).reshape(-1, groups * KEY_WIDTH)
    gate_w = w_in[:, d + 6 * kv:]
    gate_w = jnp.pad(gate_w, ((0, 0), (0, -gate_w.shape[1] % 128)))
    cols = [w_in[:, :d], part(0), part(1), pad_keys(part(2)), pad_keys(part(4)), part(3), part(5), gate_w]
    return jnp.concatenate(cols, 1).astype(BF16)


def _nsa_mixer(x, mod, w_in, pe_k, pe_v, k1, k2, v1, v2, batch, seq):
    d = x.shape[1]
    groups = NSA_GROUPS
    dh = d // NSA_HEADS
    kv = groups * dh
    n_sel = seq // SEL_BLOCK
    positions = np.arange(seq)
    feat = jnp.asarray(np.concatenate([_key_features(positions, dh, n_sel), _key_features(positions, dh, 0)], 1), BF16)
    q, kcv, keys, vals, gates = _nsa_proj(x, mod, 3, _nsa_weights(w_in, d, kv, groups), feat, seq, d, kv,
                                          dh ** -0.5 * LOG2_E)
    kc, vc = _compress(kcv, pe_k, pe_v, k1, k2, v1, v2, groups, batch)
    cmp_end = np.arange(seq // CMP_STRIDE) * CMP_STRIDE + CMP_BLOCK - 1
    cmp_feat = jnp.asarray(_key_features(cmp_end, dh, 0)[:, dh:], BF16)
    kc = jnp.concatenate([kc, jnp.broadcast_to(cmp_feat, kc.shape[:2] + cmp_feat.shape)], -1)
    o = _nsa_attention(q.reshape(batch, seq, d), kc, vc.transpose(0, 1, 3, 2), keys.reshape(batch, seq, -1),
                       vals.reshape(batch, seq, -1), gates.reshape(batch, seq, -1), groups)
    return o.reshape(batch * seq, d)


def kernel(x, c, ada_w, ada_b, ln_g, ln_b, ffn_w_in, ffn_w_out, ret_w_in, ret_w_out, nsa_w_in, nsa_w_out,
           nsa_pe_k, nsa_pe_v, nsa_ck_w1, nsa_ck_w2, nsa_cv_w1, nsa_cv_w2):
    batch, seq, d = x.shape
    depth = ada_w.shape[0]
    assert depth == DEPTH
    mod_all = _ada_mod(c, ada_w, ada_b).reshape(depth, batch, 9, d)
    xf = x.reshape(batch * seq, d)
    dk = d // RET_HEADS
    dv = 2 * dk
    for i in range(depth):
        mod = mod_all[i]
        ln = lambda k: (ln_g[i, k].reshape(1, d), ln_b[i, k].reshape(1, d))
        xf = _ffn(xf, mod, 0, ffn_w_in[i, 0].astype(BF16), ffn_w_out[i, 0].astype(BF16), *ln(0), seq)
        j = i // N_MIXERS
        if i % N_MIXERS == 0:
            proj = _proj(xf, mod, 3, ret_w_in[j].astype(BF16), seq, 2 * MXU_WIDTH)
            y = _retention_core(proj.reshape(batch, seq, -1), batch, seq, dk, dv).reshape(batch * seq, -1)
            w_out = ret_w_out[j]
        else:
            y = _nsa_mixer(xf, mod, nsa_w_in[j], nsa_pe_k[j].reshape(1, -1), nsa_pe_v[j].reshape(1, -1),
                           nsa_ck_w1[j].astype(BF16), nsa_ck_w2[j].astype(BF16),
                           nsa_cv_w1[j].astype(BF16), nsa_cv_w2[j].astype(BF16), batch, seq)
            w_out = nsa_w_out[j]
        xf = _out_proj(y, xf, mod, 5, w_out.astype(BF16), *ln(1), seq)
        xf = _ffn(xf, mod, 6, ffn_w_in[i, 1].astype(BF16), ffn_w_out[i, 1].astype(BF16), *ln(2), seq)
    return xf.reshape(batch, seq, d)
```

```python
import functools

import numpy as np
import jax
import jax.numpy as jnp
from jax import lax
from jax.experimental import pallas as pl
from jax.experimental.pallas import tpu as pltpu

F32 = jnp.float32
BF16 = jnp.bfloat16

DEPTH = 4
N_MIXERS = 2
RET_HEADS = 4
NSA_HEADS = 16
NSA_GROUPS = 4
NSA_HPG = NSA_HEADS // NSA_GROUPS
CMP_BLOCK = 32
CMP_STRIDE = 16
SEL_BLOCK = 64
SEL_TOP = 16
WINDOW = 512
Q_BLOCK = 128
FORCED_SCORE = 1e4
INVALID_SCORE = -1.0
FFN_RES = 0.5
DN_ALPHA = (2 * DEPTH) ** 0.25
LN_EPS = 1e-5
GN_EPS = 1e-6
NEG_INF = -1e30

V7X_VMEM_LIMIT_BYTES = 56 * 1024 * 1024
MXU_WIDTH = 256
TOKEN_TILE = 512
RET_CHUNK = 256
RET_HEADS_PER_STEP = 2
SEL_KEY_TILE = 256
WIN_KEY_TILE = 128
KEY_WIDTH = 128
POS_ROWS = 16
LOG2_E = 1.4426950408889634


def _dot(a, b):
    return jnp.dot(a, b, preferred_element_type=F32)


def _sigmoid(x):
    return 1.0 / (1.0 + jnp.exp(-x))


def _layer_norm(z, g, b):
    mu = jnp.mean(z, -1, keepdims=True)
    zc = z - mu
    var = jnp.mean(zc * zc, -1, keepdims=True)
    return zc * lax.rsqrt(var + LN_EPS) * g + b


def _params(*sem):
    return pltpu.CompilerParams(dimension_semantics=sem, vmem_limit_bytes=V7X_VMEM_LIMIT_BYTES)


def _resident(shape):
    return pl.BlockSpec(shape, lambda *_: (0,) * len(shape), pipeline_mode=pl.Buffered(1))


def _ada_kernel(c_ref, w_ref, b_ref, o_ref):
    c = c_ref[...]
    c_act = (c * _sigmoid(c)).astype(BF16)
    o_ref[0] = _dot(c_act, w_ref[0].astype(BF16)) + b_ref[0]


def _ada_mod(c, ada_w, ada_b):
    depth, d, n = ada_w.shape
    b = c.shape[0]
    tn = n // 8
    return pl.pallas_call(
        _ada_kernel,
        grid=(depth, n // tn),
        in_specs=[
            pl.BlockSpec((b, d), lambda l, j: (0, 0)),
            pl.BlockSpec((1, d, tn), lambda l, j: (l, 0, j)),
            pl.BlockSpec((1, 1, tn), lambda l, j: (l, 0, j)),
        ],
        out_specs=pl.BlockSpec((1, b, tn), lambda l, j: (l, 0, j)),
        out_shape=jax.ShapeDtypeStruct((depth, b, n), F32),
        compiler_params=_params("parallel", "parallel"),
        name="ada_mod",
    )(c, ada_w, ada_b.reshape(depth, 1, n))


def _swiglu_post_norm(x, mod_ref, row0, wi_ref, wo_ref, lng_ref, lnb_ref, h_ref, act_ref, n_chunks):
    shift = mod_ref[0, row0:row0 + 1, :]
    scale = mod_ref[0, row0 + 1:row0 + 2, :]
    gate = mod_ref[0, row0 + 2:row0 + 3, :]
    h_ref[...] = (x * (1.0 + scale) + shift).astype(BF16)
    f = wo_ref.shape[0]
    fc = f // n_chunks
    for j in range(n_chunks):
        h = h_ref[...]
        a = _dot(h, wi_ref[:, j * fc:(j + 1) * fc])
        u = _dot(h, wi_ref[:, f + j * fc:f + (j + 1) * fc])
        act_ref[:, j * fc:(j + 1) * fc] = (a * _sigmoid(a) * u).astype(BF16)
    y = _dot(act_ref[...], wo_ref[...])
    z = DN_ALPHA * x + FFN_RES * (1.0 + gate) * y
    return _layer_norm(z, lng_ref[...], lnb_ref[...])


def _ffn_kernel(x_ref, mod_ref, wi_ref, wo_ref, lng_ref, lnb_ref, o_ref, h_ref, act_ref, *, row0, n_chunks):
    o_ref[...] = _swiglu_post_norm(x_ref[...], mod_ref, row0, wi_ref, wo_ref, lng_ref, lnb_ref, h_ref, act_ref, n_chunks)


def _mixer_out_ffn_kernel(y_ref, x_ref, mod_ref, wm_ref, lng1_ref, lnb1_ref, wi_ref, wo_ref, lng2_ref, lnb2_ref,
                          o_ref, h_ref, act_ref, *, n_chunks):
    gate = mod_ref[0, 5:6, :]
    z = DN_ALPHA * x_ref[...] + (1.0 + gate) * _dot(y_ref[...], wm_ref[...])
    x1 = _layer_norm(z, lng1_ref[...], lnb1_ref[...])
    o_ref[...] = _swiglu_post_norm(x1, mod_ref, 6, wi_ref, wo_ref, lng2_ref, lnb2_ref, h_ref, act_ref, n_chunks)


def _ffn(x, mod, row0, w_in, w_out, ln_g, ln_b, seq):
    n_tok, d = x.shape
    f = w_out.shape[0]
    tm = min(TOKEN_TILE, seq)
    tiles_per_seq = seq // tm
    return pl.pallas_call(
        functools.partial(_ffn_kernel, row0=row0, n_chunks=f // MXU_WIDTH),
        grid=(n_tok // tm,),
        in_specs=[
            pl.BlockSpec((tm, d), lambda i: (i, 0)),
            pl.BlockSpec((1, 9, d), lambda i: (i // tiles_per_seq, 0, 0)),
            _resident(w_in.shape),
            _resident(w_out.shape),
            _resident((1, d)),
            _resident((1, d)),
        ],
        out_specs=pl.BlockSpec((tm, d), lambda i: (i, 0)),
        out_shape=jax.ShapeDtypeStruct((n_tok, d), F32),
        scratch_shapes=[pltpu.VMEM((tm, d), BF16), pltpu.VMEM((tm, f), BF16)],
        compiler_params=_params("parallel"),
        name="ffn",
    )(x, mod, w_in, w_out, ln_g, ln_b)


def _mixer_out_ffn(y, x, mod, w_mix, ln1, w_in, w_out, ln2, seq):
    n_tok, d = x.shape
    k = y.shape[1]
    f = w_out.shape[0]
    tm = min(TOKEN_TILE, seq)
    tiles_per_seq = seq // tm
    return pl.pallas_call(
        functools.partial(_mixer_out_ffn_kernel, n_chunks=f // MXU_WIDTH),
        grid=(n_tok // tm,),
        in_specs=[
            pl.BlockSpec((tm, k), lambda i: (i, 0)),
            pl.BlockSpec((tm, d), lambda i: (i, 0)),
            pl.BlockSpec((1, 9, d), lambda i: (i // tiles_per_seq, 0, 0)),
            _resident(w_mix.shape), _resident((1, d)), _resident((1, d)),
            _resident(w_in.shape), _resident(w_out.shape), _resident((1, d)), _resident((1, d)),
        ],
        out_specs=pl.BlockSpec((tm, d), lambda i: (i, 0)),
        out_shape=jax.ShapeDtypeStruct((n_tok, d), F32),
        scratch_shapes=[pltpu.VMEM((tm, d), BF16), pltpu.VMEM((tm, f), BF16)],
        compiler_params=_params("parallel"),
        name="mixer_out_ffn",
    )(y, x, mod, w_mix, *ln1, w_in, w_out, *ln2)


def _proj_kernel(x_ref, mod_ref, w_ref, o_ref, *, row0, tn):
    x = x_ref[...]
    shift = mod_ref[0, row0:row0 + 1, :]
    scale = mod_ref[0, row0 + 1:row0 + 2, :]
    h = (x * (1.0 + scale) + shift).astype(BF16)
    for j in range(o_ref.shape[1] // tn):
        o_ref[:, j * tn:(j + 1) * tn] = _dot(h, w_ref[:, j * tn:(j + 1) * tn]).astype(o_ref.dtype)


def _proj(x, mod, row0, w, seq, tn):
    n_tok, d = x.shape
    n = w.shape[1]
    tm = min(TOKEN_TILE, seq)
    tiles_per_seq = seq // tm
    return pl.pallas_call(
        functools.partial(_proj_kernel, row0=row0, tn=tn),
        grid=(n_tok // tm,),
        in_specs=[
            pl.BlockSpec((tm, d), lambda i: (i, 0)),
            pl.BlockSpec((1, 9, d), lambda i: (i // tiles_per_seq, 0, 0)),
            _resident((d, n)),
        ],
        out_specs=pl.BlockSpec((tm, n), lambda i: (i, 0)),
        out_shape=jax.ShapeDtypeStruct((n_tok, n), BF16),
        compiler_params=_params("parallel"),
        name="mixer_in_proj",
    )(x, mod, w)


def _nsa_proj_kernel(x_ref, mod_ref, w_ref, feat_ref, q_ref, kcv_ref, keys_ref, vals_ref, gates_ref, *, row0, q_scale):
    x = x_ref[...]
    shift = mod_ref[0, row0:row0 + 1, :]
    scale = mod_ref[0, row0 + 1:row0 + 2, :]
    h = (x * (1.0 + scale) + shift).astype(BF16)
    half = keys_ref.shape[1] // 2
    lane_tile = kcv_ref.shape[2]
    col = 0
    for ref in (q_ref, kcv_ref, keys_ref, vals_ref, gates_ref):
        n = ref.shape[1] if ref is not kcv_ref else kcv_ref.shape[0] * lane_tile
        step = min(n, MXU_WIDTH)
        for j in range(0, n, step):
            y = _dot(h, w_ref[:, col + j:col + j + step])
            if ref is q_ref:
                y = y * q_scale
            elif ref is keys_ref:
                f = feat_ref[:, 0:KEY_WIDTH] if j < half else feat_ref[:, KEY_WIDTH:2 * KEY_WIDTH]
                y = y + jnp.concatenate([f] * (step // KEY_WIDTH), 1).astype(F32)
            elif ref is gates_ref:
                y = _sigmoid(y)
            if ref is kcv_ref:
                for c in range(step // lane_tile):
                    kcv_ref[(j + c * lane_tile) // lane_tile] = y[:, c * lane_tile:(c + 1) * lane_tile]
            else:
                ref[:, j:j + step] = y.astype(ref.dtype)
        col += n


def _nsa_proj(x, mod, row0, w, feat, seq, d, kv, q_scale):
    n_tok = x.shape[0]
    groups = NSA_GROUPS
    tm = min(TOKEN_TILE, seq)
    tiles_per_seq = seq // tm
    widths = (d, 2 * kv, 2 * groups * KEY_WIDTH, 2 * kv, w.shape[1] - d - 4 * kv - 2 * groups * KEY_WIDTH)
    dtypes = (BF16, F32, BF16, BF16, F32)
    lane = 128
    return pl.pallas_call(
        functools.partial(_nsa_proj_kernel, row0=row0, q_scale=q_scale),
        grid=(n_tok // tm,),
        in_specs=[
            pl.BlockSpec((tm, d), lambda i: (i, 0)),
            pl.BlockSpec((1, 9, d), lambda i: (i // tiles_per_seq, 0, 0)),
            _resident(w.shape),
            pl.BlockSpec((tm, 2 * KEY_WIDTH), lambda i: (i % tiles_per_seq, 0)),
        ],
        out_specs=[pl.BlockSpec((tm, n), lambda i: (i, 0)) if k != 1 else
                   pl.BlockSpec((n // lane, tm, lane), lambda i: (0, i, 0)) for k, n in enumerate(widths)],
        out_shape=[jax.ShapeDtypeStruct((n_tok, n) if k != 1 else (n // lane, n_tok, lane), dt)
                   for k, (n, dt) in enumerate(zip(widths, dtypes))],
        compiler_params=_params("parallel"),
        name="nsa_in_proj",
    )(x, mod, w, feat)


def _ret_kernel(q_ref, k_ref, v_ref, g_ref, o_ref, state_ref, *, chunk, n_chunks, k_scale):
    heads = state_ref.shape[0]
    dk, dv = state_ref.shape[1], state_ref.shape[2]
    row = lax.broadcasted_iota(jnp.int32, (chunk, chunk), 0)
    col = lax.broadcasted_iota(jnp.int32, (chunk, chunk), 1)
    diff = (row - col).astype(F32)
    pos = lax.broadcasted_iota(jnp.int32, (chunk, 1), 0).astype(F32)
    decays = []
    for hh in range(heads):
        head = (pl.program_id(1) * heads + hh).astype(F32)
        log_g = jnp.log(1.0 - jnp.exp2(-5.0 - (jnp.zeros((1, 1), F32) + head)))
        decays.append((jnp.where(diff >= 0, jnp.exp(log_g * jnp.maximum(diff, 0.0)), 0.0),
                       jnp.exp(log_g * (pos + 1.0)),
                       jnp.exp(log_g * (chunk - 1.0 - pos)),
                       jnp.exp(log_g * float(chunk))))
    state_ref[...] = jnp.zeros_like(state_ref)

    def step(c, carry):
        rows = pl.ds(pl.multiple_of(c * chunk, chunk), chunk)
        for hh in range(heads):
            decay_intra, decay_q, decay_k, decay_state = decays[hh]
            q = q_ref[0, rows, hh * dk:(hh + 1) * dk]
            k = k_ref[0, rows, hh * dk:(hh + 1) * dk] * k_scale
            v = v_ref[0, rows, hh * dv:(hh + 1) * dv]
            scores = lax.dot_general(q, k, (((1,), (1,)), ((), ())), preferred_element_type=F32) * decay_intra
            state = state_ref[hh]
            o = _dot(scores.astype(BF16), v) + _dot((q.astype(F32) * decay_q).astype(BF16), state.astype(BF16))
            k_dec_t = (k.astype(F32) * decay_k).T.astype(BF16)
            state_ref[hh] = decay_state * state + _dot(k_dec_t, v)
            mu = jnp.mean(o, -1, keepdims=True)
            oc = o - mu
            var = jnp.mean(oc * oc, -1, keepdims=True)
            o = oc * lax.rsqrt(var + GN_EPS)
            g = g_ref[0, rows, hh * dv:(hh + 1) * dv].astype(F32)
            o_ref[0, rows, hh * dv:(hh + 1) * dv] = (o * (g * _sigmoid(g))).astype(o_ref.dtype)
        return carry

    lax.fori_loop(0, n_chunks, step, 0)


def _retention_core(proj, batch, seq, dk, dv):
    heads = RET_HEADS
    hps = RET_HEADS_PER_STEP
    chunk = min(RET_CHUNK, seq)
    k_blk0 = heads * dk // (hps * dk)
    v_blk0 = 2 * heads * dk // (hps * dv)
    g_blk0 = (2 * heads * dk + heads * dv) // (hps * dv)
    return pl.pallas_call(
        functools.partial(_ret_kernel, chunk=chunk, n_chunks=seq // chunk, k_scale=dk ** -0.5),
        grid=(batch, heads // hps),
        in_specs=[
            pl.BlockSpec((1, seq, hps * dk), lambda b, h: (b, 0, h)),
            pl.BlockSpec((1, seq, hps * dk), lambda b, h: (b, 0, k_blk0 + h)),
            pl.BlockSpec((1, seq, hps * dv), lambda b, h: (b, 0, v_blk0 + h)),
            pl.BlockSpec((1, seq, hps * dv), lambda b, h: (b, 0, g_blk0 + h)),
        ],
        out_specs=pl.BlockSpec((1, seq, hps * dv), lambda b, h: (b, 0, h)),
        out_shape=jax.ShapeDtypeStruct((batch, seq, heads * dv), BF16),
        scratch_shapes=[pltpu.VMEM((hps, dk, dv), F32)],
        compiler_params=_params("parallel", "parallel"),
        name="retention_core",
    )(proj, proj, proj, proj)


def _gelu_tanh(x):
    return 0.5 * x * (1.0 + jnp.tanh(0.7978845608028654 * (x + 0.044715 * (x * x * x))))


def _compress_kernel(x_ref, pek_ref, pev_ref, k1_ref, k2_ref, v1_ref, v2_ref, ok_ref, ov_ref, y_ref):
    groups, n_rows, half = y_ref.shape
    dh = half // CMP_STRIDE
    kv = groups * dh
    per_tile = x_ref.shape[2] // dh

    def mlp(lane0, pe_ref, w1_ref, w2_ref, out_ref):
        for r in range(CMP_STRIDE):
            for c in range(groups // per_tile):
                piece = x_ref[lane0 // x_ref.shape[2] + c, pl.ds(r, n_rows, stride=CMP_STRIDE), :]
                for k in range(per_tile):
                    y_ref[c * per_tile + k, :, r * dh:(r + 1) * dh] = piece[:, k * dh:(k + 1) * dh].astype(BF16)
        bias = _dot(pe_ref[...].astype(BF16), w1_ref[...])
        for g in range(groups):
            y = y_ref[g]
            first = _dot(y, w1_ref[0:half, :])
            second = _dot(y, w1_ref[half:2 * half, :])
            hidden = first + pltpu.roll(second, n_rows - 1, 0) + bias
            out_ref[0, g] = _dot(_gelu_tanh(hidden).astype(BF16), w2_ref[...]).astype(out_ref.dtype)

    mlp(0, pek_ref, k1_ref, k2_ref, ok_ref)
    mlp(kv, pev_ref, v1_ref, v2_ref, ov_ref)


def _compress(kcv, pe_k, pe_v, k1, k2, v1, v2, groups, batch):
    lane_tiles, n_tok, lane = kcv.shape
    seq = n_tok // batch
    rows = seq // CMP_STRIDE
    hidden = k1.shape[1]
    dh = k2.shape[1]
    out_blk = pl.BlockSpec((1, groups, rows, dh), lambda b: (b, 0, 0, 0))
    out_sds = jax.ShapeDtypeStruct((batch, groups, rows, dh), BF16)
    return pl.pallas_call(
        _compress_kernel,
        grid=(batch,),
        in_specs=[pl.BlockSpec((lane_tiles, seq, lane), lambda b: (0, b, 0)),
                  _resident(pe_k.shape), _resident(pe_v.shape),
                  _resident(k1.shape), _resident(k2.shape), _resident(v1.shape), _resident(v2.shape)],
        out_specs=[out_blk, out_blk],
        out_shape=[out_sds, out_sds],
        scratch_shapes=[pltpu.VMEM((groups, rows, CMP_STRIDE * dh), BF16)],
        compiler_params=_params("parallel"),
        name="nsa_compress",
    )(kcv, pe_k, pe_v, k1, k2, v1, v2)


def _nsa_attn_kernel(q_ref, kc_ref, vct_ref, keys_ref, vals_ref, gates_ref, o_ref,
                     qa_ref, vst_ref, vwt_ref, s_sel_ref, s_win_ref, *, n_sel, top):
    qb = pl.program_id(1)
    q0 = qb * Q_BLOCK
    groups, dh, width = qa_ref.shape[0], vct_ref.shape[2], qa_ref.shape[2]
    kv = groups * dh
    sel_row0 = KEY_WIDTH - n_sel
    all_groups = range(groups)
    pair = 2 * dh
    assert pair == Q_BLOCK == KEY_WIDTH

    @pl.when(qb == 0)
    def _():
        def sel_tile(i, carry):
            rows = pl.ds(pl.multiple_of(i * SEL_KEY_TILE, SEL_KEY_TILE), SEL_KEY_TILE)
            vt = vals_ref[0, rows, 0:kv].astype(F32).T
            for g in all_groups:
                vst_ref[g, i] = vt[g * dh:(g + 1) * dh, :].astype(BF16)
            return carry

        def win_tile(i, carry):
            rows = pl.ds(pl.multiple_of(i * WIN_KEY_TILE, WIN_KEY_TILE), WIN_KEY_TILE)
            vt = vals_ref[0, rows, kv:2 * kv].astype(F32).T
            for g in all_groups:
                vwt_ref[g, i] = vt[g * dh:(g + 1) * dh, :].astype(BF16)
            return carry

        lax.fori_loop(0, vst_ref.shape[1], sel_tile, 0)
        lax.fori_loop(0, vwt_ref.shape[1], win_tile, 0)

    lane = lax.broadcasted_iota(jnp.int32, (1, width), 1)
    t_row = q0 + lane % Q_BLOCK
    piece_row = lax.broadcasted_iota(jnp.int32, (POS_ROWS, width), 0)

    for g in all_groups:
        head = (g * NSA_HPG + lane // Q_BLOCK).astype(F32)
        slope = jnp.exp2(-8.0 * (head + 1.0) / NSA_HEADS) * LOG2_E
        hi = slope.astype(BF16).astype(F32)
        rest = slope - hi
        mid = rest.astype(BF16).astype(F32)
        lo = (rest - mid).astype(BF16).astype(F32)
        feat = jnp.zeros((POS_ROWS, width), F32)
        for idx, piece in enumerate((SEL_BLOCK * hi, SEL_BLOCK * mid, SEL_BLOCK * lo, hi, mid, lo)):
            feat = jnp.where(piece_row == idx, piece, feat)
        for p in range(NSA_HPG // 2):
            lanes0 = (g * NSA_HPG + 2 * p) * dh
            two_heads = q_ref[0, :, lanes0:lanes0 + pair].astype(F32).T
            qa_ref[g, 0:dh, (2 * p) * Q_BLOCK:(2 * p + 1) * Q_BLOCK] = two_heads[0:dh].astype(BF16)
            qa_ref[g, 0:dh, (2 * p + 1) * Q_BLOCK:(2 * p + 2) * Q_BLOCK] = two_heads[dh:pair].astype(BF16)
        qa_ref[g, dh:dh + POS_ROWS, :] = feat.astype(BF16)
        qa_ref[g, dh + POS_ROWS:KEY_WIDTH, :] = jnp.zeros((KEY_WIDTH - dh - POS_ROWS, width), BF16)

    n_cmp_pad = kc_ref.shape[2]
    blk = lax.broadcasted_iota(jnp.int32, (n_cmp_pad, width), 0)
    mask_c = blk * CMP_STRIDE + (CMP_BLOCK - 1) <= t_row
    p_cmp, o_cmp = [], []
    for g in all_groups:
        s = jnp.where(mask_c, _dot(kc_ref[0, g], qa_ref[g]), NEG_INF)
        e = jnp.exp2(s - jnp.max(s, 0, keepdims=True))
        p = jnp.where(mask_c, e * (1.0 / jnp.sum(e, 0, keepdims=True)), 0.0)
        p_cmp.append(p)
        o_cmp.append(_dot(vct_ref[0, g], p.astype(BF16)))

    sel_i = lax.broadcasted_iota(jnp.int32, (n_sel, n_cmp_pad), 0) * SEL_BLOCK
    cmp_i = lax.broadcasted_iota(jnp.int32, (n_sel, n_cmp_pad), 1) * CMP_STRIDE
    overlap = jnp.maximum(jnp.minimum(cmp_i + CMP_BLOCK, sel_i + SEL_BLOCK) - jnp.maximum(cmp_i, sel_i), 0)
    overlap = (overlap.astype(F32) * (1.0 / CMP_BLOCK)).astype(BF16)
    j_idx = lax.broadcasted_iota(jnp.int32, (n_sel, Q_BLOCK), 0)
    t_sel = q0 + lax.broadcasted_iota(jnp.int32, (n_sel, Q_BLOCK), 1)
    cur = t_sel // SEL_BLOCK
    forced = (j_idx == 0) | (j_idx == cur) | (j_idx == cur - 1)
    valid = j_idx * SEL_BLOCK <= t_sel
    for g in all_groups:
        p_sum = p_cmp[g][:, 0:Q_BLOCK]
        for h in range(1, NSA_HPG):
            p_sum = p_sum + p_cmp[g][:, h * Q_BLOCK:(h + 1) * Q_BLOCK]
        p_hi = p_sum.astype(BF16)
        rest = p_sum - p_hi.astype(F32)
        p_mid = rest.astype(BF16)
        p_lo = (rest - p_mid.astype(F32)).astype(BF16)
        imp = _dot(overlap, p_hi) + _dot(overlap, p_mid) + _dot(overlap, p_lo)
        score = jnp.where(valid, jnp.where(forced, FORCED_SCORE, imp), INVALID_SCORE)
        tiles = [score[r:r + 8, :] for r in range(0, n_sel, 8)]
        ranks = [jnp.zeros((8, Q_BLOCK), F32) for _ in tiles]
        for jp in range(n_sel):
            other = jnp.broadcast_to(score[jp:jp + 1, :], (8, Q_BLOCK))
            for v, tile_scores in enumerate(tiles):
                if jp < 8 * v:
                    ahead = other >= tile_scores
                elif jp >= 8 * v + 7:
                    ahead = other > tile_scores
                else:
                    ahead = (other > tile_scores) | ((other == tile_scores) & (j_idx[0:8, :] > jp - 8 * v))
                ranks[v] = ranks[v] + jnp.where(ahead, 1.0, 0.0)
        rank = jnp.concatenate(ranks, 0)
        block_bias = jnp.where(rank < top, 0.0, NEG_INF)
        qa_ref[g, sel_row0:KEY_WIDTH, :] = jnp.concatenate([block_bias] * NSA_HPG, 1).astype(BF16)

    def row_max(sc):
        return jnp.max(sc.reshape(sc.shape[0] // 8, 8, width), 0)

    def for_tiles(n, body, init):
        carry = lax.fori_loop(0, n // 2, lambda j, c: body(2 * j + 1, body(2 * j, c)), init)
        return lax.fori_loop(n // 2 * 2, n, body, carry)

    def softmax_values(s_ref, vt_ref, first_tile, n_tiles, m8):
        m = [jnp.max(m8[g], 0, keepdims=True) for g in all_groups]

        def step(i, carry):
            out = []
            for g in all_groups:
                l8, acc = carry[g]
                pe = jnp.exp2(s_ref[g, i] - m[g])
                l8 = l8 + jnp.sum(pe.reshape(pe.shape[0] // 8, 8, width), 0)
                out.append((l8, acc + _dot(vt_ref[g, first_tile + i], pe.astype(BF16))))
            return tuple(out)

        init = tuple((jnp.zeros((8, width), F32), jnp.zeros((dh, width), F32)) for _ in all_groups)
        res = for_tiles(n_tiles, step, init)
        return [(acc, 1.0 / jnp.sum(l8, 0, keepdims=True)) for l8, acc in res]

    m8_init = tuple(jnp.full((8, width), NEG_INF, F32) for _ in all_groups)

    tile = SEL_KEY_TILE

    def sel_scores(g, kt):
        rows = pl.ds(pl.multiple_of(kt * tile, tile), tile)
        return _dot(keys_ref[0, rows, g * KEY_WIDTH:(g + 1) * KEY_WIDTH], qa_ref[g])

    def sel_pass(kt, m8):
        out = []
        for g in all_groups:
            sc = sel_scores(g, kt)
            s_sel_ref[g, kt] = sc
            out.append(jnp.maximum(m8[g], row_max(sc)))
        return tuple(out)

    last = (q0 + Q_BLOCK - 1) // tile
    m8 = list(for_tiles(last, sel_pass, m8_init))
    visible = last * tile + lax.broadcasted_iota(jnp.int32, (tile, width), 0) <= t_row
    for g in all_groups:
        sc = jnp.where(visible, sel_scores(g, last), NEG_INF)
        s_sel_ref[g, last] = sc
        m8[g] = jnp.maximum(m8[g], row_max(sc))
    o_sel = softmax_values(s_sel_ref, vst_ref, 0, last + 1, m8)

    tile = WIN_KEY_TILE
    first = jnp.maximum(q0 - WINDOW, 0) // tile
    n_win = (q0 + Q_BLOCK - 1) // tile - first + 1
    key_row = lax.broadcasted_iota(jnp.int32, (tile, width), 0)

    def win_pass(i, m8):
        kt = first + i
        rows = pl.ds(pl.multiple_of(kt * tile, tile), tile)
        dist = (t_row - kt * tile) - key_row
        in_window = (dist >= 0) & (dist < WINDOW)
        out = []
        for g in all_groups:
            k_tile = keys_ref[0, rows, (groups + g) * KEY_WIDTH:(groups + g + 1) * KEY_WIDTH]
            sc = jnp.where(in_window, _dot(k_tile, qa_ref[g]), NEG_INF)
            s_win_ref[g, i] = sc
            out.append(jnp.maximum(m8[g], row_max(sc)))
        return tuple(out)

    m8 = for_tiles(n_win, win_pass, m8_init)
    o_win = softmax_values(s_win_ref, vwt_ref, first, n_win, m8)

    gates_t = gates_ref[0].T
    for g in all_groups:
        per_head = []
        for h in range(NSA_HPG):
            lanes = slice(h * Q_BLOCK, (h + 1) * Q_BLOCK)
            row = (g * NSA_HPG + h) * 3
            (acc_sel, inv_sel), (acc_win, inv_win) = o_sel[g], o_win[g]
            per_head.append(gates_t[row:row + 1, :] * o_cmp[g][:, lanes]
                            + (gates_t[row + 1:row + 2, :] * inv_sel[:, lanes]) * acc_sel[:, lanes]
                            + (gates_t[row + 2:row + 3, :] * inv_win[:, lanes]) * acc_win[:, lanes])
        for p in range(NSA_HPG // 2):
            lanes0 = (g * NSA_HPG + 2 * p) * dh
            two_heads = jnp.concatenate(per_head[2 * p:2 * p + 2], 0)
            o_ref[0, :, lanes0:lanes0 + pair] = two_heads.T.astype(o_ref.dtype)


def _nsa_attention(q, kc, vct, keys, vals, gates, groups):
    batch, seq, d = q.shape
    dh = vct.shape[2]
    width = NSA_HPG * Q_BLOCK
    n_qb = seq // Q_BLOCK
    n_sel = seq // SEL_BLOCK
    top = min(SEL_TOP, n_sel)
    assert dh + POS_ROWS + n_sel <= KEY_WIDTH
    per_b = lambda shape: pl.BlockSpec((1,) + shape, lambda b, i: (b,) + (0,) * len(shape))
    per_q = lambda lanes: pl.BlockSpec((1, Q_BLOCK, lanes), lambda b, i: (b, i, 0))
    n_win_tiles = min(WINDOW + Q_BLOCK, seq) // WIN_KEY_TILE
    return pl.pallas_call(
        functools.partial(_nsa_attn_kernel, n_sel=n_sel, top=top),
        grid=(batch, n_qb),
        in_specs=[
            per_q(d),
            per_b(kc.shape[1:]), per_b(vct.shape[1:]),
            per_b(keys.shape[1:]), per_b(vals.shape[1:]),
            per_q(gates.shape[2]),
        ],
        out_specs=per_q(d),
        out_shape=jax.ShapeDtypeStruct((batch, seq, d), BF16),
        scratch_shapes=[pltpu.VMEM((groups, KEY_WIDTH, width), BF16),
                        pltpu.VMEM((groups, seq // SEL_KEY_TILE, dh, SEL_KEY_TILE), BF16),
                        pltpu.VMEM((groups, seq // WIN_KEY_TILE, dh, WIN_KEY_TILE), BF16),
                        pltpu.VMEM((groups, seq // SEL_KEY_TILE, SEL_KEY_TILE, width), F32),
                        pltpu.VMEM((groups, n_win_tiles, WIN_KEY_TILE, width), F32)],
        compiler_params=_params("parallel", "arbitrary"),
        name="nsa_attention",
    )(q, kc, vct, keys, vals, gates)


def _key_features(pos, dh, n_sel):
    block, offset = pos // SEL_BLOCK, pos % SEL_BLOCK
    feat = np.zeros((pos.shape[0], KEY_WIDTH), np.float32)
    feat[:, dh:dh + 3] = block[:, None]
    feat[:, dh + 3:dh + 6] = offset[:, None]
    if n_sel:
        feat[:, KEY_WIDTH - n_sel:] = block[:, None] == np.arange(n_sel)[None, :]
    return feat


def _nsa_weights(w_in, d, kv, groups):
    dh = kv // groups
    part = lambda j: w_in[:, d + j * kv:d + (j + 1) * kv]
    pad_keys = lambda w: jnp.pad(w.reshape(-1, groups, dh), ((0, 0), (0, 0), (0, KEY_WIDTH - dh))).reshape(-1, groups * KEY_WIDTH)
    gate_w = w_in[:, d + 6 * kv:]
    gate_w = jnp.pad(gate_w, ((0, 0), (0, -gate_w.shape[1] % 128)))
    cols = [w_in[:, :d], part(0), part(1), pad_keys(part(2)), pad_keys(part(4)), part(3), part(5), gate_w]
    return jnp.concatenate(cols, 1).astype(BF16)


def _nsa_mixer(x, mod, w_in, pe_k, pe_v, k1, k2, v1, v2, batch, seq):
    d = x.shape[1]
    groups = NSA_GROUPS
    dh = d // NSA_HEADS
    kv = groups * dh
    n_sel = seq // SEL_BLOCK
    positions = np.arange(seq)
    feat = jnp.asarray(np.concatenate([_key_features(positions, dh, n_sel), _key_features(positions, dh, 0)], 1), BF16)
    q, kcv, keys, vals, gates = _nsa_proj(x, mod, 3, _nsa_weights(w_in, d, kv, groups), feat, seq, d, kv,
                                          dh ** -0.5 * LOG2_E)
    kc, vc = _compress(kcv, pe_k, pe_v, k1, k2, v1, v2, groups, batch)
    cmp_end = np.arange(seq // CMP_STRIDE) * CMP_STRIDE + CMP_BLOCK - 1
    cmp_feat = jnp.asarray(_key_features(cmp_end, dh, 0)[:, dh:], BF16)
    kc = jnp.concatenate([kc, jnp.broadcast_to(cmp_feat, kc.shape[:2] + cmp_feat.shape)], -1)
    o = _nsa_attention(q.reshape(batch, seq, d), kc, vc.transpose(0, 1, 3, 2), keys.reshape(batch, seq, -1),
                       vals.reshape(batch, seq, -1), gates.reshape(batch, seq, -1), groups)
    return o.reshape(batch * seq, d)


def kernel(x, c, ada_w, ada_b, ln_g, ln_b, ffn_w_in, ffn_w_out, ret_w_in, ret_w_out, nsa_w_in, nsa_w_out,
           nsa_pe_k, nsa_pe_v, nsa_ck_w1, nsa_ck_w2, nsa_cv_w1, nsa_cv_w2):
    batch, seq, d = x.shape
    depth = ada_w.shape[0]
    assert depth == DEPTH
    mod_all = _ada_mod(c, ada_w, ada_b).reshape(depth, batch, 9, d)
    xf = x.reshape(batch * seq, d)
    dk = d // RET_HEADS
    dv = 2 * dk
    for i in range(depth):
        mod = mod_all[i]
        ln = lambda k: (ln_g[i, k].reshape(1, d), ln_b[i, k].reshape(1, d))
        xf = _ffn(xf, mod, 0, ffn_w_in[i, 0].astype(BF16), ffn_w_out[i, 0].astype(BF16), *ln(0), seq)
        j = i // N_MIXERS
        if i % N_MIXERS == 0:
            proj = _proj(xf, mod, 3, ret_w_in[j].astype(BF16), seq, 2 * MXU_WIDTH)
            y = _retention_core(proj.reshape(batch, seq, -1), batch, seq, dk, dv).reshape(batch * seq, -1)
            w_out = ret_w_out[j]
        else:
            y = _nsa_mixer(xf, mod, nsa_w_in[j], nsa_pe_k[j].reshape(1, -1), nsa_pe_v[j].reshape(1, -1),
                           nsa_ck_w1[j].astype(BF16), nsa_ck_w2[j].astype(BF16),
                           nsa_cv_w1[j].astype(BF16), nsa_cv_w2[j].astype(BF16), batch, seq)
            w_out = nsa_w_out[j]
        xf = _mixer_out_ffn(y, xf, mod, w_out.astype(BF16), ln(1),
                            ffn_w_in[i, 1].astype(BF16), ffn_w_out[i, 1].astype(BF16), ln(2), seq)
    return xf.reshape(batch, seq, d)
```

```python
import functools

import numpy as np
import jax
import jax.numpy as jnp
from jax import lax
from jax.experimental import pallas as pl
from jax.experimental.pallas import tpu as pltpu

F32 = jnp.float32
BF16 = jnp.bfloat16

DEPTH = 4
N_MIXERS = 2
RET_HEADS = 4
NSA_HEADS = 16
NSA_GROUPS = 4
NSA_HPG = NSA_HEADS // NSA_GROUPS
CMP_BLOCK = 32
CMP_STRIDE = 16
SEL_BLOCK = 64
SEL_TOP = 16
WINDOW = 512
Q_BLOCK = 128
FORCED_SCORE = 1e4
INVALID_SCORE = -1.0
FFN_RES = 0.5
DN_ALPHA = (2 * DEPTH) ** 0.25
LN_EPS = 1e-5
GN_EPS = 1e-6
NEG_INF = -1e30

V7X_VMEM_LIMIT_BYTES = 56 * 1024 * 1024
MXU_WIDTH = 256
TOKEN_TILE = 1024
RET_CHUNK = 256
RET_HEADS_PER_STEP = 2
SEL_KEY_TILE = 256
WIN_KEY_TILE = 128
TILE_UNROLL = 4
KEY_WIDTH = 128
POS_ROWS = 16
LOG2_E = 1.4426950408889634


def _dot(a, b):
    return jnp.dot(a, b, preferred_element_type=F32)


def _sigmoid(x):
    return 1.0 / (1.0 + jnp.exp(-x))


def _layer_norm(z, g, b):
    mu = jnp.mean(z, -1, keepdims=True)
    zc = z - mu
    var = jnp.mean(zc * zc, -1, keepdims=True)
    return zc * lax.rsqrt(var + LN_EPS) * g + b


def _params(*sem):
    return pltpu.CompilerParams(dimension_semantics=sem, vmem_limit_bytes=V7X_VMEM_LIMIT_BYTES)


def _resident(shape):
    return pl.BlockSpec(shape, lambda *_: (0,) * len(shape), pipeline_mode=pl.Buffered(1))


def _ada_kernel(c_ref, w_ref, b_ref, o_ref):
    c = c_ref[...]
    c_act = (c * _sigmoid(c)).astype(BF16)
    o_ref[0] = _dot(c_act, w_ref[0].astype(BF16)) + b_ref[0]


def _ada_mod(c, ada_w, ada_b):
    depth, d, n = ada_w.shape
    b = c.shape[0]
    tn = n // 8
    return pl.pallas_call(
        _ada_kernel,
        grid=(depth, n // tn),
        in_specs=[
            pl.BlockSpec((b, d), lambda l, j: (0, 0)),
            pl.BlockSpec((1, d, tn), lambda l, j: (l, 0, j)),
            pl.BlockSpec((1, 1, tn), lambda l, j: (l, 0, j)),
        ],
        out_specs=pl.BlockSpec((1, b, tn), lambda l, j: (l, 0, j)),
        out_shape=jax.ShapeDtypeStruct((depth, b, n), F32),
        compiler_params=_params("parallel", "parallel"),
        name="ada_mod",
    )(c, ada_w, ada_b.reshape(depth, 1, n))


def _swiglu_post_norm(x, mod_ref, row0, wi_ref, wo_ref, lng_ref, lnb_ref, h_ref, act_ref, n_chunks):
    shift = mod_ref[0, row0:row0 + 1, :]
    scale = mod_ref[0, row0 + 1:row0 + 2, :]
    gate = mod_ref[0, row0 + 2:row0 + 3, :]
    h_ref[...] = (x * (1.0 + scale) + shift).astype(BF16)
    f = wo_ref.shape[0]
    fc = f // n_chunks
    for j in range(n_chunks):
        h = h_ref[...]
        a = _dot(h, wi_ref[:, j * fc:(j + 1) * fc])
        u = _dot(h, wi_ref[:, f + j * fc:f + (j + 1) * fc])
        act_ref[:, j * fc:(j + 1) * fc] = (a * _sigmoid(a) * u).astype(BF16)
    y = _dot(act_ref[...], wo_ref[...])
    z = DN_ALPHA * x + FFN_RES * (1.0 + gate) * y
    return _layer_norm(z, lng_ref[...], lnb_ref[...])


def _ffn_kernel(x_ref, mod_ref, wi_ref, wo_ref, lng_ref, lnb_ref, o_ref, h_ref, act_ref, *, row0, n_chunks):
    o_ref[...] = _swiglu_post_norm(x_ref[...], mod_ref, row0, wi_ref, wo_ref, lng_ref, lnb_ref, h_ref, act_ref, n_chunks)


def _mixer_out_ffn_kernel(y_ref, x_ref, mod_ref, wm_ref, lng1_ref, lnb1_ref, wi_ref, wo_ref, lng2_ref, lnb2_ref,
                          o_ref, h_ref, act_ref, *, n_chunks):
    gate = mod_ref[0, 5:6, :]
    z = DN_ALPHA * x_ref[...] + (1.0 + gate) * _dot(y_ref[...], wm_ref[...])
    x1 = _layer_norm(z, lng1_ref[...], lnb1_ref[...])
    o_ref[...] = _swiglu_post_norm(x1, mod_ref, 6, wi_ref, wo_ref, lng2_ref, lnb2_ref, h_ref, act_ref, n_chunks)


def _ffn(x, mod, row0, w_in, w_out, ln_g, ln_b, seq):
    n_tok, d = x.shape
    f = w_out.shape[0]
    tm = min(TOKEN_TILE, seq)
    tiles_per_seq = seq // tm
    return pl.pallas_call(
        functools.partial(_ffn_kernel, row0=row0, n_chunks=f // MXU_WIDTH),
        grid=(n_tok // tm,),
        in_specs=[
            pl.BlockSpec((tm, d), lambda i: (i, 0)),
            pl.BlockSpec((1, 9, d), lambda i: (i // tiles_per_seq, 0, 0)),
            _resident(w_in.shape),
            _resident(w_out.shape),
            _resident((1, d)),
            _resident((1, d)),
        ],
        out_specs=pl.BlockSpec((tm, d), lambda i: (i, 0)),
        out_shape=jax.ShapeDtypeStruct((n_tok, d), F32),
        scratch_shapes=[pltpu.VMEM((tm, d), BF16), pltpu.VMEM((tm, f), BF16)],
        compiler_params=_params("parallel"),
        name="ffn",
    )(x, mod, w_in, w_out, ln_g, ln_b)


def _mixer_out_ffn(y, x, mod, w_mix, ln1, w_in, w_out, ln2, seq):
    n_tok, d = x.shape
    k = y.shape[1]
    f = w_out.shape[0]
    tm = min(TOKEN_TILE, seq)
    tiles_per_seq = seq // tm
    return pl.pallas_call(
        functools.partial(_mixer_out_ffn_kernel, n_chunks=f // MXU_WIDTH),
        grid=(n_tok // tm,),
        in_specs=[
            pl.BlockSpec((tm, k), lambda i: (i, 0)),
            pl.BlockSpec((tm, d), lambda i: (i, 0)),
            pl.BlockSpec((1, 9, d), lambda i: (i // tiles_per_seq, 0, 0)),
            _resident(w_mix.shape), _resident((1, d)), _resident((1, d)),
            _resident(w_in.shape), _resident(w_out.shape), _resident((1, d)), _resident((1, d)),
        ],
        out_specs=pl.BlockSpec((tm, d), lambda i: (i, 0)),
        out_shape=jax.ShapeDtypeStruct((n_tok, d), F32),
        scratch_shapes=[pltpu.VMEM((tm, d), BF16), pltpu.VMEM((tm, f), BF16)],
        compiler_params=_params("parallel"),
        name="mixer_out_ffn",
    )(y, x, mod, w_mix, *ln1, w_in, w_out, *ln2)


def _proj_kernel(x_ref, mod_ref, w_ref, o_ref, *, row0, tn):
    x = x_ref[...]
    shift = mod_ref[0, row0:row0 + 1, :]
    scale = mod_ref[0, row0 + 1:row0 + 2, :]
    h = (x * (1.0 + scale) + shift).astype(BF16)
    for j in range(o_ref.shape[1] // tn):
        o_ref[:, j * tn:(j + 1) * tn] = _dot(h, w_ref[:, j * tn:(j + 1) * tn]).astype(o_ref.dtype)


def _proj(x, mod, row0, w, seq, tn):
    n_tok, d = x.shape
    n = w.shape[1]
    tm = min(TOKEN_TILE, seq)
    tiles_per_seq = seq // tm
    return pl.pallas_call(
        functools.partial(_proj_kernel, row0=row0, tn=tn),
        grid=(n_tok // tm,),
        in_specs=[
            pl.BlockSpec((tm, d), lambda i: (i, 0)),
            pl.BlockSpec((1, 9, d), lambda i: (i // tiles_per_seq, 0, 0)),
            _resident((d, n)),
        ],
        out_specs=pl.BlockSpec((tm, n), lambda i: (i, 0)),
        out_shape=jax.ShapeDtypeStruct((n_tok, n), BF16),
        compiler_params=_params("parallel"),
        name="mixer_in_proj",
    )(x, mod, w)


def _nsa_proj_kernel(x_ref, mod_ref, w_ref, feat_ref, q_ref, kcv_ref, keys_ref, vals_ref, gates_ref, *, row0, q_scale):
    x = x_ref[...]
    shift = mod_ref[0, row0:row0 + 1, :]
    scale = mod_ref[0, row0 + 1:row0 + 2, :]
    h = (x * (1.0 + scale) + shift).astype(BF16)
    half = keys_ref.shape[1] // 2
    lane_tile = kcv_ref.shape[2]
    col = 0
    for ref in (q_ref, kcv_ref, keys_ref, vals_ref, gates_ref):
        n = ref.shape[1] if ref is not kcv_ref else kcv_ref.shape[0] * lane_tile
        step = min(n, MXU_WIDTH)
        for j in range(0, n, step):
            y = _dot(h, w_ref[:, col + j:col + j + step])
            if ref is q_ref:
                y = y * q_scale
            elif ref is keys_ref:
                f = feat_ref[:, 0:KEY_WIDTH] if j < half else feat_ref[:, KEY_WIDTH:2 * KEY_WIDTH]
                y = y + jnp.concatenate([f] * (step // KEY_WIDTH), 1).astype(F32)
            elif ref is gates_ref:
                y = _sigmoid(y)
            if ref is kcv_ref:
                for c in range(step // lane_tile):
                    kcv_ref[(j + c * lane_tile) // lane_tile] = y[:, c * lane_tile:(c + 1) * lane_tile]
            else:
                ref[:, j:j + step] = y.astype(ref.dtype)
        col += n


def _nsa_proj(x, mod, row0, w, feat, seq, d, kv, q_scale):
    n_tok = x.shape[0]
    groups = NSA_GROUPS
    tm = min(TOKEN_TILE, seq)
    tiles_per_seq = seq // tm
    widths = (d, 2 * kv, 2 * groups * KEY_WIDTH, 2 * kv, w.shape[1] - d - 4 * kv - 2 * groups * KEY_WIDTH)
    dtypes = (BF16, F32, BF16, BF16, F32)
    lane = 128
    return pl.pallas_call(
        functools.partial(_nsa_proj_kernel, row0=row0, q_scale=q_scale),
        grid=(n_tok // tm,),
        in_specs=[
            pl.BlockSpec((tm, d), lambda i: (i, 0)),
            pl.BlockSpec((1, 9, d), lambda i: (i // tiles_per_seq, 0, 0)),
            _resident(w.shape),
            pl.BlockSpec((tm, 2 * KEY_WIDTH), lambda i: (i % tiles_per_seq, 0)),
        ],
        out_specs=[pl.BlockSpec((tm, n), lambda i: (i, 0)) if k != 1 else
                   pl.BlockSpec((n // lane, tm, lane), lambda i: (0, i, 0)) for k, n in enumerate(widths)],
        out_shape=[jax.ShapeDtypeStruct((n_tok, n) if k != 1 else (n // lane, n_tok, lane), dt)
                   for k, (n, dt) in enumerate(zip(widths, dtypes))],
        compiler_params=_params("parallel"),
        name="nsa_in_proj",
    )(x, mod, w, feat)


def _ret_kernel(q_ref, k_ref, v_ref, g_ref, o_ref, state_ref, *, chunk, n_chunks, k_scale):
    heads = state_ref.shape[0]
    dk, dv = state_ref.shape[1], state_ref.shape[2]
    row = lax.broadcasted_iota(jnp.int32, (chunk, chunk), 0)
    col = lax.broadcasted_iota(jnp.int32, (chunk, chunk), 1)
    diff = (row - col).astype(F32)
    pos = lax.broadcasted_iota(jnp.int32, (chunk, 1), 0).astype(F32)
    decays = []
    for hh in range(heads):
        head = (pl.program_id(1) * heads + hh).astype(F32)
        log_g = jnp.log(1.0 - jnp.exp2(-5.0 - (jnp.zeros((1, 1), F32) + head)))
        decays.append((jnp.where(diff >= 0, jnp.exp(log_g * jnp.maximum(diff, 0.0)), 0.0),
                       jnp.exp(log_g * (pos + 1.0)),
                       jnp.exp(log_g * (chunk - 1.0 - pos)),
                       jnp.exp(log_g * float(chunk))))
    state_ref[...] = jnp.zeros_like(state_ref)

    def step(c, carry):
        rows = pl.ds(pl.multiple_of(c * chunk, chunk), chunk)
        for hh in range(heads):
            decay_intra, decay_q, decay_k, decay_state = decays[hh]
            q = q_ref[0, rows, hh * dk:(hh + 1) * dk]
            k = k_ref[0, rows, hh * dk:(hh + 1) * dk] * k_scale
            v = v_ref[0, rows, hh * dv:(hh + 1) * dv]
            scores = lax.dot_general(q, k, (((1,), (1,)), ((), ())), preferred_element_type=F32) * decay_intra
            state = state_ref[hh]
            o = _dot(scores.astype(BF16), v) + _dot((q.astype(F32) * decay_q).astype(BF16), state.astype(BF16))
            k_dec_t = (k.astype(F32) * decay_k).T.astype(BF16)
            state_ref[hh] = decay_state * state + _dot(k_dec_t, v)
            mu = jnp.mean(o, -1, keepdims=True)
            oc = o - mu
            var = jnp.mean(oc * oc, -1, keepdims=True)
            o = oc * lax.rsqrt(var + GN_EPS)
            g = g_ref[0, rows, hh * dv:(hh + 1) * dv].astype(F32)
            o_ref[0, rows, hh * dv:(hh + 1) * dv] = (o * (g * _sigmoid(g))).astype(o_ref.dtype)
        return carry

    lax.fori_loop(0, n_chunks, step, 0)


def _retention_core(proj, batch, seq, dk, dv):
    heads = RET_HEADS
    hps = RET_HEADS_PER_STEP
    chunk = min(RET_CHUNK, seq)
    k_blk0 = heads * dk // (hps * dk)
    v_blk0 = 2 * heads * dk // (hps * dv)
    g_blk0 = (2 * heads * dk + heads * dv) // (hps * dv)
    return pl.pallas_call(
        functools.partial(_ret_kernel, chunk=chunk, n_chunks=seq // chunk, k_scale=dk ** -0.5),
        grid=(batch, heads // hps),
        in_specs=[
            pl.BlockSpec((1, seq, hps * dk), lambda b, h: (b, 0, h)),
            pl.BlockSpec((1, seq, hps * dk), lambda b, h: (b, 0, k_blk0 + h)),
            pl.BlockSpec((1, seq, hps * dv), lambda b, h: (b, 0, v_blk0 + h)),
            pl.BlockSpec((1, seq, hps * dv), lambda b, h: (b, 0, g_blk0 + h)),
        ],
        out_specs=pl.BlockSpec((1, seq, hps * dv), lambda b, h: (b, 0, h)),
        out_shape=jax.ShapeDtypeStruct((batch, seq, heads * dv), BF16),
        scratch_shapes=[pltpu.VMEM((hps, dk, dv), F32)],
        compiler_params=_params("parallel", "parallel"),
        name="retention_core",
    )(proj, proj, proj, proj)


def _gelu_tanh(x):
    return 0.5 * x * (1.0 + jnp.tanh(0.7978845608028654 * (x + 0.044715 * (x * x * x))))


def _compress_kernel(x_ref, pek_ref, pev_ref, k1_ref, k2_ref, v1_ref, v2_ref, ok_ref, ov_ref, y_ref):
    groups, n_rows, half = y_ref.shape
    dh = half // CMP_STRIDE
    kv = groups * dh
    per_tile = x_ref.shape[2] // dh

    def mlp(lane0, pe_ref, w1_ref, w2_ref, out_ref):
        for r in range(CMP_STRIDE):
            for c in range(groups // per_tile):
                piece = x_ref[lane0 // x_ref.shape[2] + c, pl.ds(r, n_rows, stride=CMP_STRIDE), :]
                for k in range(per_tile):
                    y_ref[c * per_tile + k, :, r * dh:(r + 1) * dh] = piece[:, k * dh:(k + 1) * dh].astype(BF16)
        bias = _dot(pe_ref[...].astype(BF16), w1_ref[...])
        for g in range(groups):
            y = y_ref[g]
            first = _dot(y, w1_ref[0:half, :])
            second = _dot(y, w1_ref[half:2 * half, :])
            hidden = first + pltpu.roll(second, n_rows - 1, 0) + bias
            out_ref[0, g] = _dot(_gelu_tanh(hidden).astype(BF16), w2_ref[...]).astype(out_ref.dtype)

    mlp(0, pek_ref, k1_ref, k2_ref, ok_ref)
    mlp(kv, pev_ref, v1_ref, v2_ref, ov_ref)


def _compress(kcv, pe_k, pe_v, k1, k2, v1, v2, groups, batch):
    lane_tiles, n_tok, lane = kcv.shape
    seq = n_tok // batch
    rows = seq // CMP_STRIDE
    hidden = k1.shape[1]
    dh = k2.shape[1]
    out_blk = pl.BlockSpec((1, groups, rows, dh), lambda b: (b, 0, 0, 0))
    out_sds = jax.ShapeDtypeStruct((batch, groups, rows, dh), BF16)
    return pl.pallas_call(
        _compress_kernel,
        grid=(batch,),
        in_specs=[pl.BlockSpec((lane_tiles, seq, lane), lambda b: (0, b, 0)),
                  _resident(pe_k.shape), _resident(pe_v.shape),
                  _resident(k1.shape), _resident(k2.shape), _resident(v1.shape), _resident(v2.shape)],
        out_specs=[out_blk, out_blk],
        out_shape=[out_sds, out_sds],
        scratch_shapes=[pltpu.VMEM((groups, rows, CMP_STRIDE * dh), BF16)],
        compiler_params=_params("parallel"),
        name="nsa_compress",
    )(kcv, pe_k, pe_v, k1, k2, v1, v2)


def _nsa_attn_kernel(q_ref, kc_ref, vct_ref, keys_ref, vals_ref, gates_ref, o_ref,
                     qa_ref, vst_ref, vwt_ref, s_sel_ref, s_win_ref, acc_ref, l8_ref, *, n_sel, top):
    qb = pl.program_id(1)
    q0 = qb * Q_BLOCK
    groups, dh, width = qa_ref.shape[0], vct_ref.shape[2], qa_ref.shape[2]
    kv = groups * dh
    sel_row0 = KEY_WIDTH - n_sel
    all_groups = range(groups)
    pair = 2 * dh
    assert pair == Q_BLOCK == KEY_WIDTH

    @pl.when(qb == 0)
    def _():
        def sel_tile(i, carry):
            rows = pl.ds(pl.multiple_of(i * SEL_KEY_TILE, SEL_KEY_TILE), SEL_KEY_TILE)
            vt = vals_ref[0, rows, 0:kv].astype(F32).T
            for g in all_groups:
                vst_ref[g, i] = vt[g * dh:(g + 1) * dh, :].astype(BF16)
            return carry

        def win_tile(i, carry):
            rows = pl.ds(pl.multiple_of(i * WIN_KEY_TILE, WIN_KEY_TILE), WIN_KEY_TILE)
            vt = vals_ref[0, rows, kv:2 * kv].astype(F32).T
            for g in all_groups:
                vwt_ref[g, i] = vt[g * dh:(g + 1) * dh, :].astype(BF16)
            return carry

        lax.fori_loop(0, vst_ref.shape[1], sel_tile, 0)
        lax.fori_loop(0, vwt_ref.shape[1], win_tile, 0)

    lane = lax.broadcasted_iota(jnp.int32, (1, width), 1)
    t_row = q0 + lane % Q_BLOCK
    piece_row = lax.broadcasted_iota(jnp.int32, (POS_ROWS, width), 0)

    for g in all_groups:
        head = (g * NSA_HPG + lane // Q_BLOCK).astype(F32)
        slope = jnp.exp2(-8.0 * (head + 1.0) / NSA_HEADS) * LOG2_E
        hi = slope.astype(BF16).astype(F32)
        rest = slope - hi
        mid = rest.astype(BF16).astype(F32)
        lo = (rest - mid).astype(BF16).astype(F32)
        feat = jnp.zeros((POS_ROWS, width), F32)
        for idx, piece in enumerate((SEL_BLOCK * hi, SEL_BLOCK * mid, SEL_BLOCK * lo, hi, mid, lo)):
            feat = jnp.where(piece_row == idx, piece, feat)
        for p in range(NSA_HPG // 2):
            lanes0 = (g * NSA_HPG + 2 * p) * dh
            two_heads = q_ref[0, :, lanes0:lanes0 + pair].astype(F32).T
            qa_ref[g, 0:dh, (2 * p) * Q_BLOCK:(2 * p + 1) * Q_BLOCK] = two_heads[0:dh].astype(BF16)
            qa_ref[g, 0:dh, (2 * p + 1) * Q_BLOCK:(2 * p + 2) * Q_BLOCK] = two_heads[dh:pair].astype(BF16)
        qa_ref[g, dh:dh + POS_ROWS, :] = feat.astype(BF16)
        qa_ref[g, dh + POS_ROWS:KEY_WIDTH, :] = jnp.zeros((KEY_WIDTH - dh - POS_ROWS, width), BF16)

    n_cmp_pad = kc_ref.shape[2]
    blk = lax.broadcasted_iota(jnp.int32, (n_cmp_pad, width), 0)
    mask_c = blk * CMP_STRIDE + (CMP_BLOCK - 1) <= t_row
    any_visible = t_row >= CMP_BLOCK - 1
    scores_c = [_dot(kc_ref[0, g], qa_ref[g]) for g in all_groups]
    p_cmp = []
    for g in all_groups:
        s = jnp.where(mask_c, scores_c[g], NEG_INF)
        e = jnp.exp2(s - jnp.max(s, 0, keepdims=True))
        p_cmp.append(e * jnp.where(any_visible, 1.0 / jnp.sum(e, 0, keepdims=True), 0.0))
    o_cmp = [_dot(vct_ref[0, g], p_cmp[g].astype(BF16)) for g in all_groups]

    sel_i = lax.broadcasted_iota(jnp.int32, (n_sel, n_cmp_pad), 0) * SEL_BLOCK
    cmp_i = lax.broadcasted_iota(jnp.int32, (n_sel, n_cmp_pad), 1) * CMP_STRIDE
    overlap = jnp.maximum(jnp.minimum(cmp_i + CMP_BLOCK, sel_i + SEL_BLOCK) - jnp.maximum(cmp_i, sel_i), 0)
    overlap = (overlap.astype(F32) * (1.0 / CMP_BLOCK)).astype(BF16)
    j_idx = lax.broadcasted_iota(jnp.int32, (n_sel, Q_BLOCK), 0)
    t_sel = q0 + lax.broadcasted_iota(jnp.int32, (n_sel, Q_BLOCK), 1)
    cur = t_sel // SEL_BLOCK
    forced = (j_idx == 0) | (j_idx == cur) | (j_idx == cur - 1)
    valid = j_idx * SEL_BLOCK <= t_sel
    pieces = []
    for g in all_groups:
        p_sum = p_cmp[g][:, 0:Q_BLOCK]
        for h in range(1, NSA_HPG):
            p_sum = p_sum + p_cmp[g][:, h * Q_BLOCK:(h + 1) * Q_BLOCK]
        p_hi = p_sum.astype(BF16)
        rest = p_sum - p_hi.astype(F32)
        p_mid = rest.astype(BF16)
        pieces.append((p_hi, p_mid, (rest - p_mid.astype(F32)).astype(BF16)))
    imps = [_dot(overlap, hi) + _dot(overlap, mid) + _dot(overlap, lo) for hi, mid, lo in pieces]
    for g in all_groups:
        score = jnp.where(valid, jnp.where(forced, FORCED_SCORE, imps[g]), INVALID_SCORE)
        tiles = [score[r:r + 8, :] for r in range(0, n_sel, 8)]
        ranks = [jnp.zeros((8, Q_BLOCK), F32) for _ in tiles]
        for jp in range(n_sel):
            other = jnp.broadcast_to(score[jp:jp + 1, :], (8, Q_BLOCK))
            for v, tile_scores in enumerate(tiles):
                if jp < 8 * v:
                    ahead = other >= tile_scores
                elif jp >= 8 * v + 7:
                    ahead = other > tile_scores
                else:
                    ahead = (other > tile_scores) | ((other == tile_scores) & (j_idx[0:8, :] > jp - 8 * v))
                ranks[v] = ranks[v] + jnp.where(ahead, 1.0, 0.0)
        rank = jnp.concatenate(ranks, 0)
        block_bias = jnp.where(rank < top, 0.0, NEG_INF)
        qa_ref[g, sel_row0:KEY_WIDTH, :] = jnp.concatenate([block_bias] * NSA_HPG, 1).astype(BF16)

    def row_max(sc):
        return jnp.max(sc.reshape(sc.shape[0] // 8, 8, width), 0)

    def for_tiles(n, body, init):
        carry, start, width_ = init, 0, TILE_UNROLL
        while width_ >= 1:
            def several(j, c, start=start, width_=width_):
                for u in range(width_):
                    c = body(start + width_ * j + u, c)
                return c
            trips = (n - start) // width_
            carry = lax.fori_loop(0, trips, several, carry)
            start = start + trips * width_
            width_ //= 2
        return carry

    def softmax_values(s_ref, vt_ref, first_tile, n_tiles, m8):
        m = [jnp.max(m8[g], 0, keepdims=True) for g in all_groups]
        acc_ref[...] = jnp.zeros_like(acc_ref)
        l8_ref[...] = jnp.zeros_like(l8_ref)

        def step(i, carry):
            pes = [jnp.exp2(s_ref[g, i] - m[g]) for g in all_groups]
            for g in all_groups:
                l8_ref[g] += jnp.sum(pes[g].reshape(pes[g].shape[0] // 8, 8, width), 0)
                acc_ref[g] += _dot(vt_ref[g, first_tile + i], pes[g].astype(BF16))
            return carry

        for_tiles(n_tiles, step, 0)
        return [(acc_ref[g], 1.0 / jnp.sum(l8_ref[g], 0, keepdims=True)) for g in all_groups]

    m8_init = tuple(jnp.full((8, width), NEG_INF, F32) for _ in all_groups)

    tile = SEL_KEY_TILE

    def sel_scores(g, kt):
        rows = pl.ds(pl.multiple_of(kt * tile, tile), tile)
        return _dot(keys_ref[0, rows, g * KEY_WIDTH:(g + 1) * KEY_WIDTH], qa_ref[g])

    def sel_pass(kt, m8):
        out = []
        for g in all_groups:
            sc = sel_scores(g, kt)
            s_sel_ref[g, kt] = sc
            out.append(jnp.maximum(m8[g], row_max(sc)))
        return tuple(out)

    last = (q0 + Q_BLOCK - 1) // tile
    m8 = list(for_tiles(last, sel_pass, m8_init))
    visible = last * tile + lax.broadcasted_iota(jnp.int32, (tile, width), 0) <= t_row
    last_scores = [sel_scores(g, last) for g in all_groups]
    for g in all_groups:
        sc = jnp.where(visible, last_scores[g], NEG_INF)
        s_sel_ref[g, last] = sc
        m8[g] = jnp.maximum(m8[g], row_max(sc))
    o_sel = softmax_values(s_sel_ref, vst_ref, 0, last + 1, m8)

    tile = WIN_KEY_TILE
    first = jnp.maximum(q0 - WINDOW, 0) // tile
    n_win = (q0 + Q_BLOCK - 1) // tile - first + 1
    key_row = lax.broadcasted_iota(jnp.int32, (tile, width), 0)

    def win_pass(i, m8):
        kt = first + i
        rows = pl.ds(pl.multiple_of(kt * tile, tile), tile)
        dist = (t_row - kt * tile) - key_row
        in_window = (dist >= 0) & (dist < WINDOW)
        out = []
        for g in all_groups:
            k_tile = keys_ref[0, rows, (groups + g) * KEY_WIDTH:(groups + g + 1) * KEY_WIDTH]
            sc = jnp.where(in_window, _dot(k_tile, qa_ref[g]), NEG_INF)
            s_win_ref[g, i] = sc
            out.append(jnp.maximum(m8[g], row_max(sc)))
        return tuple(out)

    m8 = for_tiles(n_win, win_pass, m8_init)
    o_win = softmax_values(s_win_ref, vwt_ref, first, n_win, m8)

    gates_t = gates_ref[0].T
    for g in all_groups:
        per_head = []
        for h in range(NSA_HPG):
            lanes = slice(h * Q_BLOCK, (h + 1) * Q_BLOCK)
            row = (g * NSA_HPG + h) * 3
            (acc_sel, inv_sel), (acc_win, inv_win) = o_sel[g], o_win[g]
            per_head.append(gates_t[row:row + 1, :] * o_cmp[g][:, lanes]
                            + (gates_t[row + 1:row + 2, :] * inv_sel[:, lanes]) * acc_sel[:, lanes]
                            + (gates_t[row + 2:row + 3, :] * inv_win[:, lanes]) * acc_win[:, lanes])
        for p in range(NSA_HPG // 2):
            lanes0 = (g * NSA_HPG + 2 * p) * dh
            two_heads = jnp.concatenate(per_head[2 * p:2 * p + 2], 0)
            o_ref[0, :, lanes0:lanes0 + pair] = two_heads.T.astype(o_ref.dtype)


def _nsa_attention(q, kc, vct, keys, vals, gates, groups):
    batch, seq, d = q.shape
    dh = vct.shape[2]
    width = NSA_HPG * Q_BLOCK
    n_qb = seq // Q_BLOCK
    n_sel = seq // SEL_BLOCK
    top = min(SEL_TOP, n_sel)
    assert dh + POS_ROWS + n_sel <= KEY_WIDTH
    per_b = lambda shape: pl.BlockSpec((1,) + shape, lambda b, i: (b,) + (0,) * len(shape))
    per_q = lambda lanes: pl.BlockSpec((1, Q_BLOCK, lanes), lambda b, i: (b, i, 0))
    n_win_tiles = min(WINDOW + Q_BLOCK, seq) // WIN_KEY_TILE
    return pl.pallas_call(
        functools.partial(_nsa_attn_kernel, n_sel=n_sel, top=top),
        grid=(batch, n_qb),
        in_specs=[
            per_q(d),
            per_b(kc.shape[1:]), per_b(vct.shape[1:]),
            per_b(keys.shape[1:]), per_b(vals.shape[1:]),
            per_q(gates.shape[2]),
        ],
        out_specs=per_q(d),
        out_shape=jax.ShapeDtypeStruct((batch, seq, d), BF16),
        scratch_shapes=[pltpu.VMEM((groups, KEY_WIDTH, width), BF16),
                        pltpu.VMEM((groups, seq // SEL_KEY_TILE, dh, SEL_KEY_TILE), BF16),
                        pltpu.VMEM((groups, seq // WIN_KEY_TILE, dh, WIN_KEY_TILE), BF16),
                        pltpu.VMEM((groups, seq // SEL_KEY_TILE, SEL_KEY_TILE, width), F32),
                        pltpu.VMEM((groups, n_win_tiles, WIN_KEY_TILE, width), F32),
                        pltpu.VMEM((groups, dh, width), F32),
                        pltpu.VMEM((groups, 8, width), F32)],
        compiler_params=_params("parallel", "arbitrary"),
        name="nsa_attention",
    )(q, kc, vct, keys, vals, gates)


def _key_features(pos, dh, n_sel):
    block, offset = pos // SEL_BLOCK, pos % SEL_BLOCK
    feat = np.zeros((pos.shape[0], KEY_WIDTH), np.float32)
    feat[:, dh:dh + 3] = block[:, None]
    feat[:, dh + 3:dh + 6] = offset[:, None]
    if n_sel:
        feat[:, KEY_WIDTH - n_sel:] = block[:, None] == np.arange(n_sel)[None, :]
    return feat


def _nsa_weights(w_in, d, kv, groups):
    dh = kv // groups
    part = lambda j: w_in[:, d + j * kv:d + (j + 1) * kv]
    pad_keys = lambda w: jnp.pad(w.reshape(-1, groups, dh), ((0, 0), (0, 0), (0, KEY_WIDTH - dh))).reshape(-1, groups * KEY_WIDTH)
    gate_w = w_in[:, d + 6 * kv:]
    gate_w = jnp.pad(gate_w, ((0, 0), (0, -gate_w.shape[1] % 128)))
    cols = [w_in[:, :d], part(0), part(1), pad_keys(part(2)), pad_keys(part(4)), part(3), part(5), gate_w]
    return jnp.concatenate(cols, 1).astype(BF16)


def _nsa_mixer(x, mod, w_in, pe_k, pe_v, k1, k2, v1, v2, batch, seq):
    d = x.shape[1]
    groups = NSA_GROUPS
    dh = d // NSA_HEADS
    kv = groups * dh
    n_sel = seq // SEL_BLOCK
    positions = np.arange(seq)
    feat = jnp.asarray(np.concatenate([_key_features(positions, dh, n_sel), _key_features(positions, dh, 0)], 1), BF16)
    q, kcv, keys, vals, gates = _nsa_proj(x, mod, 3, _nsa_weights(w_in, d, kv, groups), feat, seq, d, kv,
                                          dh ** -0.5 * LOG2_E)
    kc, vc = _compress(kcv, pe_k, pe_v, k1, k2, v1, v2, groups, batch)
    cmp_end = np.arange(seq // CMP_STRIDE) * CMP_STRIDE + CMP_BLOCK - 1
    cmp_feat = jnp.asarray(_key_features(cmp_end, dh, 0)[:, dh:], BF16)
    kc = jnp.concatenate([kc, jnp.broadcast_to(cmp_feat, kc.shape[:2] + cmp_feat.shape)], -1)
    o = _nsa_attention(q.reshape(batch, seq, d), kc, vc.transpose(0, 1, 3, 2), keys.reshape(batch, seq, -1),
                       vals.reshape(batch, seq, -1), gates.reshape(batch, seq, -1), groups)
    return o.reshape(batch * seq, d)


def kernel(x, c, ada_w, ada_b, ln_g, ln_b, ffn_w_in, ffn_w_out, ret_w_in, ret_w_out, nsa_w_in, nsa_w_out,
           nsa_pe_k, nsa_pe_v, nsa_ck_w1, nsa_ck_w2, nsa_cv_w1, nsa_cv_w2):
    batch, seq, d = x.shape
    depth = ada_w.shape[0]
    assert depth == DEPTH
    mod_all = _ada_mod(c, ada_w, ada_b).reshape(depth, batch, 9, d)
    xf = x.reshape(batch * seq, d)
    dk = d // RET_HEADS
    dv = 2 * dk
    for i in range(depth):
        mod = mod_all[i]
        ln = lambda k: (ln_g[i, k].reshape(1, d), ln_b[i, k].reshape(1, d))
        xf = _ffn(xf, mod, 0, ffn_w_in[i, 0].astype(BF16), ffn_w_out[i, 0].astype(BF16), *ln(0), seq)
        j = i // N_MIXERS
        if i % N_MIXERS == 0:
            proj = _proj(xf, mod, 3, ret_w_in[j].astype(BF16), seq, 2 * MXU_WIDTH)
            y = _retention_core(proj.reshape(batch, seq, -1), batch, seq, dk, dv).reshape(batch * seq, -1)
            w_out = ret_w_out[j]
        else:
            y = _nsa_mixer(xf, mod, nsa_w_in[j], nsa_pe_k[j].reshape(1, -1), nsa_pe_v[j].reshape(1, -1),
                           nsa_ck_w1[j].astype(BF16), nsa_ck_w2[j].astype(BF16),
                           nsa_cv_w1[j].astype(BF16), nsa_cv_w2[j].astype(BF16), batch, seq)
            w_out = nsa_w_out[j]
        xf = _mixer_out_ffn(y, xf, mod, w_out.astype(BF16), ln(1),
                            ffn_w_in[i, 1].astype(BF16), ffn_w_out[i, 1].astype(BF16), ln(2), seq)
    return xf.reshape(batch, seq, d)
```

```python
import functools

import numpy as np
import jax
import jax.numpy as jnp
from jax import lax
from jax.experimental import pallas as pl
from jax.experimental.pallas import tpu as pltpu

F32 = jnp.float32
BF16 = jnp.bfloat16

DEPTH = 4
N_MIXERS = 2
RET_HEADS = 4
NSA_HEADS = 16
NSA_GROUPS = 4
NSA_HPG = NSA_HEADS // NSA_GROUPS
CMP_BLOCK = 32
CMP_STRIDE = 16
SEL_BLOCK = 64
SEL_TOP = 16
WINDOW = 512
Q_BLOCK = 128
FORCED_SCORE = 1e4
INVALID_SCORE = -1.0
FFN_RES = 0.5
DN_ALPHA = (2 * DEPTH) ** 0.25
LN_EPS = 1e-5
GN_EPS = 1e-6
NEG_INF = -1e30

V7X_VMEM_LIMIT_BYTES = 56 * 1024 * 1024
MXU_WIDTH = 256
TOKEN_TILE = 1024
RET_CHUNK = 256
RET_HEADS_PER_STEP = 2
SEL_KEY_TILE = 256
WIN_KEY_TILE = 128
Q_BLOCKS_PER_STEP = 4
TILE_UNROLL = 4
KEY_WIDTH = 128
POS_ROWS = 16
LOG2_E = 1.4426950408889634


def _dot(a, b):
    return jnp.dot(a, b, preferred_element_type=F32)


def _sigmoid(x):
    return 1.0 / (1.0 + jnp.exp(-x))


def _layer_norm(z, g, b):
    mu = jnp.mean(z, -1, keepdims=True)
    zc = z - mu
    var = jnp.mean(zc * zc, -1, keepdims=True)
    return zc * lax.rsqrt(var + LN_EPS) * g + b


def _params(*sem):
    return pltpu.CompilerParams(dimension_semantics=sem, vmem_limit_bytes=V7X_VMEM_LIMIT_BYTES)


def _resident(shape):
    return pl.BlockSpec(shape, lambda *_: (0,) * len(shape), pipeline_mode=pl.Buffered(1))


def _ada_kernel(c_ref, w_ref, b_ref, o_ref):
    c = c_ref[...]
    c_act = (c * _sigmoid(c)).astype(BF16)
    o_ref[0] = _dot(c_act, w_ref[0].astype(BF16)) + b_ref[0]


def _ada_mod(c, ada_w, ada_b):
    depth, d, n = ada_w.shape
    b = c.shape[0]
    tn = n // 8
    return pl.pallas_call(
        _ada_kernel,
        grid=(depth, n // tn),
        in_specs=[
            pl.BlockSpec((b, d), lambda l, j: (0, 0)),
            pl.BlockSpec((1, d, tn), lambda l, j: (l, 0, j)),
            pl.BlockSpec((1, 1, tn), lambda l, j: (l, 0, j)),
        ],
        out_specs=pl.BlockSpec((1, b, tn), lambda l, j: (l, 0, j)),
        out_shape=jax.ShapeDtypeStruct((depth, b, n), F32),
        compiler_params=_params("parallel", "parallel"),
        name="ada_mod",
    )(c, ada_w, ada_b.reshape(depth, 1, n))


def _swiglu_post_norm(x, mod_ref, row0, wi_ref, wo_ref, lng_ref, lnb_ref, h_ref, act_ref, n_chunks):
    shift = mod_ref[0, row0:row0 + 1, :]
    scale = mod_ref[0, row0 + 1:row0 + 2, :]
    gate = mod_ref[0, row0 + 2:row0 + 3, :]
    h_ref[...] = (x * (1.0 + scale) + shift).astype(BF16)
    f = wo_ref.shape[0]
    fc = f // n_chunks
    for j in range(n_chunks):
        h = h_ref[...]
        a = _dot(h, wi_ref[:, j * fc:(j + 1) * fc])
        u = _dot(h, wi_ref[:, f + j * fc:f + (j + 1) * fc])
        act_ref[:, j * fc:(j + 1) * fc] = (a * _sigmoid(a) * u).astype(BF16)
    y = _dot(act_ref[...], wo_ref[...])
    z = DN_ALPHA * x + FFN_RES * (1.0 + gate) * y
    return _layer_norm(z, lng_ref[...], lnb_ref[...])


def _ffn_kernel(x_ref, mod_ref, wi_ref, wo_ref, lng_ref, lnb_ref, o_ref, h_ref, act_ref, *, row0, n_chunks):
    o_ref[...] = _swiglu_post_norm(x_ref[...], mod_ref, row0, wi_ref, wo_ref, lng_ref, lnb_ref, h_ref, act_ref, n_chunks)


def _mixer_out_ffn_kernel(y_ref, x_ref, mod_ref, wm_ref, lng1_ref, lnb1_ref, wi_ref, wo_ref, lng2_ref, lnb2_ref,
                          o_ref, h_ref, act_ref, *, n_chunks):
    gate = mod_ref[0, 5:6, :]
    z = DN_ALPHA * x_ref[...] + (1.0 + gate) * _dot(y_ref[...], wm_ref[...])
    x1 = _layer_norm(z, lng1_ref[...], lnb1_ref[...])
    o_ref[...] = _swiglu_post_norm(x1, mod_ref, 6, wi_ref, wo_ref, lng2_ref, lnb2_ref, h_ref, act_ref, n_chunks)


def _ffn(x, mod, row0, w_in, w_out, ln_g, ln_b, seq):
    n_tok, d = x.shape
    f = w_out.shape[0]
    tm = min(TOKEN_TILE, seq)
    tiles_per_seq = seq // tm
    return pl.pallas_call(
        functools.partial(_ffn_kernel, row0=row0, n_chunks=f // MXU_WIDTH),
        grid=(n_tok // tm,),
        in_specs=[
            pl.BlockSpec((tm, d), lambda i: (i, 0)),
            pl.BlockSpec((1, 9, d), lambda i: (i // tiles_per_seq, 0, 0)),
            _resident(w_in.shape),
            _resident(w_out.shape),
            _resident((1, d)),
            _resident((1, d)),
        ],
        out_specs=pl.BlockSpec((tm, d), lambda i: (i, 0)),
        out_shape=jax.ShapeDtypeStruct((n_tok, d), F32),
        scratch_shapes=[pltpu.VMEM((tm, d), BF16), pltpu.VMEM((tm, f), BF16)],
        compiler_params=_params("parallel"),
        name="ffn",
    )(x, mod, w_in, w_out, ln_g, ln_b)


def _mixer_out_ffn(y, x, mod, w_mix, ln1, w_in, w_out, ln2, seq):
    n_tok, d = x.shape
    k = y.shape[1]
    f = w_out.shape[0]
    tm = min(TOKEN_TILE, seq)
    tiles_per_seq = seq // tm
    return pl.pallas_call(
        functools.partial(_mixer_out_ffn_kernel, n_chunks=f // MXU_WIDTH),
        grid=(n_tok // tm,),
        in_specs=[
            pl.BlockSpec((tm, k), lambda i: (i, 0)),
            pl.BlockSpec((tm, d), lambda i: (i, 0)),
            pl.BlockSpec((1, 9, d), lambda i: (i // tiles_per_seq, 0, 0)),
            _resident(w_mix.shape), _resident((1, d)), _resident((1, d)),
            _resident(w_in.shape), _resident(w_out.shape), _resident((1, d)), _resident((1, d)),
        ],
        out_specs=pl.BlockSpec((tm, d), lambda i: (i, 0)),
        out_shape=jax.ShapeDtypeStruct((n_tok, d), F32),
        scratch_shapes=[pltpu.VMEM((tm, d), BF16), pltpu.VMEM((tm, f), BF16)],
        compiler_params=_params("parallel"),
        name="mixer_out_ffn",
    )(y, x, mod, w_mix, *ln1, w_in, w_out, *ln2)


def _proj_kernel(x_ref, mod_ref, w_ref, o_ref, *, row0, tn):
    x = x_ref[...]
    shift = mod_ref[0, row0:row0 + 1, :]
    scale = mod_ref[0, row0 + 1:row0 + 2, :]
    h = (x * (1.0 + scale) + shift).astype(BF16)
    for j in range(o_ref.shape[1] // tn):
        o_ref[:, j * tn:(j + 1) * tn] = _dot(h, w_ref[:, j * tn:(j + 1) * tn]).astype(o_ref.dtype)


def _proj(x, mod, row0, w, seq, tn):
    n_tok, d = x.shape
    n = w.shape[1]
    tm = min(TOKEN_TILE, seq)
    tiles_per_seq = seq // tm
    return pl.pallas_call(
        functools.partial(_proj_kernel, row0=row0, tn=tn),
        grid=(n_tok // tm,),
        in_specs=[
            pl.BlockSpec((tm, d), lambda i: (i, 0)),
            pl.BlockSpec((1, 9, d), lambda i: (i // tiles_per_seq, 0, 0)),
            _resident((d, n)),
        ],
        out_specs=pl.BlockSpec((tm, n), lambda i: (i, 0)),
        out_shape=jax.ShapeDtypeStruct((n_tok, n), BF16),
        compiler_params=_params("parallel"),
        name="mixer_in_proj",
    )(x, mod, w)


def _nsa_proj_kernel(x_ref, mod_ref, w_ref, feat_ref, q_ref, kcv_ref, keys_ref, vals_ref, gates_ref, *, row0, q_scale):
    x = x_ref[...]
    shift = mod_ref[0, row0:row0 + 1, :]
    scale = mod_ref[0, row0 + 1:row0 + 2, :]
    h = (x * (1.0 + scale) + shift).astype(BF16)
    half = keys_ref.shape[1] // 2
    lane_tile = kcv_ref.shape[2]
    col = 0
    for ref in (q_ref, kcv_ref, keys_ref, vals_ref, gates_ref):
        n = ref.shape[1] if ref is not kcv_ref else kcv_ref.shape[0] * lane_tile
        step = min(n, MXU_WIDTH)
        for j in range(0, n, step):
            y = _dot(h, w_ref[:, col + j:col + j + step])
            if ref is q_ref:
                y = y * q_scale
            elif ref is keys_ref:
                f = feat_ref[:, 0:KEY_WIDTH] if j < half else feat_ref[:, KEY_WIDTH:2 * KEY_WIDTH]
                y = y + jnp.concatenate([f] * (step // KEY_WIDTH), 1).astype(F32)
            elif ref is gates_ref:
                y = _sigmoid(y)
            if ref is kcv_ref:
                for c in range(step // lane_tile):
                    kcv_ref[(j + c * lane_tile) // lane_tile] = y[:, c * lane_tile:(c + 1) * lane_tile]
            else:
                ref[:, j:j + step] = y.astype(ref.dtype)
        col += n


def _nsa_proj(x, mod, row0, w, feat, seq, d, kv, q_scale):
    n_tok = x.shape[0]
    groups = NSA_GROUPS
    tm = min(TOKEN_TILE, seq)
    tiles_per_seq = seq // tm
    widths = (d, 2 * kv, 2 * groups * KEY_WIDTH, 2 * kv, w.shape[1] - d - 4 * kv - 2 * groups * KEY_WIDTH)
    dtypes = (BF16, F32, BF16, BF16, F32)
    lane = 128
    return pl.pallas_call(
        functools.partial(_nsa_proj_kernel, row0=row0, q_scale=q_scale),
        grid=(n_tok // tm,),
        in_specs=[
            pl.BlockSpec((tm, d), lambda i: (i, 0)),
            pl.BlockSpec((1, 9, d), lambda i: (i // tiles_per_seq, 0, 0)),
            _resident(w.shape),
            pl.BlockSpec((tm, 2 * KEY_WIDTH), lambda i: (i % tiles_per_seq, 0)),
        ],
        out_specs=[pl.BlockSpec((tm, n), lambda i: (i, 0)) if k != 1 else
                   pl.BlockSpec((n // lane, tm, lane), lambda i: (0, i, 0)) for k, n in enumerate(widths)],
        out_shape=[jax.ShapeDtypeStruct((n_tok, n) if k != 1 else (n // lane, n_tok, lane), dt)
                   for k, (n, dt) in enumerate(zip(widths, dtypes))],
        compiler_params=_params("parallel"),
        name="nsa_in_proj",
    )(x, mod, w, feat)


def _ret_kernel(q_ref, k_ref, v_ref, g_ref, o_ref, state_ref, *, chunk, n_chunks, k_scale):
    heads = state_ref.shape[0]
    dk, dv = state_ref.shape[1], state_ref.shape[2]
    row = lax.broadcasted_iota(jnp.int32, (chunk, chunk), 0)
    col = lax.broadcasted_iota(jnp.int32, (chunk, chunk), 1)
    diff = (row - col).astype(F32)
    pos = lax.broadcasted_iota(jnp.int32, (chunk, 1), 0).astype(F32)
    decays = []
    for hh in range(heads):
        head = (pl.program_id(1) * heads + hh).astype(F32)
        log_g = jnp.log(1.0 - jnp.exp2(-5.0 - (jnp.zeros((1, 1), F32) + head)))
        decays.append((jnp.where(diff >= 0, jnp.exp(log_g * jnp.maximum(diff, 0.0)), 0.0),
                       jnp.exp(log_g * (pos + 1.0)),
                       jnp.exp(log_g * (chunk - 1.0 - pos)),
                       jnp.exp(log_g * float(chunk))))
    state_ref[...] = jnp.zeros_like(state_ref)

    def step(c, carry):
        rows = pl.ds(pl.multiple_of(c * chunk, chunk), chunk)
        for hh in range(heads):
            decay_intra, decay_q, decay_k, decay_state = decays[hh]
            q = q_ref[0, rows, hh * dk:(hh + 1) * dk]
            k = k_ref[0, rows, hh * dk:(hh + 1) * dk] * k_scale
            v = v_ref[0, rows, hh * dv:(hh + 1) * dv]
            scores = lax.dot_general(q, k, (((1,), (1,)), ((), ())), preferred_element_type=F32) * decay_intra
            state = state_ref[hh]
            o = _dot(scores.astype(BF16), v) + _dot((q.astype(F32) * decay_q).astype(BF16), state.astype(BF16))
            k_dec_t = (k.astype(F32) * decay_k).T.astype(BF16)
            state_ref[hh] = decay_state * state + _dot(k_dec_t, v)
            mu = jnp.mean(o, -1, keepdims=True)
            oc = o - mu
            var = jnp.mean(oc * oc, -1, keepdims=True)
            o = oc * lax.rsqrt(var + GN_EPS)
            g = g_ref[0, rows, hh * dv:(hh + 1) * dv].astype(F32)
            o_ref[0, rows, hh * dv:(hh + 1) * dv] = (o * (g * _sigmoid(g))).astype(o_ref.dtype)
        return carry

    lax.fori_loop(0, n_chunks, step, 0)


def _retention_core(proj, batch, seq, dk, dv):
    heads = RET_HEADS
    hps = RET_HEADS_PER_STEP
    chunk = min(RET_CHUNK, seq)
    k_blk0 = heads * dk // (hps * dk)
    v_blk0 = 2 * heads * dk // (hps * dv)
    g_blk0 = (2 * heads * dk + heads * dv) // (hps * dv)
    return pl.pallas_call(
        functools.partial(_ret_kernel, chunk=chunk, n_chunks=seq // chunk, k_scale=dk ** -0.5),
        grid=(batch, heads // hps),
        in_specs=[
            pl.BlockSpec((1, seq, hps * dk), lambda b, h: (b, 0, h)),
            pl.BlockSpec((1, seq, hps * dk), lambda b, h: (b, 0, k_blk0 + h)),
            pl.BlockSpec((1, seq, hps * dv), lambda b, h: (b, 0, v_blk0 + h)),
            pl.BlockSpec((1, seq, hps * dv), lambda b, h: (b, 0, g_blk0 + h)),
        ],
        out_specs=pl.BlockSpec((1, seq, hps * dv), lambda b, h: (b, 0, h)),
        out_shape=jax.ShapeDtypeStruct((batch, seq, heads * dv), BF16),
        scratch_shapes=[pltpu.VMEM((hps, dk, dv), F32)],
        compiler_params=_params("parallel", "parallel"),
        name="retention_core",
    )(proj, proj, proj, proj)


def _gelu_tanh(x):
    return 0.5 * x * (1.0 + jnp.tanh(0.7978845608028654 * (x + 0.044715 * (x * x * x))))


def _compress_kernel(x_ref, pek_ref, pev_ref, k1_ref, k2_ref, v1_ref, v2_ref, ok_ref, ov_ref, y_ref):
    groups, n_rows, half = y_ref.shape
    dh = half // CMP_STRIDE
    kv = groups * dh
    per_tile = x_ref.shape[2] // dh

    def mlp(lane0, pe_ref, w1_ref, w2_ref, out_ref):
        for r in range(CMP_STRIDE):
            for c in range(groups // per_tile):
                piece = x_ref[lane0 // x_ref.shape[2] + c, pl.ds(r, n_rows, stride=CMP_STRIDE), :]
                for k in range(per_tile):
                    y_ref[c * per_tile + k, :, r * dh:(r + 1) * dh] = piece[:, k * dh:(k + 1) * dh].astype(BF16)
        bias = _dot(pe_ref[...].astype(BF16), w1_ref[...])
        for g in range(groups):
            y = y_ref[g]
            first = _dot(y, w1_ref[0:half, :])
            second = _dot(y, w1_ref[half:2 * half, :])
            hidden = first + pltpu.roll(second, n_rows - 1, 0) + bias
            out_ref[0, g] = _dot(_gelu_tanh(hidden).astype(BF16), w2_ref[...]).astype(out_ref.dtype)

    mlp(0, pek_ref, k1_ref, k2_ref, ok_ref)
    mlp(kv, pev_ref, v1_ref, v2_ref, ov_ref)


def _compress(kcv, pe_k, pe_v, k1, k2, v1, v2, groups, batch):
    lane_tiles, n_tok, lane = kcv.shape
    seq = n_tok // batch
    rows = seq // CMP_STRIDE
    hidden = k1.shape[1]
    dh = k2.shape[1]
    out_blk = pl.BlockSpec((1, groups, rows, dh), lambda b: (b, 0, 0, 0))
    out_sds = jax.ShapeDtypeStruct((batch, groups, rows, dh), BF16)
    return pl.pallas_call(
        _compress_kernel,
        grid=(batch,),
        in_specs=[pl.BlockSpec((lane_tiles, seq, lane), lambda b: (0, b, 0)),
                  _resident(pe_k.shape), _resident(pe_v.shape),
                  _resident(k1.shape), _resident(k2.shape), _resident(v1.shape), _resident(v2.shape)],
        out_specs=[out_blk, out_blk],
        out_shape=[out_sds, out_sds],
        scratch_shapes=[pltpu.VMEM((groups, rows, CMP_STRIDE * dh), BF16)],
        compiler_params=_params("parallel"),
        name="nsa_compress",
    )(kcv, pe_k, pe_v, k1, k2, v1, v2)


def _nsa_attn_kernel(q_ref, kc_ref, vct_ref, keys_ref, vals_ref, gates_ref, o_ref,
                     qa_ref, vst_ref, vwt_ref, s_sel_ref, s_win_ref, acc_ref, l8_ref, *, n_sel, top):
    step = pl.program_id(1)
    groups, dh = qa_ref.shape[0], vct_ref.shape[2]
    kv = groups * dh
    all_groups = range(groups)

    @pl.when(step == 0)
    def _():
        def sel_tile(i, carry):
            rows = pl.ds(pl.multiple_of(i * SEL_KEY_TILE, SEL_KEY_TILE), SEL_KEY_TILE)
            vt = vals_ref[0, rows, 0:kv].astype(F32).T
            for g in all_groups:
                vst_ref[g, i] = vt[g * dh:(g + 1) * dh, :].astype(BF16)
            return carry

        def win_tile(i, carry):
            rows = pl.ds(pl.multiple_of(i * WIN_KEY_TILE, WIN_KEY_TILE), WIN_KEY_TILE)
            vt = vals_ref[0, rows, kv:2 * kv].astype(F32).T
            for g in all_groups:
                vwt_ref[g, i] = vt[g * dh:(g + 1) * dh, :].astype(BF16)
            return carry

        lax.fori_loop(0, vst_ref.shape[1], sel_tile, 0)
        lax.fori_loop(0, vwt_ref.shape[1], win_tile, 0)

    def one_q_block(sub, carry):
        q_rows = pl.ds(pl.multiple_of(sub * Q_BLOCK, Q_BLOCK), Q_BLOCK)
        _nsa_q_block(step * Q_BLOCKS_PER_STEP + sub, q_rows, q_ref, kc_ref, vct_ref, keys_ref, gates_ref, o_ref,
                     qa_ref, vst_ref, vwt_ref, s_sel_ref, s_win_ref, acc_ref, l8_ref, n_sel, top)
        return carry

    lax.fori_loop(0, Q_BLOCKS_PER_STEP, one_q_block, 0)


def _nsa_q_block(qb, q_rows, q_ref, kc_ref, vct_ref, keys_ref, gates_ref, o_ref,
                 qa_ref, vst_ref, vwt_ref, s_sel_ref, s_win_ref, acc_ref, l8_ref, n_sel, top):
    q0 = qb * Q_BLOCK
    groups, dh, width = qa_ref.shape[0], vct_ref.shape[2], qa_ref.shape[2]
    sel_row0 = KEY_WIDTH - n_sel
    all_groups = range(groups)
    pair = 2 * dh
    assert pair == Q_BLOCK == KEY_WIDTH

    lane = lax.broadcasted_iota(jnp.int32, (1, width), 1)
    t_row = q0 + lane % Q_BLOCK
    piece_row = lax.broadcasted_iota(jnp.int32, (POS_ROWS, width), 0)

    for g in all_groups:
        head = (g * NSA_HPG + lane // Q_BLOCK).astype(F32)
        slope = jnp.exp2(-8.0 * (head + 1.0) / NSA_HEADS) * LOG2_E
        hi = slope.astype(BF16).astype(F32)
        rest = slope - hi
        mid = rest.astype(BF16).astype(F32)
        lo = (rest - mid).astype(BF16).astype(F32)
        feat = jnp.zeros((POS_ROWS, width), F32)
        for idx, piece in enumerate((SEL_BLOCK * hi, SEL_BLOCK * mid, SEL_BLOCK * lo, hi, mid, lo)):
            feat = jnp.where(piece_row == idx, piece, feat)
        for p in range(NSA_HPG // 2):
            lanes0 = (g * NSA_HPG + 2 * p) * dh
            two_heads = q_ref[0, q_rows, lanes0:lanes0 + pair].astype(F32).T
            qa_ref[g, 0:dh, (2 * p) * Q_BLOCK:(2 * p + 1) * Q_BLOCK] = two_heads[0:dh].astype(BF16)
            qa_ref[g, 0:dh, (2 * p + 1) * Q_BLOCK:(2 * p + 2) * Q_BLOCK] = two_heads[dh:pair].astype(BF16)
        qa_ref[g, dh:dh + POS_ROWS, :] = feat.astype(BF16)
        qa_ref[g, dh + POS_ROWS:KEY_WIDTH, :] = jnp.zeros((KEY_WIDTH - dh - POS_ROWS, width), BF16)

    n_cmp_pad = kc_ref.shape[2]
    blk = lax.broadcasted_iota(jnp.int32, (n_cmp_pad, width), 0)
    mask_c = blk * CMP_STRIDE + (CMP_BLOCK - 1) <= t_row
    any_visible = t_row >= CMP_BLOCK - 1
    scores_c = [_dot(kc_ref[0, g], qa_ref[g]) for g in all_groups]
    p_cmp = []
    for g in all_groups:
        s = jnp.where(mask_c, scores_c[g], NEG_INF)
        e = jnp.exp2(s - jnp.max(s, 0, keepdims=True))
        p_cmp.append(e * jnp.where(any_visible, 1.0 / jnp.sum(e, 0, keepdims=True), 0.0))
    o_cmp = [_dot(vct_ref[0, g], p_cmp[g].astype(BF16)) for g in all_groups]

    sel_i = lax.broadcasted_iota(jnp.int32, (n_sel, n_cmp_pad), 0) * SEL_BLOCK
    cmp_i = lax.broadcasted_iota(jnp.int32, (n_sel, n_cmp_pad), 1) * CMP_STRIDE
    overlap = jnp.maximum(jnp.minimum(cmp_i + CMP_BLOCK, sel_i + SEL_BLOCK) - jnp.maximum(cmp_i, sel_i), 0)
    overlap = (overlap.astype(F32) * (1.0 / CMP_BLOCK)).astype(BF16)
    j_idx = lax.broadcasted_iota(jnp.int32, (n_sel, Q_BLOCK), 0)
    t_sel = q0 + lax.broadcasted_iota(jnp.int32, (n_sel, Q_BLOCK), 1)
    cur = t_sel // SEL_BLOCK
    forced = (j_idx == 0) | (j_idx == cur) | (j_idx == cur - 1)
    valid = j_idx * SEL_BLOCK <= t_sel
    pieces = []
    for g in all_groups:
        p_sum = p_cmp[g][:, 0:Q_BLOCK]
        for h in range(1, NSA_HPG):
            p_sum = p_sum + p_cmp[g][:, h * Q_BLOCK:(h + 1) * Q_BLOCK]
        p_hi = p_sum.astype(BF16)
        rest = p_sum - p_hi.astype(F32)
        p_mid = rest.astype(BF16)
        pieces.append((p_hi, p_mid, (rest - p_mid.astype(F32)).astype(BF16)))
    imps = [_dot(overlap, hi) + _dot(overlap, mid) + _dot(overlap, lo) for hi, mid, lo in pieces]
    for g in all_groups:
        score = jnp.where(valid, jnp.where(forced, FORCED_SCORE, imps[g]), INVALID_SCORE)
        tiles = [score[r:r + 8, :] for r in range(0, n_sel, 8)]
        ranks = [jnp.zeros((8, Q_BLOCK), F32) for _ in tiles]
        for jp in range(n_sel):
            other = jnp.broadcast_to(score[jp:jp + 1, :], (8, Q_BLOCK))
            for v, tile_scores in enumerate(tiles):
                if jp < 8 * v:
                    ahead = other >= tile_scores
                elif jp >= 8 * v + 7:
                    ahead = other > tile_scores
                else:
                    ahead = (other > tile_scores) | ((other == tile_scores) & (j_idx[0:8, :] > jp - 8 * v))
                ranks[v] = ranks[v] + jnp.where(ahead, 1.0, 0.0)
        rank = jnp.concatenate(ranks, 0)
        block_bias = jnp.where(rank < top, 0.0, NEG_INF)
        qa_ref[g, sel_row0:KEY_WIDTH, :] = jnp.concatenate([block_bias] * NSA_HPG, 1).astype(BF16)

    def row_max(sc):
        return jnp.max(sc.reshape(sc.shape[0] // 8, 8, width), 0)

    def for_tiles(n, body, init):
        carry, start, width_ = init, 0, TILE_UNROLL
        while width_ >= 1:
            def several(j, c, start=start, width_=width_):
                for u in range(width_):
                    c = body(start + width_ * j + u, c)
                return c
            trips = (n - start) // width_
            carry = lax.fori_loop(0, trips, several, carry)
            start = start + trips * width_
            width_ //= 2
        return carry

    def softmax_values(s_ref, vt_ref, first_tile, n_tiles, m8):
        m = [jnp.max(m8[g], 0, keepdims=True) for g in all_groups]
        acc_ref[...] = jnp.zeros_like(acc_ref)
        l8_ref[...] = jnp.zeros_like(l8_ref)

        def step(i, carry):
            pes = [jnp.exp2(s_ref[g, i] - m[g]) for g in all_groups]
            for g in all_groups:
                l8_ref[g] += jnp.sum(pes[g].reshape(pes[g].shape[0] // 8, 8, width), 0)
                acc_ref[g] += _dot(vt_ref[g, first_tile + i], pes[g].astype(BF16))
            return carry

        for_tiles(n_tiles, step, 0)
        return [(acc_ref[g], 1.0 / jnp.sum(l8_ref[g], 0, keepdims=True)) for g in all_groups]

    m8_init = tuple(jnp.full((8, width), NEG_INF, F32) for _ in all_groups)

    tile = SEL_KEY_TILE

    def sel_scores(g, kt):
        rows = pl.ds(pl.multiple_of(kt * tile, tile), tile)
        return _dot(keys_ref[0, rows, g * KEY_WIDTH:(g + 1) * KEY_WIDTH], qa_ref[g])

    def sel_pass(kt, m8):
        out = []
        for g in all_groups:
            sc = sel_scores(g, kt)
            s_sel_ref[g, kt] = sc
            out.append(jnp.maximum(m8[g], row_max(sc)))
        return tuple(out)

    last = (q0 + Q_BLOCK - 1) // tile
    m8 = list(for_tiles(last, sel_pass, m8_init))
    visible = last * tile + lax.broadcasted_iota(jnp.int32, (tile, width), 0) <= t_row
    last_scores = [sel_scores(g, last) for g in all_groups]
    for g in all_groups:
        sc = jnp.where(visible, last_scores[g], NEG_INF)
        s_sel_ref[g, last] = sc
        m8[g] = jnp.maximum(m8[g], row_max(sc))
    o_sel = softmax_values(s_sel_ref, vst_ref, 0, last + 1, m8)

    tile = WIN_KEY_TILE
    first = jnp.maximum(q0 - WINDOW, 0) // tile
    n_win = (q0 + Q_BLOCK - 1) // tile - first + 1
    key_row = lax.broadcasted_iota(jnp.int32, (tile, width), 0)

    def win_pass(i, m8):
        kt = first + i
        rows = pl.ds(pl.multiple_of(kt * tile, tile), tile)
        dist = (t_row - kt * tile) - key_row
        in_window = (dist >= 0) & (dist < WINDOW)
        out = []
        for g in all_groups:
            k_tile = keys_ref[0, rows, (groups + g) * KEY_WIDTH:(groups + g + 1) * KEY_WIDTH]
            sc = jnp.where(in_window, _dot(k_tile, qa_ref[g]), NEG_INF)
            s_win_ref[g, i] = sc
            out.append(jnp.maximum(m8[g], row_max(sc)))
        return tuple(out)

    m8 = for_tiles(n_win, win_pass, m8_init)
    o_win = softmax_values(s_win_ref, vwt_ref, first, n_win, m8)

    gates_t = gates_ref[0, q_rows, :].T
    for g in all_groups:
        per_head = []
        for h in range(NSA_HPG):
            lanes = slice(h * Q_BLOCK, (h + 1) * Q_BLOCK)
            row = (g * NSA_HPG + h) * 3
            (acc_sel, inv_sel), (acc_win, inv_win) = o_sel[g], o_win[g]
            per_head.append(gates_t[row:row + 1, :] * o_cmp[g][:, lanes]
                            + (gates_t[row + 1:row + 2, :] * inv_sel[:, lanes]) * acc_sel[:, lanes]
                            + (gates_t[row + 2:row + 3, :] * inv_win[:, lanes]) * acc_win[:, lanes])
        for p in range(NSA_HPG // 2):
            lanes0 = (g * NSA_HPG + 2 * p) * dh
            two_heads = jnp.concatenate(per_head[2 * p:2 * p + 2], 0)
            o_ref[0, q_rows, lanes0:lanes0 + pair] = two_heads.T.astype(o_ref.dtype)


def _nsa_attention(q, kc, vct, keys, vals, gates, groups):
    batch, seq, d = q.shape
    dh = vct.shape[2]
    width = NSA_HPG * Q_BLOCK
    n_qb = seq // Q_BLOCK
    n_sel = seq // SEL_BLOCK
    top = min(SEL_TOP, n_sel)
    assert dh + POS_ROWS + n_sel <= KEY_WIDTH
    per_b = lambda shape: pl.BlockSpec((1,) + shape, lambda b, i: (b,) + (0,) * len(shape))
    per_q = lambda lanes: pl.BlockSpec((1, Q_BLOCKS_PER_STEP * Q_BLOCK, lanes), lambda b, i: (b, i, 0))
    n_win_tiles = min(WINDOW + Q_BLOCK, seq) // WIN_KEY_TILE
    return pl.pallas_call(
        functools.partial(_nsa_attn_kernel, n_sel=n_sel, top=top),
        grid=(batch, n_qb // Q_BLOCKS_PER_STEP),
        in_specs=[
            per_q(d),
            per_b(kc.shape[1:]), per_b(vct.shape[1:]),
            per_b(keys.shape[1:]), per_b(vals.shape[1:]),
            per_q(gates.shape[2]),
        ],
        out_specs=per_q(d),
        out_shape=jax.ShapeDtypeStruct((batch, seq, d), BF16),
        scratch_shapes=[pltpu.VMEM((groups, KEY_WIDTH, width), BF16),
                        pltpu.VMEM((groups, seq // SEL_KEY_TILE, dh, SEL_KEY_TILE), BF16),
                        pltpu.VMEM((groups, seq // WIN_KEY_TILE, dh, WIN_KEY_TILE), BF16),
                        pltpu.VMEM((groups, seq // SEL_KEY_TILE, SEL_KEY_TILE, width), F32),
                        pltpu.VMEM((groups, n_win_tiles, WIN_KEY_TILE, width), F32),
                        pltpu.VMEM((groups, dh, width), F32),
                        pltpu.VMEM((groups, 8, width), F32)],
        compiler_params=_params("parallel", "arbitrary"),
        name="nsa_attention",
    )(q, kc, vct, keys, vals, gates)


def _key_features(pos, dh, n_sel):
    block, offset = pos // SEL_BLOCK, pos % SEL_BLOCK
    feat = np.zeros((pos.shape[0], KEY_WIDTH), np.float32)
    feat[:, dh:dh + 3] = block[:, None]
    feat[:, dh + 3:dh + 6] = offset[:, None]
    if n_sel:
        feat[:, KEY_WIDTH - n_sel:] = block[:, None] == np.arange(n_sel)[None, :]
    return feat


def _nsa_weights(w_in, d, kv, groups):
    dh = kv // groups
    part = lambda j: w_in[:, d + j * kv:d + (j + 1) * kv]
    pad_keys = lambda w: jnp.pad(w.reshape(-1, groups, dh), ((0, 0), (0, 0), (0, KEY_WIDTH - dh))).reshape(-1, groups * KEY_WIDTH)
    gate_w = w_in[:, d + 6 * kv:]
    gate_w = jnp.pad(gate_w, ((0, 0), (0, -gate_w.shape[1] % 128)))
    cols = [w_in[:, :d], part(0), part(1), pad_keys(part(2)), pad_keys(part(4)), part(3), part(5), gate_w]
    return jnp.concatenate(cols, 1).astype(BF16)


def _nsa_mixer(x, mod, w_in, pe_k, pe_v, k1, k2, v1, v2, batch, seq):
    d = x.shape[1]
    groups = NSA_GROUPS
    dh = d // NSA_HEADS
    kv = groups * dh
    n_sel = seq // SEL_BLOCK
    positions = np.arange(seq)
    feat = jnp.asarray(np.concatenate([_key_features(positions, dh, n_sel), _key_features(positions, dh, 0)], 1), BF16)
    q, kcv, keys, vals, gates = _nsa_proj(x, mod, 3, _nsa_weights(w_in, d, kv, groups), feat, seq, d, kv,
                                          dh ** -0.5 * LOG2_E)
    kc, vc = _compress(kcv, pe_k, pe_v, k1, k2, v1, v2, groups, batch)
    cmp_end = np.arange(seq // CMP_STRIDE) * CMP_STRIDE + CMP_BLOCK - 1
    cmp_feat = jnp.asarray(_key_features(cmp_end, dh, 0)[:, dh:], BF16)
    kc = jnp.concatenate([kc, jnp.broadcast_to(cmp_feat, kc.shape[:2] + cmp_feat.shape)], -1)
    o = _nsa_attention(q.reshape(batch, seq, d), kc, vc.transpose(0, 1, 3, 2), keys.reshape(batch, seq, -1),
                       vals.reshape(batch, seq, -1), gates.reshape(batch, seq, -1), groups)
    return o.reshape(batch * seq, d)


def kernel(x, c, ada_w, ada_b, ln_g, ln_b, ffn_w_in, ffn_w_out, ret_w_in, ret_w_out, nsa_w_in, nsa_w_out,
           nsa_pe_k, nsa_pe_v, nsa_ck_w1, nsa_ck_w2, nsa_cv_w1, nsa_cv_w2):
    batch, seq, d = x.shape
    depth = ada_w.shape[0]
    assert depth == DEPTH
    mod_all = _ada_mod(c, ada_w, ada_b).reshape(depth, batch, 9, d)
    xf = x.reshape(batch * seq, d)
    dk = d // RET_HEADS
    dv = 2 * dk
    for i in range(depth):
        mod = mod_all[i]
        ln = lambda k: (ln_g[i, k].reshape(1, d), ln_b[i, k].reshape(1, d))
        xf = _ffn(xf, mod, 0, ffn_w_in[i, 0].astype(BF16), ffn_w_out[i, 0].astype(BF16), *ln(0), seq)
        j = i // N_MIXERS
        if i % N_MIXERS == 0:
            proj = _proj(xf, mod, 3, ret_w_in[j].astype(BF16), seq, 2 * MXU_WIDTH)
            y = _retention_core(proj.reshape(batch, seq, -1), batch, seq, dk, dv).reshape(batch * seq, -1)
            w_out = ret_w_out[j]
        else:
            y = _nsa_mixer(xf, mod, nsa_w_in[j], nsa_pe_k[j].reshape(1, -1), nsa_pe_v[j].reshape(1, -1),
                           nsa_ck_w1[j].astype(BF16), nsa_ck_w2[j].astype(BF16),
                           nsa_cv_w1[j].astype(BF16), nsa_cv_w2[j].astype(BF16), batch, seq)
            w_out = nsa_w_out[j]
        xf = _mixer_out_ffn(y, xf, mod, w_out.astype(BF16), ln(1),
                            ffn_w_in[i, 1].astype(BF16), ffn_w_out[i, 1].astype(BF16), ln(2), seq)
    return xf.reshape(batch, seq, d)
```

```python
import functools

import numpy as np
import jax
import jax.numpy as jnp
from jax import lax
from jax.experimental import pallas as pl
from jax.experimental.pallas import tpu as pltpu

F32 = jnp.float32
BF16 = jnp.bfloat16

DEPTH = 4
N_MIXERS = 2
RET_HEADS = 4
NSA_HEADS = 16
NSA_GROUPS = 4
NSA_HPG = NSA_HEADS // NSA_GROUPS
CMP_BLOCK = 32
CMP_STRIDE = 16
SEL_BLOCK = 64
SEL_TOP = 16
WINDOW = 512
Q_BLOCK = 128
FORCED_SCORE = 1e4
INVALID_SCORE = -1.0
FFN_RES = 0.5
DN_ALPHA = (2 * DEPTH) ** 0.25
LN_EPS = 1e-5
GN_EPS = 1e-6
NEG_INF = -1e30

V7X_VMEM_LIMIT_BYTES = 56 * 1024 * 1024
MXU_WIDTH = 256
TOKEN_TILE = 1024
RET_CHUNK = 256
RET_HEADS_PER_STEP = 2
SEL_KEY_TILE = 256
WIN_KEY_TILE = 128
Q_BLOCKS_PER_STEP = 4
TILE_UNROLL = 4
KEY_WIDTH = 128
POS_ROWS = 16
LOG2_E = 1.4426950408889634


def _dot(a, b):
    return jnp.dot(a, b, preferred_element_type=F32)


def _sigmoid(x):
    return 1.0 / (1.0 + jnp.exp(-x))


def _layer_norm(z, g, b):
    mu = jnp.mean(z, -1, keepdims=True)
    zc = z - mu
    var = jnp.mean(zc * zc, -1, keepdims=True)
    return zc * lax.rsqrt(var + LN_EPS) * g + b


def _params(*sem):
    return pltpu.CompilerParams(dimension_semantics=sem, vmem_limit_bytes=V7X_VMEM_LIMIT_BYTES)


def _resident(shape):
    return pl.BlockSpec(shape, lambda *_: (0,) * len(shape), pipeline_mode=pl.Buffered(1))


def _resident_slice(stacked, index):
    rest = stacked.shape[len(index):]
    return pl.BlockSpec((None,) * len(index) + rest, lambda *_: tuple(index) + (0,) * len(rest),
                        pipeline_mode=pl.Buffered(1))


def _ada_kernel(c_ref, w_ref, b_ref, o_ref):
    c = c_ref[...]
    c_act = (c * _sigmoid(c)).astype(BF16)
    o_ref[0] = _dot(c_act, w_ref[0].astype(BF16)) + b_ref[0]


def _ada_mod(c, ada_w, ada_b):
    depth, d, n = ada_w.shape
    b = c.shape[0]
    tn = n // 8
    return pl.pallas_call(
        _ada_kernel,
        grid=(depth, n // tn),
        in_specs=[
            pl.BlockSpec((b, d), lambda l, j: (0, 0)),
            pl.BlockSpec((1, d, tn), lambda l, j: (l, 0, j)),
            pl.BlockSpec((1, 1, tn), lambda l, j: (l, 0, j)),
        ],
        out_specs=pl.BlockSpec((1, b, tn), lambda l, j: (l, 0, j)),
        out_shape=jax.ShapeDtypeStruct((depth, b, n), F32),
        compiler_params=_params("parallel", "parallel"),
        name="ada_mod",
    )(c, ada_w, ada_b.reshape(depth, 1, n))


def _swiglu_post_norm(x, mod_ref, row0, wi_ref, wo_ref, lng_ref, lnb_ref, h_ref, act_ref, n_chunks):
    shift = mod_ref[0, row0:row0 + 1, :]
    scale = mod_ref[0, row0 + 1:row0 + 2, :]
    gate = mod_ref[0, row0 + 2:row0 + 3, :]
    h_ref[...] = (x * (1.0 + scale) + shift).astype(BF16)
    f = wo_ref.shape[0]
    fc = f // n_chunks
    for j in range(n_chunks):
        h = h_ref[...]
        a = _dot(h, wi_ref[:, j * fc:(j + 1) * fc])
        u = _dot(h, wi_ref[:, f + j * fc:f + (j + 1) * fc])
        act_ref[:, j * fc:(j + 1) * fc] = (a * _sigmoid(a) * u).astype(BF16)
    y = _dot(act_ref[...], wo_ref[...])
    z = DN_ALPHA * x + FFN_RES * (1.0 + gate) * y
    return _layer_norm(z, lng_ref[...], lnb_ref[...])


def _ffn_kernel(x_ref, mod_ref, wi_ref, wo_ref, lng_ref, lnb_ref, o_ref, h_ref, act_ref, *, row0, n_chunks):
    o_ref[...] = _swiglu_post_norm(x_ref[...], mod_ref, row0, wi_ref, wo_ref, lng_ref, lnb_ref, h_ref, act_ref, n_chunks)


def _mixer_out_ffn_kernel(y_ref, x_ref, mod_ref, wm_ref, lng1_ref, lnb1_ref, wi_ref, wo_ref, lng2_ref, lnb2_ref,
                          o_ref, h_ref, act_ref, *, n_chunks):
    gate = mod_ref[0, 5:6, :]
    z = DN_ALPHA * x_ref[...] + (1.0 + gate) * _dot(y_ref[...], wm_ref[...])
    x1 = _layer_norm(z, lng1_ref[...], lnb1_ref[...])
    o_ref[...] = _swiglu_post_norm(x1, mod_ref, 6, wi_ref, wo_ref, lng2_ref, lnb2_ref, h_ref, act_ref, n_chunks)


def _ffn(x, mod, row0, w_in, w_out, ln_g, ln_b, seq):
    n_tok, d = x.shape
    f = w_out[0].shape[-2]
    tm = min(TOKEN_TILE, seq)
    tiles_per_seq = seq // tm
    return pl.pallas_call(
        functools.partial(_ffn_kernel, row0=row0, n_chunks=f // MXU_WIDTH),
        grid=(n_tok // tm,),
        in_specs=[
            pl.BlockSpec((tm, d), lambda i: (i, 0)),
            pl.BlockSpec((1, 9, d), lambda i: (i // tiles_per_seq, 0, 0)),
            _resident_slice(*w_in),
            _resident_slice(*w_out),
            _resident((1, d)),
            _resident((1, d)),
        ],
        out_specs=pl.BlockSpec((tm, d), lambda i: (i, 0)),
        out_shape=jax.ShapeDtypeStruct((n_tok, d), F32),
        scratch_shapes=[pltpu.VMEM((tm, d), BF16), pltpu.VMEM((tm, f), BF16)],
        compiler_params=_params("parallel"),
        name="ffn",
    )(x, mod, w_in[0], w_out[0], ln_g, ln_b)


def _mixer_out_ffn(y, x, mod, w_mix, ln1, w_in, w_out, ln2, seq):
    n_tok, d = x.shape
    k = y.shape[1]
    f = w_out[0].shape[-2]
    tm = min(TOKEN_TILE, seq)
    tiles_per_seq = seq // tm
    return pl.pallas_call(
        functools.partial(_mixer_out_ffn_kernel, n_chunks=f // MXU_WIDTH),
        grid=(n_tok // tm,),
        in_specs=[
            pl.BlockSpec((tm, k), lambda i: (i, 0)),
            pl.BlockSpec((tm, d), lambda i: (i, 0)),
            pl.BlockSpec((1, 9, d), lambda i: (i // tiles_per_seq, 0, 0)),
            _resident_slice(*w_mix), _resident((1, d)), _resident((1, d)),
            _resident_slice(*w_in), _resident_slice(*w_out), _resident((1, d)), _resident((1, d)),
        ],
        out_specs=pl.BlockSpec((tm, d), lambda i: (i, 0)),
        out_shape=jax.ShapeDtypeStruct((n_tok, d), F32),
        scratch_shapes=[pltpu.VMEM((tm, d), BF16), pltpu.VMEM((tm, f), BF16)],
        compiler_params=_params("parallel"),
        name="mixer_out_ffn",
    )(y, x, mod, w_mix[0], *ln1, w_in[0], w_out[0], *ln2)


def _proj_kernel(x_ref, mod_ref, w_ref, o_ref, *, row0, tn):
    x = x_ref[...]
    shift = mod_ref[0, row0:row0 + 1, :]
    scale = mod_ref[0, row0 + 1:row0 + 2, :]
    h = (x * (1.0 + scale) + shift).astype(BF16)
    for j in range(o_ref.shape[1] // tn):
        o_ref[:, j * tn:(j + 1) * tn] = _dot(h, w_ref[:, j * tn:(j + 1) * tn]).astype(o_ref.dtype)


def _proj(x, mod, row0, w, seq, tn):
    n_tok, d = x.shape
    n = w[0].shape[-1]
    tm = min(TOKEN_TILE, seq)
    tiles_per_seq = seq // tm
    return pl.pallas_call(
        functools.partial(_proj_kernel, row0=row0, tn=tn),
        grid=(n_tok // tm,),
        in_specs=[
            pl.BlockSpec((tm, d), lambda i: (i, 0)),
            pl.BlockSpec((1, 9, d), lambda i: (i // tiles_per_seq, 0, 0)),
            _resident_slice(*w),
        ],
        out_specs=pl.BlockSpec((tm, n), lambda i: (i, 0)),
        out_shape=jax.ShapeDtypeStruct((n_tok, n), BF16),
        compiler_params=_params("parallel"),
        name="mixer_in_proj",
    )(x, mod, w[0])


def _nsa_proj_kernel(x_ref, mod_ref, w_ref, feat_ref, q_ref, kcv_ref, keys_ref, vals_ref, gates_ref, *, row0, q_scale):
    x = x_ref[...]
    shift = mod_ref[0, row0:row0 + 1, :]
    scale = mod_ref[0, row0 + 1:row0 + 2, :]
    h = (x * (1.0 + scale) + shift).astype(BF16)
    half = keys_ref.shape[1] // 2
    lane_tile = kcv_ref.shape[2]
    col = 0
    for ref in (q_ref, kcv_ref, keys_ref, vals_ref, gates_ref):
        n = ref.shape[1] if ref is not kcv_ref else kcv_ref.shape[0] * lane_tile
        step = min(n, MXU_WIDTH)
        for j in range(0, n, step):
            y = _dot(h, w_ref[:, col + j:col + j + step])
            if ref is q_ref:
                y = y * q_scale
            elif ref is keys_ref:
                f = feat_ref[:, 0:KEY_WIDTH] if j < half else feat_ref[:, KEY_WIDTH:2 * KEY_WIDTH]
                y = y + jnp.concatenate([f] * (step // KEY_WIDTH), 1).astype(F32)
            elif ref is gates_ref:
                y = _sigmoid(y)
            if ref is kcv_ref:
                for c in range(step // lane_tile):
                    kcv_ref[(j + c * lane_tile) // lane_tile] = y[:, c * lane_tile:(c + 1) * lane_tile]
            else:
                ref[:, j:j + step] = y.astype(ref.dtype)
        col += n


def _nsa_proj(x, mod, row0, w, feat, seq, d, kv, q_scale):
    n_tok = x.shape[0]
    groups = NSA_GROUPS
    tm = min(TOKEN_TILE, seq)
    tiles_per_seq = seq // tm
    widths = (d, 2 * kv, 2 * groups * KEY_WIDTH, 2 * kv, w.shape[1] - d - 4 * kv - 2 * groups * KEY_WIDTH)
    dtypes = (BF16, F32, BF16, BF16, F32)
    lane = 128
    return pl.pallas_call(
        functools.partial(_nsa_proj_kernel, row0=row0, q_scale=q_scale),
        grid=(n_tok // tm,),
        in_specs=[
            pl.BlockSpec((tm, d), lambda i: (i, 0)),
            pl.BlockSpec((1, 9, d), lambda i: (i // tiles_per_seq, 0, 0)),
            _resident(w.shape),
            pl.BlockSpec((tm, 2 * KEY_WIDTH), lambda i: (i % tiles_per_seq, 0)),
        ],
        out_specs=[pl.BlockSpec((tm, n), lambda i: (i, 0)) if k != 1 else
                   pl.BlockSpec((n // lane, tm, lane), lambda i: (0, i, 0)) for k, n in enumerate(widths)],
        out_shape=[jax.ShapeDtypeStruct((n_tok, n) if k != 1 else (n // lane, n_tok, lane), dt)
                   for k, (n, dt) in enumerate(zip(widths, dtypes))],
        compiler_params=_params("parallel"),
        name="nsa_in_proj",
    )(x, mod, w, feat)


def _ret_kernel(q_ref, k_ref, v_ref, g_ref, o_ref, state_ref, *, chunk, n_chunks, k_scale):
    heads = state_ref.shape[0]
    dk, dv = state_ref.shape[1], state_ref.shape[2]
    row = lax.broadcasted_iota(jnp.int32, (chunk, chunk), 0)
    col = lax.broadcasted_iota(jnp.int32, (chunk, chunk), 1)
    diff = (row - col).astype(F32)
    pos = lax.broadcasted_iota(jnp.int32, (chunk, 1), 0).astype(F32)
    decays = []
    for hh in range(heads):
        head = (pl.program_id(1) * heads + hh).astype(F32)
        log_g = jnp.log(1.0 - jnp.exp2(-5.0 - (jnp.zeros((1, 1), F32) + head)))
        decays.append((jnp.where(diff >= 0, jnp.exp(log_g * jnp.maximum(diff, 0.0)), 0.0),
                       jnp.exp(log_g * (pos + 1.0)),
                       jnp.exp(log_g * (chunk - 1.0 - pos)),
                       jnp.exp(log_g * float(chunk))))
    state_ref[...] = jnp.zeros_like(state_ref)

    def step(c, carry):
        rows = pl.ds(pl.multiple_of(c * chunk, chunk), chunk)
        for hh in range(heads):
            decay_intra, decay_q, decay_k, decay_state = decays[hh]
            q = q_ref[0, rows, hh * dk:(hh + 1) * dk]
            k = k_ref[0, rows, hh * dk:(hh + 1) * dk] * k_scale
            v = v_ref[0, rows, hh * dv:(hh + 1) * dv]
            scores = lax.dot_general(q, k, (((1,), (1,)), ((), ())), preferred_element_type=F32) * decay_intra
            state = state_ref[hh]
            o = _dot(scores.astype(BF16), v) + _dot((q.astype(F32) * decay_q).astype(BF16), state.astype(BF16))
            k_dec_t = (k.astype(F32) * decay_k).T.astype(BF16)
            state_ref[hh] = decay_state * state + _dot(k_dec_t, v)
            mu = jnp.mean(o, -1, keepdims=True)
            oc = o - mu
            var = jnp.mean(oc * oc, -1, keepdims=True)
            o = oc * lax.rsqrt(var + GN_EPS)
            g = g_ref[0, rows, hh * dv:(hh + 1) * dv].astype(F32)
            o_ref[0, rows, hh * dv:(hh + 1) * dv] = (o * (g * _sigmoid(g))).astype(o_ref.dtype)
        return carry

    lax.fori_loop(0, n_chunks, step, 0)


def _retention_core(proj, batch, seq, dk, dv):
    heads = RET_HEADS
    hps = RET_HEADS_PER_STEP
    chunk = min(RET_CHUNK, seq)
    k_blk0 = heads * dk // (hps * dk)
    v_blk0 = 2 * heads * dk // (hps * dv)
    g_blk0 = (2 * heads * dk + heads * dv) // (hps * dv)
    return pl.pallas_call(
        functools.partial(_ret_kernel, chunk=chunk, n_chunks=seq // chunk, k_scale=dk ** -0.5),
        grid=(batch, heads // hps),
        in_specs=[
            pl.BlockSpec((1, seq, hps * dk), lambda b, h: (b, 0, h)),
            pl.BlockSpec((1, seq, hps * dk), lambda b, h: (b, 0, k_blk0 + h)),
            pl.BlockSpec((1, seq, hps * dv), lambda b, h: (b, 0, v_blk0 + h)),
            pl.BlockSpec((1, seq, hps * dv), lambda b, h: (b, 0, g_blk0 + h)),
        ],
        out_specs=pl.BlockSpec((1, seq, hps * dv), lambda b, h: (b, 0, h)),
        out_shape=jax.ShapeDtypeStruct((batch, seq, heads * dv), BF16),
        scratch_shapes=[pltpu.VMEM((hps, dk, dv), F32)],
        compiler_params=_params("parallel", "parallel"),
        name="retention_core",
    )(proj, proj, proj, proj)


def _gelu_tanh(x):
    return 0.5 * x * (1.0 + jnp.tanh(0.7978845608028654 * (x + 0.044715 * (x * x * x))))


def _compress_kernel(x_ref, pek_ref, pev_ref, k1_ref, k2_ref, v1_ref, v2_ref, ok_ref, ov_ref, y_ref):
    groups, n_rows, half = y_ref.shape
    dh = half // CMP_STRIDE
    kv = groups * dh
    per_tile = x_ref.shape[2] // dh

    def mlp(lane0, pe_ref, w1_ref, w2_ref, out_ref):
        for r in range(CMP_STRIDE):
            for c in range(groups // per_tile):
                piece = x_ref[lane0 // x_ref.shape[2] + c, pl.ds(r, n_rows, stride=CMP_STRIDE), :]
                for k in range(per_tile):
                    y_ref[c * per_tile + k, :, r * dh:(r + 1) * dh] = piece[:, k * dh:(k + 1) * dh].astype(BF16)
        bias = _dot(pe_ref[...].astype(BF16), w1_ref[...])
        for g in range(groups):
            y = y_ref[g]
            first = _dot(y, w1_ref[0:half, :])
            second = _dot(y, w1_ref[half:2 * half, :])
            hidden = first + pltpu.roll(second, n_rows - 1, 0) + bias
            out_ref[0, g] = _dot(_gelu_tanh(hidden).astype(BF16), w2_ref[...]).astype(out_ref.dtype)

    mlp(0, pek_ref, k1_ref, k2_ref, ok_ref)
    mlp(kv, pev_ref, v1_ref, v2_ref, ov_ref)


def _compress(kcv, pe_k, pe_v, k1, k2, v1, v2, groups, batch):
    lane_tiles, n_tok, lane = kcv.shape
    seq = n_tok // batch
    rows = seq // CMP_STRIDE
    hidden = k1.shape[1]
    dh = k2.shape[1]
    out_blk = pl.BlockSpec((1, groups, rows, dh), lambda b: (b, 0, 0, 0))
    out_sds = jax.ShapeDtypeStruct((batch, groups, rows, dh), BF16)
    return pl.pallas_call(
        _compress_kernel,
        grid=(batch,),
        in_specs=[pl.BlockSpec((lane_tiles, seq, lane), lambda b: (0, b, 0)),
                  _resident(pe_k.shape), _resident(pe_v.shape),
                  _resident(k1.shape), _resident(k2.shape), _resident(v1.shape), _resident(v2.shape)],
        out_specs=[out_blk, out_blk],
        out_shape=[out_sds, out_sds],
        scratch_shapes=[pltpu.VMEM((groups, rows, CMP_STRIDE * dh), BF16)],
        compiler_params=_params("parallel"),
        name="nsa_compress",
    )(kcv, pe_k, pe_v, k1, k2, v1, v2)


def _nsa_attn_kernel(q_ref, kc_ref, vct_ref, keys_ref, vals_ref, gates_ref, o_ref,
                     qa_ref, vst_ref, vwt_ref, s_sel_ref, s_win_ref, acc_ref, l8_ref, *, n_sel, top):
    step = pl.program_id(1)
    groups, dh = qa_ref.shape[0], vct_ref.shape[2]
    kv = groups * dh
    all_groups = range(groups)

    @pl.when(step == 0)
    def _():
        def sel_tile(i, carry):
            rows = pl.ds(pl.multiple_of(i * SEL_KEY_TILE, SEL_KEY_TILE), SEL_KEY_TILE)
            vt = vals_ref[0, rows, 0:kv].astype(F32).T
            for g in all_groups:
                vst_ref[g, i] = vt[g * dh:(g + 1) * dh, :].astype(BF16)
            return carry

        def win_tile(i, carry):
            rows = pl.ds(pl.multiple_of(i * WIN_KEY_TILE, WIN_KEY_TILE), WIN_KEY_TILE)
            vt = vals_ref[0, rows, kv:2 * kv].astype(F32).T
            for g in all_groups:
                vwt_ref[g, i] = vt[g * dh:(g + 1) * dh, :].astype(BF16)
            return carry

        lax.fori_loop(0, vst_ref.shape[1], sel_tile, 0)
        lax.fori_loop(0, vwt_ref.shape[1], win_tile, 0)

    def one_q_block(sub, carry):
        q_rows = pl.ds(pl.multiple_of(sub * Q_BLOCK, Q_BLOCK), Q_BLOCK)
        _nsa_q_block(step * Q_BLOCKS_PER_STEP + sub, q_rows, q_ref, kc_ref, vct_ref, keys_ref, gates_ref, o_ref,
                     qa_ref, vst_ref, vwt_ref, s_sel_ref, s_win_ref, acc_ref, l8_ref, n_sel, top)
        return carry

    lax.fori_loop(0, Q_BLOCKS_PER_STEP, one_q_block, 0)


def _nsa_q_block(qb, q_rows, q_ref, kc_ref, vct_ref, keys_ref, gates_ref, o_ref,
                 qa_ref, vst_ref, vwt_ref, s_sel_ref, s_win_ref, acc_ref, l8_ref, n_sel, top):
    q0 = qb * Q_BLOCK
    groups, dh, width = qa_ref.shape[0], vct_ref.shape[2], qa_ref.shape[2]
    sel_row0 = KEY_WIDTH - n_sel
    all_groups = range(groups)
    pair = 2 * dh
    assert pair == Q_BLOCK == KEY_WIDTH

    lane = lax.broadcasted_iota(jnp.int32, (1, width), 1)
    t_row = q0 + lane % Q_BLOCK
    piece_row = lax.broadcasted_iota(jnp.int32, (POS_ROWS, width), 0)

    for g in all_groups:
        head = (g * NSA_HPG + lane // Q_BLOCK).astype(F32)
        slope = jnp.exp2(-8.0 * (head + 1.0) / NSA_HEADS) * LOG2_E
        hi = slope.astype(BF16).astype(F32)
        rest = slope - hi
        mid = rest.astype(BF16).astype(F32)
        lo = (rest - mid).astype(BF16).astype(F32)
        feat = jnp.zeros((POS_ROWS, width), F32)
        for idx, piece in enumerate((SEL_BLOCK * hi, SEL_BLOCK * mid, SEL_BLOCK * lo, hi, mid, lo)):
            feat = jnp.where(piece_row == idx, piece, feat)
        for p in range(NSA_HPG // 2):
            lanes0 = (g * NSA_HPG + 2 * p) * dh
            two_heads = q_ref[0, q_rows, lanes0:lanes0 + pair].astype(F32).T
            qa_ref[g, 0:dh, (2 * p) * Q_BLOCK:(2 * p + 1) * Q_BLOCK] = two_heads[0:dh].astype(BF16)
            qa_ref[g, 0:dh, (2 * p + 1) * Q_BLOCK:(2 * p + 2) * Q_BLOCK] = two_heads[dh:pair].astype(BF16)
        qa_ref[g, dh:dh + POS_ROWS, :] = feat.astype(BF16)
        qa_ref[g, dh + POS_ROWS:KEY_WIDTH, :] = jnp.zeros((KEY_WIDTH - dh - POS_ROWS, width), BF16)

    n_cmp_pad = kc_ref.shape[2]
    blk = lax.broadcasted_iota(jnp.int32, (n_cmp_pad, width), 0)
    mask_c = blk * CMP_STRIDE + (CMP_BLOCK - 1) <= t_row
    any_visible = t_row >= CMP_BLOCK - 1
    scores_c = [_dot(kc_ref[0, g], qa_ref[g]) for g in all_groups]
    p_cmp = []
    for g in all_groups:
        s = jnp.where(mask_c, scores_c[g], NEG_INF)
        e = jnp.exp2(s - jnp.max(s, 0, keepdims=True))
        p_cmp.append(e * jnp.where(any_visible, 1.0 / jnp.sum(e, 0, keepdims=True), 0.0))
    o_cmp = [_dot(vct_ref[0, g], p_cmp[g].astype(BF16)) for g in all_groups]

    sel_i = lax.broadcasted_iota(jnp.int32, (n_sel, n_cmp_pad), 0) * SEL_BLOCK
    cmp_i = lax.broadcasted_iota(jnp.int32, (n_sel, n_cmp_pad), 1) * CMP_STRIDE
    overlap = jnp.maximum(jnp.minimum(cmp_i + CMP_BLOCK, sel_i + SEL_BLOCK) - jnp.maximum(cmp_i, sel_i), 0)
    overlap = (overlap.astype(F32) * (1.0 / CMP_BLOCK)).astype(BF16)
    j_idx = lax.broadcasted_iota(jnp.int32, (n_sel, Q_BLOCK), 0)
    t_sel = q0 + lax.broadcasted_iota(jnp.int32, (n_sel, Q_BLOCK), 1)
    cur = t_sel // SEL_BLOCK
    forced = (j_idx == 0) | (j_idx == cur) | (j_idx == cur - 1)
    valid = j_idx * SEL_BLOCK <= t_sel
    pieces = []
    for g in all_groups:
        p_sum = p_cmp[g][:, 0:Q_BLOCK]
        for h in range(1, NSA_HPG):
            p_sum = p_sum + p_cmp[g][:, h * Q_BLOCK:(h + 1) * Q_BLOCK]
        p_hi = p_sum.astype(BF16)
        rest = p_sum - p_hi.astype(F32)
        p_mid = rest.astype(BF16)
        pieces.append((p_hi, p_mid, (rest - p_mid.astype(F32)).astype(BF16)))
    imps = [_dot(overlap, hi) + _dot(overlap, mid) + _dot(overlap, lo) for hi, mid, lo in pieces]
    for g in all_groups:
        score = jnp.where(valid, jnp.where(forced, FORCED_SCORE, imps[g]), INVALID_SCORE)
        tiles = [score[r:r + 8, :] for r in range(0, n_sel, 8)]
        ranks = [jnp.zeros((8, Q_BLOCK), F32) for _ in tiles]
        for jp in range(n_sel):
            other = jnp.broadcast_to(score[jp:jp + 1, :], (8, Q_BLOCK))
            for v, tile_scores in enumerate(tiles):
                if jp < 8 * v:
                    ahead = other >= tile_scores
                elif jp >= 8 * v + 7:
                    ahead = other > tile_scores
                else:
                    ahead = (other > tile_scores) | ((other == tile_scores) & (j_idx[0:8, :] > jp - 8 * v))
                ranks[v] = ranks[v] + jnp.where(ahead, 1.0, 0.0)
        rank = jnp.concatenate(ranks, 0)
        block_bias = jnp.where(rank < top, 0.0, NEG_INF)
        qa_ref[g, sel_row0:KEY_WIDTH, :] = jnp.concatenate([block_bias] * NSA_HPG, 1).astype(BF16)

    def row_max(sc):
        return jnp.max(sc.reshape(sc.shape[0] // 8, 8, width), 0)

    def for_tiles(n, body, init):
        carry, start, width_ = init, 0, TILE_UNROLL
        while width_ >= 1:
            def several(j, c, start=start, width_=width_):
                for u in range(width_):
                    c = body(start + width_ * j + u, c)
                return c
            trips = (n - start) // width_
            carry = lax.fori_loop(0, trips, several, carry)
            start = start + trips * width_
            width_ //= 2
        return carry

    def softmax_values(s_ref, vt_ref, first_tile, n_tiles, m8):
        m = [jnp.max(m8[g], 0, keepdims=True) for g in all_groups]
        acc_ref[...] = jnp.zeros_like(acc_ref)
        l8_ref[...] = jnp.zeros_like(l8_ref)

        def step(i, carry):
            pes = [jnp.exp2(s_ref[g, i] - m[g]) for g in all_groups]
            for g in all_groups:
                l8_ref[g] += jnp.sum(pes[g].reshape(pes[g].shape[0] // 8, 8, width), 0)
                acc_ref[g] += _dot(vt_ref[g, first_tile + i], pes[g].astype(BF16))
            return carry

        for_tiles(n_tiles, step, 0)
        return [(acc_ref[g], 1.0 / jnp.sum(l8_ref[g], 0, keepdims=True)) for g in all_groups]

    m8_init = tuple(jnp.full((8, width), NEG_INF, F32) for _ in all_groups)

    tile = SEL_KEY_TILE

    def sel_scores(g, kt):
        rows = pl.ds(pl.multiple_of(kt * tile, tile), tile)
        return _dot(keys_ref[0, rows, g * KEY_WIDTH:(g + 1) * KEY_WIDTH], qa_ref[g])

    def sel_pass(kt, m8):
        out = []
        for g in all_groups:
            sc = sel_scores(g, kt)
            s_sel_ref[g, kt] = sc
            out.append(jnp.maximum(m8[g], row_max(sc)))
        return tuple(out)

    last = (q0 + Q_BLOCK - 1) // tile
    m8 = list(for_tiles(last, sel_pass, m8_init))
    visible = last * tile + lax.broadcasted_iota(jnp.int32, (tile, width), 0) <= t_row
    last_scores = [sel_scores(g, last) for g in all_groups]
    for g in all_groups:
        sc = jnp.where(visible, last_scores[g], NEG_INF)
        s_sel_ref[g, last] = sc
        m8[g] = jnp.maximum(m8[g], row_max(sc))
    o_sel = softmax_values(s_sel_ref, vst_ref, 0, last + 1, m8)

    tile = WIN_KEY_TILE
    first = jnp.maximum(q0 - WINDOW, 0) // tile
    n_win = (q0 + Q_BLOCK - 1) // tile - first + 1
    key_row = lax.broadcasted_iota(jnp.int32, (tile, width), 0)

    def win_pass(i, m8):
        kt = first + i
        rows = pl.ds(pl.multiple_of(kt * tile, tile), tile)
        dist = (t_row - kt * tile) - key_row
        in_window = (dist >= 0) & (dist < WINDOW)
        out = []
        for g in all_groups:
            k_tile = keys_ref[0, rows, (groups + g) * KEY_WIDTH:(groups + g + 1) * KEY_WIDTH]
            sc = jnp.where(in_window, _dot(k_tile, qa_ref[g]), NEG_INF)
            s_win_ref[g, i] = sc
            out.append(jnp.maximum(m8[g], row_max(sc)))
        return tuple(out)

    m8 = for_tiles(n_win, win_pass, m8_init)
    o_win = softmax_values(s_win_ref, vwt_ref, first, n_win, m8)

    gates_t = gates_ref[0, q_rows, :].T
    for g in all_groups:
        per_head = []
        for h in range(NSA_HPG):
            lanes = slice(h * Q_BLOCK, (h + 1) * Q_BLOCK)
            row = (g * NSA_HPG + h) * 3
            (acc_sel, inv_sel), (acc_win, inv_win) = o_sel[g], o_win[g]
            per_head.append(gates_t[row:row + 1, :] * o_cmp[g][:, lanes]
                            + (gates_t[row + 1:row + 2, :] * inv_sel[:, lanes]) * acc_sel[:, lanes]
                            + (gates_t[row + 2:row + 3, :] * inv_win[:, lanes]) * acc_win[:, lanes])
        for p in range(NSA_HPG // 2):
            lanes0 = (g * NSA_HPG + 2 * p) * dh
            two_heads = jnp.concatenate(per_head[2 * p:2 * p + 2], 0)
            o_ref[0, q_rows, lanes0:lanes0 + pair] = two_heads.T.astype(o_ref.dtype)


def _nsa_attention(q, kc, vct, keys, vals, gates, groups):
    batch, seq, d = q.shape
    dh = vct.shape[2]
    width = NSA_HPG * Q_BLOCK
    n_qb = seq // Q_BLOCK
    n_sel = seq // SEL_BLOCK
    top = min(SEL_TOP, n_sel)
    assert dh + POS_ROWS + n_sel <= KEY_WIDTH
    per_b = lambda shape: pl.BlockSpec((1,) + shape, lambda b, i: (b,) + (0,) * len(shape))
    per_q = lambda lanes: pl.BlockSpec((1, Q_BLOCKS_PER_STEP * Q_BLOCK, lanes), lambda b, i: (b, i, 0))
    n_win_tiles = min(WINDOW + Q_BLOCK, seq) // WIN_KEY_TILE
    return pl.pallas_call(
        functools.partial(_nsa_attn_kernel, n_sel=n_sel, top=top),
        grid=(batch, n_qb // Q_BLOCKS_PER_STEP),
        in_specs=[
            per_q(d),
            per_b(kc.shape[1:]), per_b(vct.shape[1:]),
            per_b(keys.shape[1:]), per_b(vals.shape[1:]),
            per_q(gates.shape[2]),
        ],
        out_specs=per_q(d),
        out_shape=jax.ShapeDtypeStruct((batch, seq, d), BF16),
        scratch_shapes=[pltpu.VMEM((groups, KEY_WIDTH, width), BF16),
                        pltpu.VMEM((groups, seq // SEL_KEY_TILE, dh, SEL_KEY_TILE), BF16),
                        pltpu.VMEM((groups, seq // WIN_KEY_TILE, dh, WIN_KEY_TILE), BF16),
                        pltpu.VMEM((groups, seq // SEL_KEY_TILE, SEL_KEY_TILE, width), F32),
                        pltpu.VMEM((groups, n_win_tiles, WIN_KEY_TILE, width), F32),
                        pltpu.VMEM((groups, dh, width), F32),
                        pltpu.VMEM((groups, 8, width), F32)],
        compiler_params=_params("parallel", "arbitrary"),
        name="nsa_attention",
    )(q, kc, vct, keys, vals, gates)


def _key_features(pos, dh, n_sel):
    block, offset = pos // SEL_BLOCK, pos % SEL_BLOCK
    feat = np.zeros((pos.shape[0], KEY_WIDTH), np.float32)
    feat[:, dh:dh + 3] = block[:, None]
    feat[:, dh + 3:dh + 6] = offset[:, None]
    if n_sel:
        feat[:, KEY_WIDTH - n_sel:] = block[:, None] == np.arange(n_sel)[None, :]
    return feat


def _nsa_weights(w_in, d, kv, groups):
    dh = kv // groups
    part = lambda j: w_in[:, d + j * kv:d + (j + 1) * kv]
    pad_keys = lambda w: jnp.pad(w.reshape(-1, groups, dh), ((0, 0), (0, 0), (0, KEY_WIDTH - dh))).reshape(-1, groups * KEY_WIDTH)
    gate_w = w_in[:, d + 6 * kv:]
    gate_w = jnp.pad(gate_w, ((0, 0), (0, -gate_w.shape[1] % 128)))
    cols = [w_in[:, :d], part(0), part(1), pad_keys(part(2)), pad_keys(part(4)), part(3), part(5), gate_w]
    return jnp.concatenate(cols, 1).astype(BF16)


def _nsa_mixer(x, mod, w_in, pe_k, pe_v, k1, k2, v1, v2, batch, seq):
    d = x.shape[1]
    groups = NSA_GROUPS
    dh = d // NSA_HEADS
    kv = groups * dh
    n_sel = seq // SEL_BLOCK
    positions = np.arange(seq)
    feat = jnp.asarray(np.concatenate([_key_features(positions, dh, n_sel), _key_features(positions, dh, 0)], 1), BF16)
    q, kcv, keys, vals, gates = _nsa_proj(x, mod, 3, _nsa_weights(w_in, d, kv, groups), feat, seq, d, kv,
                                          dh ** -0.5 * LOG2_E)
    kc, vc = _compress(kcv, pe_k, pe_v, k1, k2, v1, v2, groups, batch)
    cmp_end = np.arange(seq // CMP_STRIDE) * CMP_STRIDE + CMP_BLOCK - 1
    cmp_feat = jnp.asarray(_key_features(cmp_end, dh, 0)[:, dh:], BF16)
    kc = jnp.concatenate([kc, jnp.broadcast_to(cmp_feat, kc.shape[:2] + cmp_feat.shape)], -1)
    o = _nsa_attention(q.reshape(batch, seq, d), kc, vc.transpose(0, 1, 3, 2), keys.reshape(batch, seq, -1),
                       vals.reshape(batch, seq, -1), gates.reshape(batch, seq, -1), groups)
    return o.reshape(batch * seq, d)


def kernel(x, c, ada_w, ada_b, ln_g, ln_b, ffn_w_in, ffn_w_out, ret_w_in, ret_w_out, nsa_w_in, nsa_w_out,
           nsa_pe_k, nsa_pe_v, nsa_ck_w1, nsa_ck_w2, nsa_cv_w1, nsa_cv_w2):
    batch, seq, d = x.shape
    depth = ada_w.shape[0]
    assert depth == DEPTH
    mod_all = _ada_mod(c, ada_w, ada_b).reshape(depth, batch, 9, d)
    xf = x.reshape(batch * seq, d)
    dk = d // RET_HEADS
    dv = 2 * dk
    ffn_wi, ffn_wo = ffn_w_in.astype(BF16), ffn_w_out.astype(BF16)
    ret_wi, ret_wo, nsa_wo = ret_w_in.astype(BF16), ret_w_out.astype(BF16), nsa_w_out.astype(BF16)
    for i in range(depth):
        mod = mod_all[i]
        ln = lambda k: (ln_g[i, k].reshape(1, d), ln_b[i, k].reshape(1, d))
        xf = _ffn(xf, mod, 0, (ffn_wi, (i, 0)), (ffn_wo, (i, 0)), *ln(0), seq)
        j = i // N_MIXERS
        if i % N_MIXERS == 0:
            proj = _proj(xf, mod, 3, (ret_wi, (j,)), seq, 2 * MXU_WIDTH)
            y = _retention_core(proj.reshape(batch, seq, -1), batch, seq, dk, dv).reshape(batch * seq, -1)
            w_mix = (ret_wo, (j,))
        else:
            y = _nsa_mixer(xf, mod, nsa_w_in[j], nsa_pe_k[j].reshape(1, -1), nsa_pe_v[j].reshape(1, -1),
                           nsa_ck_w1[j].astype(BF16), nsa_ck_w2[j].astype(BF16),
                           nsa_cv_w1[j].astype(BF16), nsa_cv_w2[j].astype(BF16), batch, seq)
            w_mix = (nsa_wo, (j,))
        xf = _mixer_out_ffn(y, xf, mod, w_mix, ln(1), (ffn_wi, (i, 1)), (ffn_wo, (i, 1)), ln(2), seq)
    return xf.reshape(batch, seq, d)
```

```python
import functools

import numpy as np
import jax
import jax.numpy as jnp
from jax import lax
from jax.experimental import pallas as pl
from jax.experimental.pallas import tpu as pltpu

F32 = jnp.float32
BF16 = jnp.bfloat16

DEPTH = 4
N_MIXERS = 2
RET_HEADS = 4
NSA_HEADS = 16
NSA_GROUPS = 4
NSA_HPG = NSA_HEADS // NSA_GROUPS
CMP_BLOCK = 32
CMP_STRIDE = 16
SEL_BLOCK = 64
SEL_TOP = 16
WINDOW = 512
Q_BLOCK = 128
FORCED_SCORE = 1e4
INVALID_SCORE = -1.0
FFN_RES = 0.5
DN_ALPHA = (2 * DEPTH) ** 0.25
LN_EPS = 1e-5
GN_EPS = 1e-6
NEG_INF = -1e30

V7X_VMEM_LIMIT_BYTES = 56 * 1024 * 1024
MXU_WIDTH = 256
TOKEN_TILE = 1024
RET_CHUNK = 256
RET_HEADS_PER_STEP = 2
SEL_KEY_TILE = 256
WIN_KEY_TILE = 128
Q_BLOCKS_PER_STEP = 4
TILE_UNROLL = 4
KEY_WIDTH = 128
POS_ROWS = 16
LOG2_E = 1.4426950408889634


def _dot(a, b):
    return jnp.dot(a, b, preferred_element_type=F32)


def _sigmoid(x):
    return 1.0 / (1.0 + jnp.exp(-x))


def _layer_norm(z, g, b):
    mu = jnp.mean(z, -1, keepdims=True)
    zc = z - mu
    var = jnp.mean(zc * zc, -1, keepdims=True)
    return zc * lax.rsqrt(var + LN_EPS) * g + b


def _params(*sem):
    return pltpu.CompilerParams(dimension_semantics=sem, vmem_limit_bytes=V7X_VMEM_LIMIT_BYTES)


def _resident(shape):
    return pl.BlockSpec(shape, lambda *_: (0,) * len(shape), pipeline_mode=pl.Buffered(1))


def _resident_slice(stacked, index):
    rest = stacked.shape[len(index):]
    return pl.BlockSpec((None,) * len(index) + rest, lambda *_: tuple(index) + (0,) * len(rest),
                        pipeline_mode=pl.Buffered(1))


def _ada_kernel(c_ref, w_ref, b_ref, o_ref):
    c = c_ref[...]
    c_act = (c * _sigmoid(c)).astype(BF16)
    o_ref[0] = _dot(c_act, w_ref[0].astype(BF16)) + b_ref[0]


def _ada_mod(c, ada_w, ada_b):
    depth, d, n = ada_w.shape
    b = c.shape[0]
    tn = n // 8
    return pl.pallas_call(
        _ada_kernel,
        grid=(depth, n // tn),
        in_specs=[
            pl.BlockSpec((b, d), lambda l, j: (0, 0)),
            pl.BlockSpec((1, d, tn), lambda l, j: (l, 0, j)),
            pl.BlockSpec((1, 1, tn), lambda l, j: (l, 0, j)),
        ],
        out_specs=pl.BlockSpec((1, b, tn), lambda l, j: (l, 0, j)),
        out_shape=jax.ShapeDtypeStruct((depth, b, n), F32),
        compiler_params=_params("parallel", "parallel"),
        name="ada_mod",
    )(c, ada_w, ada_b.reshape(depth, 1, n))


def _swiglu_post_norm(x, mod_ref, row0, wi_ref, wo_ref, lng_ref, lnb_ref, h_ref, act_ref, n_chunks):
    shift = mod_ref[0, row0:row0 + 1, :]
    scale = mod_ref[0, row0 + 1:row0 + 2, :]
    gate = mod_ref[0, row0 + 2:row0 + 3, :]
    h_ref[...] = (x * (1.0 + scale) + shift).astype(BF16)
    f = wo_ref.shape[0]
    fc = f // n_chunks
    for j in range(n_chunks):
        h = h_ref[...]
        a = _dot(h, wi_ref[:, j * fc:(j + 1) * fc])
        u = _dot(h, wi_ref[:, f + j * fc:f + (j + 1) * fc])
        act_ref[:, j * fc:(j + 1) * fc] = (a * _sigmoid(a) * u).astype(BF16)
    y = _dot(act_ref[...], wo_ref[...])
    z = DN_ALPHA * x + FFN_RES * (1.0 + gate) * y
    return _layer_norm(z, lng_ref[...], lnb_ref[...])


def _ffn_kernel(x_ref, mod_ref, wi_ref, wo_ref, lng_ref, lnb_ref, o_ref, h_ref, act_ref, *, row0, n_chunks):
    o_ref[...] = _swiglu_post_norm(x_ref[...], mod_ref, row0, wi_ref, wo_ref, lng_ref, lnb_ref, h_ref, act_ref, n_chunks)


def _mixer_out_ffn_kernel(y_ref, x_ref, mod_ref, wm_ref, lng1_ref, lnb1_ref, wi_ref, wo_ref, lng2_ref, lnb2_ref,
                          o_ref, h_ref, act_ref, *, n_chunks):
    gate = mod_ref[0, 5:6, :]
    z = DN_ALPHA * x_ref[...] + (1.0 + gate) * _dot(y_ref[...], wm_ref[...])
    x1 = _layer_norm(z, lng1_ref[...], lnb1_ref[...])
    o_ref[...] = _swiglu_post_norm(x1, mod_ref, 6, wi_ref, wo_ref, lng2_ref, lnb2_ref, h_ref, act_ref, n_chunks)


def _ffn(x, mod, row0, w_in, w_out, ln_g, ln_b, seq):
    n_tok, d = x.shape
    f = w_out[0].shape[-2]
    tm = min(TOKEN_TILE, seq)
    tiles_per_seq = seq // tm
    return pl.pallas_call(
        functools.partial(_ffn_kernel, row0=row0, n_chunks=f // MXU_WIDTH),
        grid=(n_tok // tm,),
        in_specs=[
            pl.BlockSpec((tm, d), lambda i: (i, 0)),
            pl.BlockSpec((1, 9, d), lambda i: (i // tiles_per_seq, 0, 0)),
            _resident_slice(*w_in),
            _resident_slice(*w_out),
            _resident((1, d)),
            _resident((1, d)),
        ],
        out_specs=pl.BlockSpec((tm, d), lambda i: (i, 0)),
        out_shape=jax.ShapeDtypeStruct((n_tok, d), F32),
        scratch_shapes=[pltpu.VMEM((tm, d), BF16), pltpu.VMEM((tm, f), BF16)],
        compiler_params=_params("parallel"),
        name="ffn",
    )(x, mod, w_in[0], w_out[0], ln_g, ln_b)


def _mixer_out_ffn(y, x, mod, w_mix, ln1, w_in, w_out, ln2, seq):
    n_tok, d = x.shape
    k = y.shape[1]
    f = w_out[0].shape[-2]
    tm = min(TOKEN_TILE, seq)
    tiles_per_seq = seq // tm
    return pl.pallas_call(
        functools.partial(_mixer_out_ffn_kernel, n_chunks=f // MXU_WIDTH),
        grid=(n_tok // tm,),
        in_specs=[
            pl.BlockSpec((tm, k), lambda i: (i, 0)),
            pl.BlockSpec((tm, d), lambda i: (i, 0)),
            pl.BlockSpec((1, 9, d), lambda i: (i // tiles_per_seq, 0, 0)),
            _resident_slice(*w_mix), _resident((1, d)), _resident((1, d)),
            _resident_slice(*w_in), _resident_slice(*w_out), _resident((1, d)), _resident((1, d)),
        ],
        out_specs=pl.BlockSpec((tm, d), lambda i: (i, 0)),
        out_shape=jax.ShapeDtypeStruct((n_tok, d), F32),
        scratch_shapes=[pltpu.VMEM((tm, d), BF16), pltpu.VMEM((tm, f), BF16)],
        compiler_params=_params("parallel"),
        name="mixer_out_ffn",
    )(y, x, mod, w_mix[0], *ln1, w_in[0], w_out[0], *ln2)


def _proj_kernel(x_ref, mod_ref, w_ref, o_ref, *, row0, tn):
    x = x_ref[...]
    shift = mod_ref[0, row0:row0 + 1, :]
    scale = mod_ref[0, row0 + 1:row0 + 2, :]
    h = (x * (1.0 + scale) + shift).astype(BF16)
    for j in range(o_ref.shape[1] // tn):
        o_ref[:, j * tn:(j + 1) * tn] = _dot(h, w_ref[:, j * tn:(j + 1) * tn]).astype(o_ref.dtype)


def _proj(x, mod, row0, w, seq, tn):
    n_tok, d = x.shape
    n = w[0].shape[-1]
    tm = min(TOKEN_TILE, seq)
    tiles_per_seq = seq // tm
    return pl.pallas_call(
        functools.partial(_proj_kernel, row0=row0, tn=tn),
        grid=(n_tok // tm,),
        in_specs=[
            pl.BlockSpec((tm, d), lambda i: (i, 0)),
            pl.BlockSpec((1, 9, d), lambda i: (i // tiles_per_seq, 0, 0)),
            _resident_slice(*w),
        ],
        out_specs=pl.BlockSpec((tm, n), lambda i: (i, 0)),
        out_shape=jax.ShapeDtypeStruct((n_tok, n), BF16),
        compiler_params=_params("parallel"),
        name="mixer_in_proj",
    )(x, mod, w[0])


def _nsa_proj_kernel(x_ref, mod_ref, w_ref, feat_ref, q_ref, kcv_ref, keys_ref, vals_ref, gates_ref, *, row0, q_scale):
    x = x_ref[...]
    shift = mod_ref[0, row0:row0 + 1, :]
    scale = mod_ref[0, row0 + 1:row0 + 2, :]
    h = (x * (1.0 + scale) + shift).astype(BF16)
    half = keys_ref.shape[1] // 2
    lane_tile = kcv_ref.shape[2]
    col = 0
    for ref in (q_ref, kcv_ref, keys_ref, vals_ref, gates_ref):
        n = kcv_ref.shape[0] * lane_tile if ref is kcv_ref else ref.shape[1]
        step = min(n, MXU_WIDTH)
        for j in range(0, n, step):
            y = _dot(h, w_ref[:, col + j:col + j + step])
            if ref is q_ref:
                y = y * q_scale
                for r in range(y.shape[0] // Q_BLOCK):
                    q_ref[r, j:j + step, :] = y[r * Q_BLOCK:(r + 1) * Q_BLOCK, :].T.astype(q_ref.dtype)
                continue
            elif ref is keys_ref:
                f = feat_ref[:, 0:KEY_WIDTH] if j < half else feat_ref[:, KEY_WIDTH:2 * KEY_WIDTH]
                y = y + jnp.concatenate([f] * (step // KEY_WIDTH), 1).astype(F32)
            elif ref is gates_ref:
                y = _sigmoid(y)
                for r in range(y.shape[0] // Q_BLOCK):
                    gates_ref[r] = y[r * Q_BLOCK:(r + 1) * Q_BLOCK, :].T
                continue
            if ref is kcv_ref:
                for c in range(step // lane_tile):
                    kcv_ref[(j + c * lane_tile) // lane_tile] = y[:, c * lane_tile:(c + 1) * lane_tile]
            else:
                ref[:, j:j + step] = y.astype(ref.dtype)
        col += n


def _nsa_proj(x, mod, row0, w, feat, seq, d, kv, q_scale):
    n_tok = x.shape[0]
    groups = NSA_GROUPS
    tm = min(TOKEN_TILE, seq)
    tiles_per_seq = seq // tm
    widths = (d, 2 * kv, 2 * groups * KEY_WIDTH, 2 * kv, w.shape[1] - d - 4 * kv - 2 * groups * KEY_WIDTH)
    dtypes = (BF16, F32, BF16, BF16, F32)
    lane = 128
    return pl.pallas_call(
        functools.partial(_nsa_proj_kernel, row0=row0, q_scale=q_scale),
        grid=(n_tok // tm,),
        in_specs=[
            pl.BlockSpec((tm, d), lambda i: (i, 0)),
            pl.BlockSpec((1, 9, d), lambda i: (i // tiles_per_seq, 0, 0)),
            _resident(w.shape),
            pl.BlockSpec((tm, 2 * KEY_WIDTH), lambda i: (i % tiles_per_seq, 0)),
        ],
        out_specs=[pl.BlockSpec((tm // Q_BLOCK, widths[0], Q_BLOCK), lambda i: (i, 0, 0)),
                   pl.BlockSpec((widths[1] // lane, tm, lane), lambda i: (0, i, 0)),
                   pl.BlockSpec((tm, widths[2]), lambda i: (i, 0)),
                   pl.BlockSpec((tm, widths[3]), lambda i: (i, 0)),
                   pl.BlockSpec((tm // Q_BLOCK, widths[4], Q_BLOCK), lambda i: (i, 0, 0))],
        out_shape=[jax.ShapeDtypeStruct((n_tok // Q_BLOCK, widths[0], Q_BLOCK), dtypes[0]),
                   jax.ShapeDtypeStruct((widths[1] // lane, n_tok, lane), dtypes[1]),
                   jax.ShapeDtypeStruct((n_tok, widths[2]), dtypes[2]),
                   jax.ShapeDtypeStruct((n_tok, widths[3]), dtypes[3]),
                   jax.ShapeDtypeStruct((n_tok // Q_BLOCK, widths[4], Q_BLOCK), dtypes[4])],
        compiler_params=_params("parallel"),
        name="nsa_in_proj",
    )(x, mod, w, feat)


def _ret_kernel(q_ref, k_ref, v_ref, g_ref, o_ref, state_ref, *, chunk, n_chunks, k_scale):
    heads = state_ref.shape[0]
    dk, dv = state_ref.shape[1], state_ref.shape[2]
    row = lax.broadcasted_iota(jnp.int32, (chunk, chunk), 0)
    col = lax.broadcasted_iota(jnp.int32, (chunk, chunk), 1)
    diff = (row - col).astype(F32)
    pos = lax.broadcasted_iota(jnp.int32, (chunk, 1), 0).astype(F32)
    decays = []
    for hh in range(heads):
        head = (pl.program_id(1) * heads + hh).astype(F32)
        log_g = jnp.log(1.0 - jnp.exp2(-5.0 - (jnp.zeros((1, 1), F32) + head)))
        decays.append((jnp.where(diff >= 0, jnp.exp(log_g * jnp.maximum(diff, 0.0)), 0.0),
                       jnp.exp(log_g * (pos + 1.0)),
                       jnp.exp(log_g * (chunk - 1.0 - pos)),
                       jnp.exp(log_g * float(chunk))))
    state_ref[...] = jnp.zeros_like(state_ref)

    def step(c, carry):
        rows = pl.ds(pl.multiple_of(c * chunk, chunk), chunk)
        for hh in range(heads):
            decay_intra, decay_q, decay_k, decay_state = decays[hh]
            q = q_ref[0, rows, hh * dk:(hh + 1) * dk]
            k = k_ref[0, rows, hh * dk:(hh + 1) * dk] * k_scale
            v = v_ref[0, rows, hh * dv:(hh + 1) * dv]
            scores = lax.dot_general(q, k, (((1,), (1,)), ((), ())), preferred_element_type=F32) * decay_intra
            state = state_ref[hh]
            o = _dot(scores.astype(BF16), v) + _dot((q.astype(F32) * decay_q).astype(BF16), state.astype(BF16))
            k_dec_t = (k.astype(F32) * decay_k).T.astype(BF16)
            state_ref[hh] = decay_state * state + _dot(k_dec_t, v)
            mu = jnp.mean(o, -1, keepdims=True)
            oc = o - mu
            var = jnp.mean(oc * oc, -1, keepdims=True)
            o = oc * lax.rsqrt(var + GN_EPS)
            g = g_ref[0, rows, hh * dv:(hh + 1) * dv].astype(F32)
            o_ref[0, rows, hh * dv:(hh + 1) * dv] = (o * (g * _sigmoid(g))).astype(o_ref.dtype)
        return carry

    lax.fori_loop(0, n_chunks, step, 0)


def _retention_core(proj, batch, seq, dk, dv):
    heads = RET_HEADS
    hps = RET_HEADS_PER_STEP
    chunk = min(RET_CHUNK, seq)
    k_blk0 = heads * dk // (hps * dk)
    v_blk0 = 2 * heads * dk // (hps * dv)
    g_blk0 = (2 * heads * dk + heads * dv) // (hps * dv)
    return pl.pallas_call(
        functools.partial(_ret_kernel, chunk=chunk, n_chunks=seq // chunk, k_scale=dk ** -0.5),
        grid=(batch, heads // hps),
        in_specs=[
            pl.BlockSpec((1, seq, hps * dk), lambda b, h: (b, 0, h)),
            pl.BlockSpec((1, seq, hps * dk), lambda b, h: (b, 0, k_blk0 + h)),
            pl.BlockSpec((1, seq, hps * dv), lambda b, h: (b, 0, v_blk0 + h)),
            pl.BlockSpec((1, seq, hps * dv), lambda b, h: (b, 0, g_blk0 + h)),
        ],
        out_specs=pl.BlockSpec((1, seq, hps * dv), lambda b, h: (b, 0, h)),
        out_shape=jax.ShapeDtypeStruct((batch, seq, heads * dv), BF16),
        scratch_shapes=[pltpu.VMEM((hps, dk, dv), F32)],
        compiler_params=_params("parallel", "parallel"),
        name="retention_core",
    )(proj, proj, proj, proj)


def _gelu_tanh(x):
    return 0.5 * x * (1.0 + jnp.tanh(0.7978845608028654 * (x + 0.044715 * (x * x * x))))


def _compress_kernel(x_ref, pek_ref, pev_ref, k1_ref, k2_ref, v1_ref, v2_ref, ok_ref, ov_ref, y_ref):
    groups, n_rows, half = y_ref.shape
    dh = half // CMP_STRIDE
    kv = groups * dh
    per_tile = x_ref.shape[2] // dh

    def mlp(lane0, pe_ref, w1_ref, w2_ref, out_ref):
        for r in range(CMP_STRIDE):
            for c in range(groups // per_tile):
                piece = x_ref[lane0 // x_ref.shape[2] + c, pl.ds(r, n_rows, stride=CMP_STRIDE), :]
                for k in range(per_tile):
                    y_ref[c * per_tile + k, :, r * dh:(r + 1) * dh] = piece[:, k * dh:(k + 1) * dh].astype(BF16)
        bias = _dot(pe_ref[...].astype(BF16), w1_ref[...])
        for g in range(groups):
            y = y_ref[g]
            first = _dot(y, w1_ref[0:half, :])
            second = _dot(y, w1_ref[half:2 * half, :])
            hidden = first + pltpu.roll(second, n_rows - 1, 0) + bias
            out_ref[0, g] = _dot(_gelu_tanh(hidden).astype(BF16), w2_ref[...]).astype(out_ref.dtype)

    mlp(0, pek_ref, k1_ref, k2_ref, ok_ref)
    mlp(kv, pev_ref, v1_ref, v2_ref, ov_ref)


def _compress(kcv, pe_k, pe_v, k1, k2, v1, v2, groups, batch):
    lane_tiles, n_tok, lane = kcv.shape
    seq = n_tok // batch
    rows = seq // CMP_STRIDE
    hidden = k1.shape[1]
    dh = k2.shape[1]
    out_blk = pl.BlockSpec((1, groups, rows, dh), lambda b: (b, 0, 0, 0))
    out_sds = jax.ShapeDtypeStruct((batch, groups, rows, dh), BF16)
    return pl.pallas_call(
        _compress_kernel,
        grid=(batch,),
        in_specs=[pl.BlockSpec((lane_tiles, seq, lane), lambda b: (0, b, 0)),
                  _resident(pe_k.shape), _resident(pe_v.shape),
                  _resident(k1.shape), _resident(k2.shape), _resident(v1.shape), _resident(v2.shape)],
        out_specs=[out_blk, out_blk],
        out_shape=[out_sds, out_sds],
        scratch_shapes=[pltpu.VMEM((groups, rows, CMP_STRIDE * dh), BF16)],
        compiler_params=_params("parallel"),
        name="nsa_compress",
    )(kcv, pe_k, pe_v, k1, k2, v1, v2)


def _nsa_attn_kernel(q_ref, kc_ref, vct_ref, keys_ref, vals_ref, gates_ref, o_ref,
                     qa_ref, vst_ref, vwt_ref, s_sel_ref, s_win_ref, acc_ref, l8_ref, *, n_sel, top):
    step = pl.program_id(1)
    groups, dh = qa_ref.shape[0], vct_ref.shape[2]
    kv = groups * dh
    all_groups = range(groups)

    @pl.when(step == 0)
    def _():
        def sel_tile(i, carry):
            rows = pl.ds(pl.multiple_of(i * SEL_KEY_TILE, SEL_KEY_TILE), SEL_KEY_TILE)
            vt = vals_ref[0, rows, 0:kv].astype(F32).T
            for g in all_groups:
                vst_ref[g, i] = vt[g * dh:(g + 1) * dh, :].astype(BF16)
            return carry

        def win_tile(i, carry):
            rows = pl.ds(pl.multiple_of(i * WIN_KEY_TILE, WIN_KEY_TILE), WIN_KEY_TILE)
            vt = vals_ref[0, rows, kv:2 * kv].astype(F32).T
            for g in all_groups:
                vwt_ref[g, i] = vt[g * dh:(g + 1) * dh, :].astype(BF16)
            return carry

        lax.fori_loop(0, vst_ref.shape[1], sel_tile, 0)
        lax.fori_loop(0, vwt_ref.shape[1], win_tile, 0)

    def one_q_block(sub, carry):
        q_rows = pl.ds(pl.multiple_of(sub * Q_BLOCK, Q_BLOCK), Q_BLOCK)
        _nsa_q_block(step * Q_BLOCKS_PER_STEP + sub, sub, q_rows, q_ref, kc_ref, vct_ref, keys_ref, gates_ref, o_ref,
                     qa_ref, vst_ref, vwt_ref, s_sel_ref, s_win_ref, acc_ref, l8_ref, n_sel, top)
        return carry

    lax.fori_loop(0, Q_BLOCKS_PER_STEP, one_q_block, 0)


def _nsa_q_block(qb, sub, q_rows, q_ref, kc_ref, vct_ref, keys_ref, gates_ref, o_ref,
                 qa_ref, vst_ref, vwt_ref, s_sel_ref, s_win_ref, acc_ref, l8_ref, n_sel, top):
    q0 = qb * Q_BLOCK
    groups, dh, width = qa_ref.shape[0], vct_ref.shape[2], qa_ref.shape[2]
    sel_row0 = KEY_WIDTH - n_sel
    all_groups = range(groups)
    pair = 2 * dh
    assert pair == Q_BLOCK == KEY_WIDTH

    lane = lax.broadcasted_iota(jnp.int32, (1, width), 1)
    t_row = q0 + lane % Q_BLOCK
    piece_row = lax.broadcasted_iota(jnp.int32, (POS_ROWS, width), 0)

    for g in all_groups:
        head = (g * NSA_HPG + lane // Q_BLOCK).astype(F32)
        slope = jnp.exp2(-8.0 * (head + 1.0) / NSA_HEADS) * LOG2_E
        hi = slope.astype(BF16).astype(F32)
        rest = slope - hi
        mid = rest.astype(BF16).astype(F32)
        lo = (rest - mid).astype(BF16).astype(F32)
        feat = jnp.zeros((POS_ROWS, width), F32)
        for idx, piece in enumerate((SEL_BLOCK * hi, SEL_BLOCK * mid, SEL_BLOCK * lo, hi, mid, lo)):
            feat = jnp.where(piece_row == idx, piece, feat)
        for h in range(NSA_HPG):
            lanes0 = (g * NSA_HPG + h) * dh
            qa_ref[g, 0:dh, h * Q_BLOCK:(h + 1) * Q_BLOCK] = q_ref[0, sub, lanes0:lanes0 + dh, :]
        qa_ref[g, dh:dh + POS_ROWS, :] = feat.astype(BF16)
        qa_ref[g, dh + POS_ROWS:KEY_WIDTH, :] = jnp.zeros((KEY_WIDTH - dh - POS_ROWS, width), BF16)

    n_cmp_pad = kc_ref.shape[2]
    blk = lax.broadcasted_iota(jnp.int32, (n_cmp_pad, width), 0)
    mask_c = blk * CMP_STRIDE + (CMP_BLOCK - 1) <= t_row
    any_visible = t_row >= CMP_BLOCK - 1
    scores_c = [_dot(kc_ref[0, g], qa_ref[g]) for g in all_groups]
    p_cmp = []
    for g in all_groups:
        s = jnp.where(mask_c, scores_c[g], NEG_INF)
        e = jnp.exp2(s - jnp.max(s, 0, keepdims=True))
        p_cmp.append(e * jnp.where(any_visible, 1.0 / jnp.sum(e, 0, keepdims=True), 0.0))
    o_cmp = [_dot(vct_ref[0, g], p_cmp[g].astype(BF16)) for g in all_groups]

    sel_i = lax.broadcasted_iota(jnp.int32, (n_sel, n_cmp_pad), 0) * SEL_BLOCK
    cmp_i = lax.broadcasted_iota(jnp.int32, (n_sel, n_cmp_pad), 1) * CMP_STRIDE
    overlap = jnp.maximum(jnp.minimum(cmp_i + CMP_BLOCK, sel_i + SEL_BLOCK) - jnp.maximum(cmp_i, sel_i), 0)
    overlap = (overlap.astype(F32) * (1.0 / CMP_BLOCK)).astype(BF16)
    j_idx = lax.broadcasted_iota(jnp.int32, (n_sel, Q_BLOCK), 0)
    t_sel = q0 + lax.broadcasted_iota(jnp.int32, (n_sel, Q_BLOCK), 1)
    cur = t_sel // SEL_BLOCK
    forced = (j_idx == 0) | (j_idx == cur) | (j_idx == cur - 1)
    valid = j_idx * SEL_BLOCK <= t_sel
    pieces = []
    for g in all_groups:
        p_sum = p_cmp[g][:, 0:Q_BLOCK]
        for h in range(1, NSA_HPG):
            p_sum = p_sum + p_cmp[g][:, h * Q_BLOCK:(h + 1) * Q_BLOCK]
        p_hi = p_sum.astype(BF16)
        rest = p_sum - p_hi.astype(F32)
        p_mid = rest.astype(BF16)
        pieces.append((p_hi, p_mid, (rest - p_mid.astype(F32)).astype(BF16)))
    imps = [_dot(overlap, hi) + _dot(overlap, mid) + _dot(overlap, lo) for hi, mid, lo in pieces]
    for g in all_groups:
        score = jnp.where(valid, jnp.where(forced, FORCED_SCORE, imps[g]), INVALID_SCORE)
        tiles = [score[r:r + 8, :] for r in range(0, n_sel, 8)]
        ranks = [jnp.zeros((8, Q_BLOCK), F32) for _ in tiles]
        for jp in range(n_sel):
            other = jnp.broadcast_to(score[jp:jp + 1, :], (8, Q_BLOCK))
            for v, tile_scores in enumerate(tiles):
                if jp < 8 * v:
                    ahead = other >= tile_scores
                elif jp >= 8 * v + 7:
                    ahead = other > tile_scores
                else:
                    ahead = (other > tile_scores) | ((other == tile_scores) & (j_idx[0:8, :] > jp - 8 * v))
                ranks[v] = ranks[v] + jnp.where(ahead, 1.0, 0.0)
        rank = jnp.concatenate(ranks, 0)
        block_bias = jnp.where(rank < top, 0.0, NEG_INF)
        qa_ref[g, sel_row0:KEY_WIDTH, :] = jnp.concatenate([block_bias] * NSA_HPG, 1).astype(BF16)

    def row_max(sc):
        return jnp.max(sc.reshape(sc.shape[0] // 8, 8, width), 0)

    def for_tiles(n, body, init):
        carry, start, width_ = init, 0, TILE_UNROLL
        while width_ >= 1:
            def several(j, c, start=start, width_=width_):
                for u in range(width_):
                    c = body(start + width_ * j + u, c)
                return c
            trips = (n - start) // width_
            carry = lax.fori_loop(0, trips, several, carry)
            start = start + trips * width_
            width_ //= 2
        return carry

    def softmax_values(s_ref, vt_ref, first_tile, n_tiles, m8):
        m = [jnp.max(m8[g], 0, keepdims=True) for g in all_groups]
        acc_ref[...] = jnp.zeros_like(acc_ref)
        l8_ref[...] = jnp.zeros_like(l8_ref)

        def step(i, carry):
            pes = [jnp.exp2(s_ref[g, i] - m[g]) for g in all_groups]
            for g in all_groups:
                l8_ref[g] += jnp.sum(pes[g].reshape(pes[g].shape[0] // 8, 8, width), 0)
                acc_ref[g] += _dot(vt_ref[g, first_tile + i], pes[g].astype(BF16))
            return carry

        for_tiles(n_tiles, step, 0)
        return [(acc_ref[g], 1.0 / jnp.sum(l8_ref[g], 0, keepdims=True)) for g in all_groups]

    m8_init = tuple(jnp.full((8, width), NEG_INF, F32) for _ in all_groups)

    tile = SEL_KEY_TILE

    def sel_scores(g, kt):
        rows = pl.ds(pl.multiple_of(kt * tile, tile), tile)
        return _dot(keys_ref[0, rows, g * KEY_WIDTH:(g + 1) * KEY_WIDTH], qa_ref[g])

    def sel_pass(kt, m8):
        out = []
        for g in all_groups:
            sc = sel_scores(g, kt)
            s_sel_ref[g, kt] = sc
            out.append(jnp.maximum(m8[g], row_max(sc)))
        return tuple(out)

    last = (q0 + Q_BLOCK - 1) // tile
    m8 = list(for_tiles(last, sel_pass, m8_init))
    visible = last * tile + lax.broadcasted_iota(jnp.int32, (tile, width), 0) <= t_row
    last_scores = [sel_scores(g, last) for g in all_groups]
    for g in all_groups:
        sc = jnp.where(visible, last_scores[g], NEG_INF)
        s_sel_ref[g, last] = sc
        m8[g] = jnp.maximum(m8[g], row_max(sc))
    o_sel = softmax_values(s_sel_ref, vst_ref, 0, last + 1, m8)

    tile = WIN_KEY_TILE
    first = jnp.maximum(q0 - WINDOW, 0) // tile
    n_win = (q0 + Q_BLOCK - 1) // tile - first + 1
    key_row = lax.broadcasted_iota(jnp.int32, (tile, width), 0)

    def win_pass(i, m8):
        kt = first + i
        rows = pl.ds(pl.multiple_of(kt * tile, tile), tile)
        dist = (t_row - kt * tile) - key_row
        in_window = (dist >= 0) & (dist < WINDOW)
        out = []
        for g in all_groups:
            k_tile = keys_ref[0, rows, (groups + g) * KEY_WIDTH:(groups + g + 1) * KEY_WIDTH]
            sc = jnp.where(in_window, _dot(k_tile, qa_ref[g]), NEG_INF)
            s_win_ref[g, i] = sc
            out.append(jnp.maximum(m8[g], row_max(sc)))
        return tuple(out)

    m8 = for_tiles(n_win, win_pass, m8_init)
    o_win = softmax_values(s_win_ref, vwt_ref, first, n_win, m8)

    gates_t = gates_ref[0, sub]
    for g in all_groups:
        per_head = []
        for h in range(NSA_HPG):
            lanes = slice(h * Q_BLOCK, (h + 1) * Q_BLOCK)
            row = (g * NSA_HPG + h) * 3
            (acc_sel, inv_sel), (acc_win, inv_win) = o_sel[g], o_win[g]
            per_head.append(gates_t[row:row + 1, :] * o_cmp[g][:, lanes]
                            + (gates_t[row + 1:row + 2, :] * inv_sel[:, lanes]) * acc_sel[:, lanes]
                            + (gates_t[row + 2:row + 3, :] * inv_win[:, lanes]) * acc_win[:, lanes])
        for p in range(NSA_HPG // 2):
            lanes0 = (g * NSA_HPG + 2 * p) * dh
            two_heads = jnp.concatenate(per_head[2 * p:2 * p + 2], 0)
            o_ref[0, q_rows, lanes0:lanes0 + pair] = two_heads.T.astype(o_ref.dtype)


def _nsa_attention(q, kc, vct, keys, vals, gates, groups):
    batch, n_qb, d, _ = q.shape
    seq = n_qb * Q_BLOCK
    dh = vct.shape[2]
    width = NSA_HPG * Q_BLOCK
    n_sel = seq // SEL_BLOCK
    top = min(SEL_TOP, n_sel)
    assert dh + POS_ROWS + n_sel <= KEY_WIDTH
    per_b = lambda shape: pl.BlockSpec((1,) + shape, lambda b, i: (b,) + (0,) * len(shape))
    per_q = lambda lanes: pl.BlockSpec((1, Q_BLOCKS_PER_STEP * Q_BLOCK, lanes), lambda b, i: (b, i, 0))
    per_q_t = lambda lanes: pl.BlockSpec((1, Q_BLOCKS_PER_STEP, lanes, Q_BLOCK), lambda b, i: (b, i, 0, 0))
    n_win_tiles = min(WINDOW + Q_BLOCK, seq) // WIN_KEY_TILE
    return pl.pallas_call(
        functools.partial(_nsa_attn_kernel, n_sel=n_sel, top=top),
        grid=(batch, n_qb // Q_BLOCKS_PER_STEP),
        in_specs=[
            per_q_t(d),
            per_b(kc.shape[1:]), per_b(vct.shape[1:]),
            per_b(keys.shape[1:]), per_b(vals.shape[1:]),
            per_q_t(gates.shape[2]),
        ],
        out_specs=per_q(d),
        out_shape=jax.ShapeDtypeStruct((batch, seq, d), BF16),
        scratch_shapes=[pltpu.VMEM((groups, KEY_WIDTH, width), BF16),
                        pltpu.VMEM((groups, seq // SEL_KEY_TILE, dh, SEL_KEY_TILE), BF16),
                        pltpu.VMEM((groups, seq // WIN_KEY_TILE, dh, WIN_KEY_TILE), BF16),
                        pltpu.VMEM((groups, seq // SEL_KEY_TILE, SEL_KEY_TILE, width), F32),
                        pltpu.VMEM((groups, n_win_tiles, WIN_KEY_TILE, width), F32),
                        pltpu.VMEM((groups, dh, width), F32),
                        pltpu.VMEM((groups, 8, width), F32)],
        compiler_params=_params("parallel", "arbitrary"),
        name="nsa_attention",
    )(q, kc, vct, keys, vals, gates)


def _key_features(pos, dh, n_sel):
    block, offset = pos // SEL_BLOCK, pos % SEL_BLOCK
    feat = np.zeros((pos.shape[0], KEY_WIDTH), np.float32)
    feat[:, dh:dh + 3] = block[:, None]
    feat[:, dh + 3:dh + 6] = offset[:, None]
    if n_sel:
        feat[:, KEY_WIDTH - n_sel:] = block[:, None] == np.arange(n_sel)[None, :]
    return feat


def _nsa_weights(w_in, d, kv, groups):
    dh = kv // groups
    part = lambda j: w_in[:, d + j * kv:d + (j + 1) * kv]
    pad_keys = lambda w: jnp.pad(w.reshape(-1, groups, dh), ((0, 0), (0, 0), (0, KEY_WIDTH - dh))).reshape(-1, groups * KEY_WIDTH)
    gate_w = w_in[:, d + 6 * kv:]
    gate_w = jnp.pad(gate_w, ((0, 0), (0, -gate_w.shape[1] % 128)))
    cols = [w_in[:, :d], part(0), part(1), pad_keys(part(2)), pad_keys(part(4)), part(3), part(5), gate_w]
    return jnp.concatenate(cols, 1).astype(BF16)


def _nsa_mixer(x, mod, w_in, pe_k, pe_v, k1, k2, v1, v2, batch, seq):
    d = x.shape[1]
    groups = NSA_GROUPS
    dh = d // NSA_HEADS
    kv = groups * dh
    n_sel = seq // SEL_BLOCK
    positions = np.arange(seq)
    feat = jnp.asarray(np.concatenate([_key_features(positions, dh, n_sel), _key_features(positions, dh, 0)], 1), BF16)
    q, kcv, keys, vals, gates = _nsa_proj(x, mod, 3, _nsa_weights(w_in, d, kv, groups), feat, seq, d, kv,
                                          dh ** -0.5 * LOG2_E)
    kc, vc = _compress(kcv, pe_k, pe_v, k1, k2, v1, v2, groups, batch)
    cmp_end = np.arange(seq // CMP_STRIDE) * CMP_STRIDE + CMP_BLOCK - 1
    cmp_feat = jnp.asarray(_key_features(cmp_end, dh, 0)[:, dh:], BF16)
    kc = jnp.concatenate([kc, jnp.broadcast_to(cmp_feat, kc.shape[:2] + cmp_feat.shape)], -1)
    n_qb = seq // Q_BLOCK
    o = _nsa_attention(q.reshape(batch, n_qb, d, Q_BLOCK), kc, vc.transpose(0, 1, 3, 2), keys.reshape(batch, seq, -1),
                       vals.reshape(batch, seq, -1), gates.reshape(batch, n_qb, -1, Q_BLOCK), groups)
    return o.reshape(batch * seq, d)


def kernel(x, c, ada_w, ada_b, ln_g, ln_b, ffn_w_in, ffn_w_out, ret_w_in, ret_w_out, nsa_w_in, nsa_w_out,
           nsa_pe_k, nsa_pe_v, nsa_ck_w1, nsa_ck_w2, nsa_cv_w1, nsa_cv_w2):
    batch, seq, d = x.shape
    depth = ada_w.shape[0]
    assert depth == DEPTH
    mod_all = _ada_mod(c, ada_w, ada_b).reshape(depth, batch, 9, d)
    xf = x.reshape(batch * seq, d)
    dk = d // RET_HEADS
    dv = 2 * dk
    ffn_wi, ffn_wo = ffn_w_in.astype(BF16), ffn_w_out.astype(BF16)
    ret_wi, ret_wo, nsa_wo = ret_w_in.astype(BF16), ret_w_out.astype(BF16), nsa_w_out.astype(BF16)
    for i in range(depth):
        mod = mod_all[i]
        ln = lambda k: (ln_g[i, k].reshape(1, d), ln_b[i, k].reshape(1, d))
        xf = _ffn(xf, mod, 0, (ffn_wi, (i, 0)), (ffn_wo, (i, 0)), *ln(0), seq)
        j = i // N_MIXERS
        if i % N_MIXERS == 0:
            proj = _proj(xf, mod, 3, (ret_wi, (j,)), seq, 2 * MXU_WIDTH)
            y = _retention_core(proj.reshape(batch, seq, -1), batch, seq, dk, dv).reshape(batch * seq, -1)
            w_mix = (ret_wo, (j,))
        else:
            y = _nsa_mixer(xf, mod, nsa_w_in[j], nsa_pe_k[j].reshape(1, -1), nsa_pe_v[j].reshape(1, -1),
                           nsa_ck_w1[j].astype(BF16), nsa_ck_w2[j].astype(BF16),
                           nsa_cv_w1[j].astype(BF16), nsa_cv_w2[j].astype(BF16), batch, seq)
            w_mix = (nsa_wo, (j,))
        xf = _mixer_out_ffn(y, xf, mod, w_mix, ln(1), (ffn_wi, (i, 1)), (ffn_wo, (i, 1)), ln(2), seq)
    return xf.reshape(batch, seq, d)
```

```python
import functools

import numpy as np
import jax
import jax.numpy as jnp
from jax import lax
from jax.experimental import pallas as pl
from jax.experimental.pallas import tpu as pltpu

F32 = jnp.float32
BF16 = jnp.bfloat16

DEPTH = 4
N_MIXERS = 2
RET_HEADS = 4
NSA_HEADS = 16
NSA_GROUPS = 4
NSA_HPG = NSA_HEADS // NSA_GROUPS
CMP_BLOCK = 32
CMP_STRIDE = 16
SEL_BLOCK = 64
SEL_TOP = 16
WINDOW = 512
Q_BLOCK = 128
FORCED_SCORE = 1e4
INVALID_SCORE = -1.0
FFN_RES = 0.5
DN_ALPHA = (2 * DEPTH) ** 0.25
LN_EPS = 1e-5
GN_EPS = 1e-6
NEG_INF = -1e30

V7X_VMEM_LIMIT_BYTES = 56 * 1024 * 1024
MXU_WIDTH = 256
TOKEN_TILE = 1024
FUSED_TOKEN_TILE = 512
NORM_ROW_BLOCKS = 8
RET_CHUNK = 256
RET_HEADS_PER_STEP = 2
SEL_KEY_TILE = 256
WIN_KEY_TILE = 128
Q_BLOCKS_PER_STEP = 4
TILE_UNROLL = 4
KEY_WIDTH = 128
POS_ROWS = 16
LOG2_E = 1.4426950408889634


def _dot(a, b):
    return jnp.dot(a, b, preferred_element_type=F32)


def _sigmoid(x):
    return 1.0 / (1.0 + jnp.exp(-x))


def _layer_norm(z, g, b):
    mu = jnp.mean(z, -1, keepdims=True)
    zc = z - mu
    var = jnp.mean(zc * zc, -1, keepdims=True)
    return zc * lax.rsqrt(var + LN_EPS) * g + b


def _params(*sem):
    return pltpu.CompilerParams(dimension_semantics=sem, vmem_limit_bytes=V7X_VMEM_LIMIT_BYTES)


def _resident(shape):
    return pl.BlockSpec(shape, lambda *_: (0,) * len(shape), pipeline_mode=pl.Buffered(1))


def _resident_slice(stacked, index):
    rest = stacked.shape[len(index):]
    return pl.BlockSpec((None,) * len(index) + rest, lambda *_: tuple(index) + (0,) * len(rest),
                        pipeline_mode=pl.Buffered(1))


def _ada_kernel(c_ref, w_ref, b_ref, o_ref):
    c = c_ref[...]
    c_act = (c * _sigmoid(c)).astype(BF16)
    o_ref[0] = _dot(c_act, w_ref[0].astype(BF16)) + b_ref[0]


def _ada_mod(c, ada_w, ada_b):
    depth, d, n = ada_w.shape
    b = c.shape[0]
    tn = n // 8
    return pl.pallas_call(
        _ada_kernel,
        grid=(depth, n // tn),
        in_specs=[
            pl.BlockSpec((b, d), lambda l, j: (0, 0)),
            pl.BlockSpec((1, d, tn), lambda l, j: (l, 0, j)),
            pl.BlockSpec((1, 1, tn), lambda l, j: (l, 0, j)),
        ],
        out_specs=pl.BlockSpec((1, b, tn), lambda l, j: (l, 0, j)),
        out_shape=jax.ShapeDtypeStruct((depth, b, n), F32),
        compiler_params=_params("parallel", "parallel"),
        name="ada_mod",
    )(c, ada_w, ada_b.reshape(depth, 1, n))


def _swiglu_residual(x, mod_ref, row0, wi_ref, wo_ref, h_ref, act_ref, n_chunks, between_chunks):
    shift = mod_ref[0, row0:row0 + 1, :]
    scale = mod_ref[0, row0 + 1:row0 + 2, :]
    gate = mod_ref[0, row0 + 2:row0 + 3, :]
    h_ref[...] = (x * (1.0 + scale) + shift).astype(BF16)
    f = wo_ref.shape[0]
    fc = f // n_chunks
    for j in range(n_chunks):
        h = h_ref[...]
        a = _dot(h, wi_ref[:, j * fc:(j + 1) * fc])
        u = _dot(h, wi_ref[:, f + j * fc:f + (j + 1) * fc])
        act_ref[:, j * fc:(j + 1) * fc] = (a * _sigmoid(a) * u).astype(BF16)
        between_chunks(j)
    y = _dot(act_ref[...], wo_ref[...])
    return DN_ALPHA * x + FFN_RES * (1.0 + gate) * y


def _deferred_norm(z_ref, lng_ref, lnb_ref, o_ref, h_ref):
    rows_per_call = z_ref.shape[0] // NORM_ROW_BLOCKS

    def emit(j):
        if j < NORM_ROW_BLOCKS:
            rows = slice(j * rows_per_call, (j + 1) * rows_per_call)
            normed = _layer_norm(z_ref[rows, :], lng_ref[...], lnb_ref[...])
            o_ref[rows, :] = normed
            folded = jnp.sum(normed.reshape(rows_per_call // 8, 8, normed.shape[1]), 0)
            folded = sum(folded[:, c:c + 128] for c in range(0, normed.shape[1], 128))
            zero = lax.shift_right_logical(pltpu.bitcast(folded, jnp.uint32), jnp.uint32(32))
            h_ref[0:8, 0:128] = h_ref[0:8, 0:128] + pltpu.bitcast(zero, F32).astype(BF16)
    return emit


def _ffn_kernel(x_ref, mod_ref, wi_ref, wo_ref, lng_ref, lnb_ref, o_ref, h_ref, act_ref, z_ref, *, row0, n_chunks, n_tiles):
    i = pl.program_id(0)

    @pl.when(i == 0)
    def _():
        z_ref[...] = jnp.zeros_like(z_ref)

    @pl.when(i < n_tiles)
    def _():
        emit = _deferred_norm(z_ref, lng_ref, lnb_ref, o_ref, h_ref)
        z_ref[...] = _swiglu_residual(x_ref[...], mod_ref, row0, wi_ref, wo_ref, h_ref, act_ref, n_chunks, emit)

    @pl.when(i == n_tiles)
    def _():
        o_ref[...] = _layer_norm(z_ref[...], lng_ref[...], lnb_ref[...])


def _mixer_out_ffn_kernel(y_ref, x_ref, mod_ref, wm_ref, lng1_ref, lnb1_ref, wi_ref, wo_ref, lng2_ref, lnb2_ref,
                          o_ref, h_ref, act_ref, z_ref, *, n_chunks, n_tiles):
    i = pl.program_id(0)

    @pl.when(i == 0)
    def _():
        z_ref[...] = jnp.zeros_like(z_ref)

    @pl.when(i < n_tiles)
    def _():
        gate = mod_ref[0, 5:6, :]
        z1 = DN_ALPHA * x_ref[...] + (1.0 + gate) * _dot(y_ref[...], wm_ref[...])
        x1 = _layer_norm(z1, lng1_ref[...], lnb1_ref[...])
        emit = _deferred_norm(z_ref, lng2_ref, lnb2_ref, o_ref, h_ref)
        z_ref[...] = _swiglu_residual(x1, mod_ref, 6, wi_ref, wo_ref, h_ref, act_ref, n_chunks, emit)

    @pl.when(i == n_tiles)
    def _():
        o_ref[...] = _layer_norm(z_ref[...], lng2_ref[...], lnb2_ref[...])


def _ffn(x, mod, row0, w_in, w_out, ln_g, ln_b, seq):
    n_tok, d = x.shape
    f = w_out[0].shape[-2]
    tm = min(TOKEN_TILE, seq)
    tiles_per_seq = seq // tm
    n_tiles = n_tok // tm
    cur = lambda i: jnp.minimum(i, n_tiles - 1)
    return pl.pallas_call(
        functools.partial(_ffn_kernel, row0=row0, n_chunks=f // MXU_WIDTH, n_tiles=n_tiles),
        grid=(n_tiles + 1,),
        in_specs=[
            pl.BlockSpec((tm, d), lambda i: (cur(i), 0)),
            pl.BlockSpec((1, 9, d), lambda i: (cur(i) // tiles_per_seq, 0, 0)),
            _resident_slice(*w_in),
            _resident_slice(*w_out),
            _resident((1, d)),
            _resident((1, d)),
        ],
        out_specs=pl.BlockSpec((tm, d), lambda i: (jnp.maximum(i - 1, 0), 0)),
        out_shape=jax.ShapeDtypeStruct((n_tok, d), F32),
        scratch_shapes=[pltpu.VMEM((tm, d), BF16), pltpu.VMEM((tm, f), BF16), pltpu.VMEM((tm, d), F32)],
        compiler_params=_params("arbitrary"),
        name="ffn",
    )(x, mod, w_in[0], w_out[0], ln_g, ln_b)


def _mixer_out_ffn(y, x, mod, w_mix, ln1, w_in, w_out, ln2, seq):
    n_tok, d = x.shape
    k = y.shape[1]
    f = w_out[0].shape[-2]
    tm = min(FUSED_TOKEN_TILE, seq)
    tiles_per_seq = seq // tm
    n_tiles = n_tok // tm
    cur = lambda i: jnp.minimum(i, n_tiles - 1)
    return pl.pallas_call(
        functools.partial(_mixer_out_ffn_kernel, n_chunks=f // MXU_WIDTH, n_tiles=n_tiles),
        grid=(n_tiles + 1,),
        in_specs=[
            pl.BlockSpec((tm, k), lambda i: (cur(i), 0)),
            pl.BlockSpec((tm, d), lambda i: (cur(i), 0)),
            pl.BlockSpec((1, 9, d), lambda i: (cur(i) // tiles_per_seq, 0, 0)),
            _resident_slice(*w_mix), _resident((1, d)), _resident((1, d)),
            _resident_slice(*w_in), _resident_slice(*w_out), _resident((1, d)), _resident((1, d)),
        ],
        out_specs=pl.BlockSpec((tm, d), lambda i: (jnp.maximum(i - 1, 0), 0)),
        out_shape=jax.ShapeDtypeStruct((n_tok, d), F32),
        scratch_shapes=[pltpu.VMEM((tm, d), BF16), pltpu.VMEM((tm, f), BF16), pltpu.VMEM((tm, d), F32)],
        compiler_params=_params("arbitrary"),
        name="mixer_out_ffn",
    )(y, x, mod, w_mix[0], *ln1, w_in[0], w_out[0], *ln2)


def _proj_kernel(x_ref, mod_ref, w_ref, o_ref, *, row0, tn):
    x = x_ref[...]
    shift = mod_ref[0, row0:row0 + 1, :]
    scale = mod_ref[0, row0 + 1:row0 + 2, :]
    h = (x * (1.0 + scale) + shift).astype(BF16)
    for j in range(o_ref.shape[1] // tn):
        o_ref[:, j * tn:(j + 1) * tn] = _dot(h, w_ref[:, j * tn:(j + 1) * tn]).astype(o_ref.dtype)


def _proj(x, mod, row0, w, seq, tn):
    n_tok, d = x.shape
    n = w[0].shape[-1]
    tm = min(TOKEN_TILE, seq)
    tiles_per_seq = seq // tm
    return pl.pallas_call(
        functools.partial(_proj_kernel, row0=row0, tn=tn),
        grid=(n_tok // tm,),
        in_specs=[
            pl.BlockSpec((tm, d), lambda i: (i, 0)),
            pl.BlockSpec((1, 9, d), lambda i: (i // tiles_per_seq, 0, 0)),
            _resident_slice(*w),
        ],
        out_specs=pl.BlockSpec((tm, n), lambda i: (i, 0)),
        out_shape=jax.ShapeDtypeStruct((n_tok, n), BF16),
        compiler_params=_params("parallel"),
        name="mixer_in_proj",
    )(x, mod, w[0])


def _nsa_proj_kernel(x_ref, mod_ref, w_ref, feat_ref, q_ref, kcv_ref, keys_ref, vst_ref, vwt_ref, gates_ref, *, row0, q_scale):
    x = x_ref[...]
    shift = mod_ref[0, row0:row0 + 1, :]
    scale = mod_ref[0, row0 + 1:row0 + 2, :]
    h = (x * (1.0 + scale) + shift).astype(BF16)
    half = keys_ref.shape[1] // 2
    lane_tile = kcv_ref.shape[2]
    col = 0
    for ref in (q_ref, kcv_ref, keys_ref, vst_ref, vwt_ref, gates_ref):
        n = kcv_ref.shape[0] * lane_tile if ref is kcv_ref else ref.shape[1]
        step = min(n, MXU_WIDTH)
        for j in range(0, n, step):
            y = _dot(h, w_ref[:, col + j:col + j + step])
            if ref is q_ref:
                y = y * q_scale
                for r in range(y.shape[0] // Q_BLOCK):
                    q_ref[r, j:j + step, :] = y[r * Q_BLOCK:(r + 1) * Q_BLOCK, :].T.astype(q_ref.dtype)
                continue
            elif ref is keys_ref:
                f = feat_ref[:, 0:KEY_WIDTH] if j < half else feat_ref[:, KEY_WIDTH:2 * KEY_WIDTH]
                y = y + jnp.concatenate([f] * (step // KEY_WIDTH), 1).astype(F32)
            elif ref is gates_ref:
                y = _sigmoid(y)
                for r in range(y.shape[0] // Q_BLOCK):
                    gates_ref[r] = y[r * Q_BLOCK:(r + 1) * Q_BLOCK, :].T
                continue
            elif ref is vst_ref or ref is vwt_ref:
                tile = ref.shape[2]
                for r in range(y.shape[0] // tile):
                    ref[r, j:j + step, :] = y[r * tile:(r + 1) * tile, :].T.astype(ref.dtype)
                continue
            if ref is kcv_ref:
                for c in range(step // lane_tile):
                    kcv_ref[(j + c * lane_tile) // lane_tile] = y[:, c * lane_tile:(c + 1) * lane_tile]
            else:
                ref[:, j:j + step] = y.astype(ref.dtype)
        col += n


def _nsa_proj(x, mod, row0, w, feat, seq, d, kv, q_scale):
    n_tok = x.shape[0]
    groups = NSA_GROUPS
    tm = min(TOKEN_TILE, seq)
    tiles_per_seq = seq // tm
    lane = 128
    widths = (d, 2 * kv, 2 * groups * KEY_WIDTH, kv, kv, w.shape[1] - d - 4 * kv - 2 * groups * KEY_WIDTH)
    dtypes = (BF16, F32, BF16, BF16, BF16, F32)
    transposed = lambda n, tile, dt: (pl.BlockSpec((tm // tile, n, tile), lambda i: (i, 0, 0)),
                                      jax.ShapeDtypeStruct((n_tok // tile, n, tile), dt))
    token_major = lambda n, dt: (pl.BlockSpec((tm, n), lambda i: (i, 0)), jax.ShapeDtypeStruct((n_tok, n), dt))
    outs = [transposed(widths[0], Q_BLOCK, dtypes[0]),
            (pl.BlockSpec((widths[1] // lane, tm, lane), lambda i: (0, i, 0)),
             jax.ShapeDtypeStruct((widths[1] // lane, n_tok, lane), dtypes[1])),
            token_major(widths[2], dtypes[2]),
            transposed(widths[3], SEL_KEY_TILE, dtypes[3]),
            transposed(widths[4], WIN_KEY_TILE, dtypes[4]),
            transposed(widths[5], Q_BLOCK, dtypes[5])]
    return pl.pallas_call(
        functools.partial(_nsa_proj_kernel, row0=row0, q_scale=q_scale),
        grid=(n_tok // tm,),
        in_specs=[
            pl.BlockSpec((tm, d), lambda i: (i, 0)),
            pl.BlockSpec((1, 9, d), lambda i: (i // tiles_per_seq, 0, 0)),
            _resident(w.shape),
            pl.BlockSpec((tm, 2 * KEY_WIDTH), lambda i: (i % tiles_per_seq, 0)),
        ],
        out_specs=[spec for spec, _ in outs],
        out_shape=[sds for _, sds in outs],
        compiler_params=_params("parallel"),
        name="nsa_in_proj",
    )(x, mod, w, feat)


def _ret_kernel(q_ref, k_ref, v_ref, g_ref, o_ref, state_ref, *, chunk, n_chunks, k_scale):
    heads = state_ref.shape[0]
    dk, dv = state_ref.shape[1], state_ref.shape[2]
    row = lax.broadcasted_iota(jnp.int32, (chunk, chunk), 0)
    col = lax.broadcasted_iota(jnp.int32, (chunk, chunk), 1)
    diff = (row - col).astype(F32)
    pos = lax.broadcasted_iota(jnp.int32, (chunk, 1), 0).astype(F32)
    decays = []
    for hh in range(heads):
        head = (pl.program_id(1) * heads + hh).astype(F32)
        log_g = jnp.log(1.0 - jnp.exp2(-5.0 - (jnp.zeros((1, 1), F32) + head)))
        decays.append((jnp.where(diff >= 0, jnp.exp(log_g * jnp.maximum(diff, 0.0)), 0.0),
                       jnp.exp(log_g * (pos + 1.0)),
                       jnp.exp(log_g * (chunk - 1.0 - pos)),
                       jnp.exp(log_g * float(chunk))))
    state_ref[...] = jnp.zeros_like(state_ref)

    def step(c, carry):
        rows = pl.ds(pl.multiple_of(c * chunk, chunk), chunk)
        for hh in range(heads):
            decay_intra, decay_q, decay_k, decay_state = decays[hh]
            q = q_ref[0, rows, hh * dk:(hh + 1) * dk]
            k = k_ref[0, rows, hh * dk:(hh + 1) * dk] * k_scale
            v = v_ref[0, rows, hh * dv:(hh + 1) * dv]
            scores = lax.dot_general(q, k, (((1,), (1,)), ((), ())), preferred_element_type=F32) * decay_intra
            state = state_ref[hh]
            o = _dot(scores.astype(BF16), v) + _dot((q.astype(F32) * decay_q).astype(BF16), state.astype(BF16))
            k_dec_t = (k.astype(F32) * decay_k).T.astype(BF16)
            state_ref[hh] = decay_state * state + _dot(k_dec_t, v)
            mu = jnp.mean(o, -1, keepdims=True)
            oc = o - mu
            var = jnp.mean(oc * oc, -1, keepdims=True)
            o = oc * lax.rsqrt(var + GN_EPS)
            g = g_ref[0, rows, hh * dv:(hh + 1) * dv].astype(F32)
            o_ref[0, rows, hh * dv:(hh + 1) * dv] = (o * (g * _sigmoid(g))).astype(o_ref.dtype)
        return carry

    lax.fori_loop(0, n_chunks, step, 0)


def _retention_core(proj, batch, seq, dk, dv):
    heads = RET_HEADS
    hps = RET_HEADS_PER_STEP
    chunk = min(RET_CHUNK, seq)
    k_blk0 = heads * dk // (hps * dk)
    v_blk0 = 2 * heads * dk // (hps * dv)
    g_blk0 = (2 * heads * dk + heads * dv) // (hps * dv)
    return pl.pallas_call(
        functools.partial(_ret_kernel, chunk=chunk, n_chunks=seq // chunk, k_scale=dk ** -0.5),
        grid=(batch, heads // hps),
        in_specs=[
            pl.BlockSpec((1, seq, hps * dk), lambda b, h: (b, 0, h)),
            pl.BlockSpec((1, seq, hps * dk), lambda b, h: (b, 0, k_blk0 + h)),
            pl.BlockSpec((1, seq, hps * dv), lambda b, h: (b, 0, v_blk0 + h)),
            pl.BlockSpec((1, seq, hps * dv), lambda b, h: (b, 0, g_blk0 + h)),
        ],
        out_specs=pl.BlockSpec((1, seq, hps * dv), lambda b, h: (b, 0, h)),
        out_shape=jax.ShapeDtypeStruct((batch, seq, heads * dv), BF16),
        scratch_shapes=[pltpu.VMEM((hps, dk, dv), F32)],
        compiler_params=_params("parallel", "parallel"),
        name="retention_core",
    )(proj, proj, proj, proj)


def _gelu_tanh(x):
    return 0.5 * x * (1.0 + jnp.tanh(0.7978845608028654 * (x + 0.044715 * (x * x * x))))


def _compress_kernel(x_ref, pek_ref, pev_ref, k1_ref, k2_ref, v1_ref, v2_ref, ok_ref, ov_ref, y_ref):
    groups, n_rows, half = y_ref.shape
    dh = half // CMP_STRIDE
    kv = groups * dh
    per_tile = x_ref.shape[2] // dh

    def mlp(lane0, pe_ref, w1_ref, w2_ref, out_ref):
        for r in range(CMP_STRIDE):
            for c in range(groups // per_tile):
                piece = x_ref[lane0 // x_ref.shape[2] + c, pl.ds(r, n_rows, stride=CMP_STRIDE), :]
                for k in range(per_tile):
                    y_ref[c * per_tile + k, :, r * dh:(r + 1) * dh] = piece[:, k * dh:(k + 1) * dh].astype(BF16)
        bias = _dot(pe_ref[...].astype(BF16), w1_ref[...])
        for g in range(groups):
            y = y_ref[g]
            first = _dot(y, w1_ref[0:half, :])
            second = _dot(y, w1_ref[half:2 * half, :])
            hidden = first + pltpu.roll(second, n_rows - 1, 0) + bias
            out_ref[0, g] = _dot(_gelu_tanh(hidden).astype(BF16), w2_ref[...]).astype(out_ref.dtype)

    mlp(0, pek_ref, k1_ref, k2_ref, ok_ref)
    mlp(kv, pev_ref, v1_ref, v2_ref, ov_ref)


def _compress(kcv, pe_k, pe_v, k1, k2, v1, v2, groups, batch):
    lane_tiles, n_tok, lane = kcv.shape
    seq = n_tok // batch
    rows = seq // CMP_STRIDE
    hidden = k1.shape[1]
    dh = k2.shape[1]
    out_blk = pl.BlockSpec((1, groups, rows, dh), lambda b: (b, 0, 0, 0))
    out_sds = jax.ShapeDtypeStruct((batch, groups, rows, dh), BF16)
    return pl.pallas_call(
        _compress_kernel,
        grid=(batch,),
        in_specs=[pl.BlockSpec((lane_tiles, seq, lane), lambda b: (0, b, 0)),
                  _resident(pe_k.shape), _resident(pe_v.shape),
                  _resident(k1.shape), _resident(k2.shape), _resident(v1.shape), _resident(v2.shape)],
        out_specs=[out_blk, out_blk],
        out_shape=[out_sds, out_sds],
        scratch_shapes=[pltpu.VMEM((groups, rows, CMP_STRIDE * dh), BF16)],
        compiler_params=_params("parallel"),
        name="nsa_compress",
    )(kcv, pe_k, pe_v, k1, k2, v1, v2)


def _nsa_attn_kernel(q_ref, kc_ref, vct_ref, keys_ref, vst_ref, vwt_ref, gates_ref, o_ref,
                     qa_ref, s_sel_ref, s_win_ref, acc_ref, l8_ref, *, n_sel, top):
    step = pl.program_id(1)

    def one_q_block(sub, carry):
        q_rows = pl.ds(pl.multiple_of(sub * Q_BLOCK, Q_BLOCK), Q_BLOCK)
        _nsa_q_block(step * Q_BLOCKS_PER_STEP + sub, sub, q_rows, q_ref, kc_ref, vct_ref, keys_ref, gates_ref, o_ref,
                     qa_ref, vst_ref, vwt_ref, s_sel_ref, s_win_ref, acc_ref, l8_ref, n_sel, top)
        return carry

    lax.fori_loop(0, Q_BLOCKS_PER_STEP, one_q_block, 0)


def _nsa_q_block(qb, sub, q_rows, q_ref, kc_ref, vct_ref, keys_ref, gates_ref, o_ref,
                 qa_ref, vst_ref, vwt_ref, s_sel_ref, s_win_ref, acc_ref, l8_ref, n_sel, top):
    q0 = qb * Q_BLOCK
    groups, dh, width = qa_ref.shape[0], vct_ref.shape[2], qa_ref.shape[2]
    sel_row0 = KEY_WIDTH - n_sel
    all_groups = range(groups)
    pair = 2 * dh
    assert pair == Q_BLOCK == KEY_WIDTH

    lane = lax.broadcasted_iota(jnp.int32, (1, width), 1)
    t_row = q0 + lane % Q_BLOCK
    piece_row = lax.broadcasted_iota(jnp.int32, (POS_ROWS, width), 0)

    for g in all_groups:
        head = (g * NSA_HPG + lane // Q_BLOCK).astype(F32)
        slope = jnp.exp2(-8.0 * (head + 1.0) / NSA_HEADS) * LOG2_E
        hi = slope.astype(BF16).astype(F32)
        rest = slope - hi
        mid = rest.astype(BF16).astype(F32)
        lo = (rest - mid).astype(BF16).astype(F32)
        feat = jnp.zeros((POS_ROWS, width), F32)
        for idx, piece in enumerate((SEL_BLOCK * hi, SEL_BLOCK * mid, SEL_BLOCK * lo, hi, mid, lo)):
            feat = jnp.where(piece_row == idx, piece, feat)
        for h in range(NSA_HPG):
            lanes0 = (g * NSA_HPG + h) * dh
            qa_ref[g, 0:dh, h * Q_BLOCK:(h + 1) * Q_BLOCK] = q_ref[0, sub, lanes0:lanes0 + dh, :]
        qa_ref[g, dh:dh + POS_ROWS, :] = feat.astype(BF16)
        qa_ref[g, dh + POS_ROWS:KEY_WIDTH, :] = jnp.zeros((KEY_WIDTH - dh - POS_ROWS, width), BF16)

    n_cmp_pad = kc_ref.shape[2]
    blk = lax.broadcasted_iota(jnp.int32, (n_cmp_pad, width), 0)
    mask_c = blk * CMP_STRIDE + (CMP_BLOCK - 1) <= t_row
    any_visible = t_row >= CMP_BLOCK - 1
    scores_c = [_dot(kc_ref[0, g], qa_ref[g]) for g in all_groups]
    p_cmp = []
    for g in all_groups:
        s = jnp.where(mask_c, scores_c[g], NEG_INF)
        e = jnp.exp2(s - jnp.max(s, 0, keepdims=True))
        p_cmp.append(e * jnp.where(any_visible, 1.0 / jnp.sum(e, 0, keepdims=True), 0.0))
    o_cmp = [_dot(vct_ref[0, g], p_cmp[g].astype(BF16)) for g in all_groups]

    sel_i = lax.broadcasted_iota(jnp.int32, (n_sel, n_cmp_pad), 0) * SEL_BLOCK
    cmp_i = lax.broadcasted_iota(jnp.int32, (n_sel, n_cmp_pad), 1) * CMP_STRIDE
    overlap = jnp.maximum(jnp.minimum(cmp_i + CMP_BLOCK, sel_i + SEL_BLOCK) - jnp.maximum(cmp_i, sel_i), 0)
    overlap = (overlap.astype(F32) * (1.0 / CMP_BLOCK)).astype(BF16)
    j_idx = lax.broadcasted_iota(jnp.int32, (n_sel, Q_BLOCK), 0)
    t_sel = q0 + lax.broadcasted_iota(jnp.int32, (n_sel, Q_BLOCK), 1)
    cur = t_sel // SEL_BLOCK
    forced = (j_idx == 0) | (j_idx == cur) | (j_idx == cur - 1)
    valid = j_idx * SEL_BLOCK <= t_sel
    pieces = []
    for g in all_groups:
        p_sum = p_cmp[g][:, 0:Q_BLOCK]
        for h in range(1, NSA_HPG):
            p_sum = p_sum + p_cmp[g][:, h * Q_BLOCK:(h + 1) * Q_BLOCK]
        p_hi = p_sum.astype(BF16)
        rest = p_sum - p_hi.astype(F32)
        p_mid = rest.astype(BF16)
        pieces.append((p_hi, p_mid, (rest - p_mid.astype(F32)).astype(BF16)))
    imps = [_dot(overlap, hi) + _dot(overlap, mid) + _dot(overlap, lo) for hi, mid, lo in pieces]
    for g in all_groups:
        score = jnp.where(valid, jnp.where(forced, FORCED_SCORE, imps[g]), INVALID_SCORE)
        tiles = [score[r:r + 8, :] for r in range(0, n_sel, 8)]
        ranks = [jnp.zeros((8, Q_BLOCK), F32) for _ in tiles]
        for jp in range(n_sel):
            other = jnp.broadcast_to(score[jp:jp + 1, :], (8, Q_BLOCK))
            for v, tile_scores in enumerate(tiles):
                if jp < 8 * v:
                    ahead = other >= tile_scores
                elif jp >= 8 * v + 7:
                    ahead = other > tile_scores
                else:
                    ahead = (other > tile_scores) | ((other == tile_scores) & (j_idx[0:8, :] > jp - 8 * v))
                ranks[v] = ranks[v] + jnp.where(ahead, 1.0, 0.0)
        rank = jnp.concatenate(ranks, 0)
        block_bias = jnp.where(rank < top, 0.0, NEG_INF)
        qa_ref[g, sel_row0:KEY_WIDTH, :] = jnp.concatenate([block_bias] * NSA_HPG, 1).astype(BF16)

    def row_max(sc):
        return jnp.max(sc.reshape(sc.shape[0] // 8, 8, width), 0)

    def for_tiles(n, body, init):
        carry, start, width_ = init, 0, TILE_UNROLL
        while width_ >= 1:
            def several(j, c, start=start, width_=width_):
                for u in range(width_):
                    c = body(start + width_ * j + u, c)
                return c
            trips = (n - start) // width_
            carry = lax.fori_loop(0, trips, several, carry)
            start = start + trips * width_
            width_ //= 2
        return carry

    def softmax_values(s_ref, vt_ref, first_tile, n_tiles, m8):
        m = [jnp.max(m8[g], 0, keepdims=True) for g in all_groups]
        acc_ref[...] = jnp.zeros_like(acc_ref)
        l8_ref[...] = jnp.zeros_like(l8_ref)

        def step(i, carry):
            pes = [jnp.exp2(s_ref[g, i] - m[g]) for g in all_groups]
            for g in all_groups:
                l8_ref[g] += jnp.sum(pes[g].reshape(pes[g].shape[0] // 8, 8, width), 0)
                acc_ref[g] += _dot(vt_ref[0, first_tile + i, g * dh:(g + 1) * dh, :], pes[g].astype(BF16))
            return carry

        for_tiles(n_tiles, step, 0)
        return [(acc_ref[g], 1.0 / jnp.sum(l8_ref[g], 0, keepdims=True)) for g in all_groups]

    m8_init = tuple(jnp.full((8, width), NEG_INF, F32) for _ in all_groups)

    tile = SEL_KEY_TILE

    def sel_scores(g, kt):
        rows = pl.ds(pl.multiple_of(kt * tile, tile), tile)
        return _dot(keys_ref[0, rows, g * KEY_WIDTH:(g + 1) * KEY_WIDTH], qa_ref[g])

    def sel_pass(kt, m8):
        out = []
        for g in all_groups:
            sc = sel_scores(g, kt)
            s_sel_ref[g, kt] = sc
            out.append(jnp.maximum(m8[g], row_max(sc)))
        return tuple(out)

    last = (q0 + Q_BLOCK - 1) // tile
    m8 = list(for_tiles(last, sel_pass, m8_init))
    visible = last * tile + lax.broadcasted_iota(jnp.int32, (tile, width), 0) <= t_row
    last_scores = [sel_scores(g, last) for g in all_groups]
    for g in all_groups:
        sc = jnp.where(visible, last_scores[g], NEG_INF)
        s_sel_ref[g, last] = sc
        m8[g] = jnp.maximum(m8[g], row_max(sc))
    o_sel = softmax_values(s_sel_ref, vst_ref, 0, last + 1, m8)

    tile = WIN_KEY_TILE
    first = jnp.maximum(q0 - WINDOW, 0) // tile
    n_win = (q0 + Q_BLOCK - 1) // tile - first + 1
    key_row = lax.broadcasted_iota(jnp.int32, (tile, width), 0)

    def win_pass(i, m8):
        kt = first + i
        rows = pl.ds(pl.multiple_of(kt * tile, tile), tile)
        dist = (t_row - kt * tile) - key_row
        in_window = (dist >= 0) & (dist < WINDOW)
        out = []
        for g in all_groups:
            k_tile = keys_ref[0, rows, (groups + g) * KEY_WIDTH:(groups + g + 1) * KEY_WIDTH]
            sc = jnp.where(in_window, _dot(k_tile, qa_ref[g]), NEG_INF)
            s_win_ref[g, i] = sc
            out.append(jnp.maximum(m8[g], row_max(sc)))
        return tuple(out)

    m8 = for_tiles(n_win, win_pass, m8_init)
    o_win = softmax_values(s_win_ref, vwt_ref, first, n_win, m8)

    gates_t = gates_ref[0, sub]
    for g in all_groups:
        per_head = []
        for h in range(NSA_HPG):
            lanes = slice(h * Q_BLOCK, (h + 1) * Q_BLOCK)
            row = (g * NSA_HPG + h) * 3
            (acc_sel, inv_sel), (acc_win, inv_win) = o_sel[g], o_win[g]
            per_head.append(gates_t[row:row + 1, :] * o_cmp[g][:, lanes]
                            + (gates_t[row + 1:row + 2, :] * inv_sel[:, lanes]) * acc_sel[:, lanes]
                            + (gates_t[row + 2:row + 3, :] * inv_win[:, lanes]) * acc_win[:, lanes])
        for p in range(NSA_HPG // 2):
            lanes0 = (g * NSA_HPG + 2 * p) * dh
            two_heads = jnp.concatenate(per_head[2 * p:2 * p + 2], 0)
            o_ref[0, q_rows, lanes0:lanes0 + pair] = two_heads.T.astype(o_ref.dtype)


def _nsa_attention(q, kc, vct, keys, vst, vwt, gates, groups):
    batch, n_qb, d, _ = q.shape
    seq = n_qb * Q_BLOCK
    dh = vct.shape[2]
    width = NSA_HPG * Q_BLOCK
    n_sel = seq // SEL_BLOCK
    top = min(SEL_TOP, n_sel)
    assert dh + POS_ROWS + n_sel <= KEY_WIDTH
    per_b = lambda shape: pl.BlockSpec((1,) + shape, lambda b, i: (b,) + (0,) * len(shape))
    per_q = lambda lanes: pl.BlockSpec((1, Q_BLOCKS_PER_STEP * Q_BLOCK, lanes), lambda b, i: (b, i, 0))
    per_q_t = lambda lanes: pl.BlockSpec((1, Q_BLOCKS_PER_STEP, lanes, Q_BLOCK), lambda b, i: (b, i, 0, 0))
    n_win_tiles = min(WINDOW + Q_BLOCK, seq) // WIN_KEY_TILE
    return pl.pallas_call(
        functools.partial(_nsa_attn_kernel, n_sel=n_sel, top=top),
        grid=(batch, n_qb // Q_BLOCKS_PER_STEP),
        in_specs=[
            per_q_t(d),
            per_b(kc.shape[1:]), per_b(vct.shape[1:]),
            per_b(keys.shape[1:]), per_b(vst.shape[1:]), per_b(vwt.shape[1:]),
            per_q_t(gates.shape[2]),
        ],
        out_specs=per_q(d),
        out_shape=jax.ShapeDtypeStruct((batch, seq, d), BF16),
        scratch_shapes=[pltpu.VMEM((groups, KEY_WIDTH, width), BF16),
                        pltpu.VMEM((groups, seq // SEL_KEY_TILE, SEL_KEY_TILE, width), F32),
                        pltpu.VMEM((groups, n_win_tiles, WIN_KEY_TILE, width), F32),
                        pltpu.VMEM((groups, dh, width), F32),
                        pltpu.VMEM((groups, 8, width), F32)],
        compiler_params=_params("parallel", "arbitrary"),
        name="nsa_attention",
    )(q, kc, vct, keys, vst, vwt, gates)


def _key_features(pos, dh, n_sel):
    block, offset = pos // SEL_BLOCK, pos % SEL_BLOCK
    feat = np.zeros((pos.shape[0], KEY_WIDTH), np.float32)
    feat[:, dh:dh + 3] = block[:, None]
    feat[:, dh + 3:dh + 6] = offset[:, None]
    if n_sel:
        feat[:, KEY_WIDTH - n_sel:] = block[:, None] == np.arange(n_sel)[None, :]
    return feat


def _nsa_weights(w_in, d, kv, groups):
    dh = kv // groups
    part = lambda j: w_in[:, d + j * kv:d + (j + 1) * kv]
    pad_keys = lambda w: jnp.pad(w.reshape(-1, groups, dh), ((0, 0), (0, 0), (0, KEY_WIDTH - dh))).reshape(-1, groups * KEY_WIDTH)
    gate_w = w_in[:, d + 6 * kv:]
    gate_w = jnp.pad(gate_w, ((0, 0), (0, -gate_w.shape[1] % 128)))
    cols = [w_in[:, :d], part(0), part(1), pad_keys(part(2)), pad_keys(part(4)), part(3), part(5), gate_w]
    return jnp.concatenate(cols, 1).astype(BF16)


def _nsa_mixer(x, mod, w_in, pe_k, pe_v, k1, k2, v1, v2, batch, seq):
    d = x.shape[1]
    groups = NSA_GROUPS
    dh = d // NSA_HEADS
    kv = groups * dh
    n_sel = seq // SEL_BLOCK
    positions = np.arange(seq)
    feat = jnp.asarray(np.concatenate([_key_features(positions, dh, n_sel), _key_features(positions, dh, 0)], 1), BF16)
    q, kcv, keys, vst, vwt, gates = _nsa_proj(x, mod, 3, _nsa_weights(w_in, d, kv, groups), feat, seq, d, kv,
                                          dh ** -0.5 * LOG2_E)
    kc, vc = _compress(kcv, pe_k, pe_v, k1, k2, v1, v2, groups, batch)
    cmp_end = np.arange(seq // CMP_STRIDE) * CMP_STRIDE + CMP_BLOCK - 1
    cmp_feat = jnp.asarray(_key_features(cmp_end, dh, 0)[:, dh:], BF16)
    kc = jnp.concatenate([kc, jnp.broadcast_to(cmp_feat, kc.shape[:2] + cmp_feat.shape)], -1)
    n_qb = seq // Q_BLOCK
    o = _nsa_attention(q.reshape(batch, n_qb, d, Q_BLOCK), kc, vc.transpose(0, 1, 3, 2), keys.reshape(batch, seq, -1),
                       vst.reshape(batch, seq // SEL_KEY_TILE, kv, SEL_KEY_TILE),
                       vwt.reshape(batch, seq // WIN_KEY_TILE, kv, WIN_KEY_TILE), gates.reshape(batch, n_qb, -1, Q_BLOCK), groups)
    return o.reshape(batch * seq, d)


def kernel(x, c, ada_w, ada_b, ln_g, ln_b, ffn_w_in, ffn_w_out, ret_w_in, ret_w_out, nsa_w_in, nsa_w_out,
           nsa_pe_k, nsa_pe_v, nsa_ck_w1, nsa_ck_w2, nsa_cv_w1, nsa_cv_w2):
    batch, seq, d = x.shape
    depth = ada_w.shape[0]
    assert depth == DEPTH
    mod_all = _ada_mod(c, ada_w, ada_b).reshape(depth, batch, 9, d)
    xf = x.reshape(batch * seq, d)
    dk = d // RET_HEADS
    dv = 2 * dk
    ffn_wi, ffn_wo = ffn_w_in.astype(BF16), ffn_w_out.astype(BF16)
    ret_wi, ret_wo, nsa_wo = ret_w_in.astype(BF16), ret_w_out.astype(BF16), nsa_w_out.astype(BF16)
    for i in range(depth):
        mod = mod_all[i]
        ln = lambda k: (ln_g[i, k].reshape(1, d), ln_b[i, k].reshape(1, d))
        xf = _ffn(xf, mod, 0, (ffn_wi, (i, 0)), (ffn_wo, (i, 0)), *ln(0), seq)
        j = i // N_MIXERS
        if i % N_MIXERS == 0:
            proj = _proj(xf, mod, 3, (ret_wi, (j,)), seq, 2 * MXU_WIDTH)
            y = _retention_core(proj.reshape(batch, seq, -1), batch, seq, dk, dv).reshape(batch * seq, -1)
            w_mix = (ret_wo, (j,))
        else:
            y = _nsa_mixer(xf, mod, nsa_w_in[j], nsa_pe_k[j].reshape(1, -1), nsa_pe_v[j].reshape(1, -1),
                           nsa_ck_w1[j].astype(BF16), nsa_ck_w2[j].astype(BF16),
                           nsa_cv_w1[j].astype(BF16), nsa_cv_w2[j].astype(BF16), batch, seq)
            w_mix = (nsa_wo, (j,))
        xf = _mixer_out_ffn(y, xf, mod, w_mix, ln(1), (ffn_wi, (i, 1)), (ffn_wo, (i, 1)), ln(2), seq)
    return xf.reshape(batch, seq, d)
```

```python
import functools

import numpy as np
import jax
import jax.numpy as jnp
from jax import lax
from jax.experimental import pallas as pl
from jax.experimental.pallas import tpu as pltpu

F32 = jnp.float32
BF16 = jnp.bfloat16

DEPTH = 4
N_MIXERS = 2
RET_HEADS = 4
NSA_HEADS = 16
NSA_GROUPS = 4
NSA_HPG = NSA_HEADS // NSA_GROUPS
CMP_BLOCK = 32
CMP_STRIDE = 16
SEL_BLOCK = 64
SEL_TOP = 16
WINDOW = 512
Q_BLOCK = 128
FORCED_SCORE = 1e4
INVALID_SCORE = -1.0
FFN_RES = 0.5
DN_ALPHA = (2 * DEPTH) ** 0.25
LN_EPS = 1e-5
GN_EPS = 1e-6
NEG_INF = -1e30

V7X_VMEM_LIMIT_BYTES = 56 * 1024 * 1024
MXU_WIDTH = 256
TOKEN_TILE = 1024
RET_CHUNK = 256
RET_HEADS_PER_STEP = 2
SEL_KEY_TILE = 256
WIN_KEY_TILE = 128
Q_BLOCKS_PER_STEP = 4
TILE_UNROLL = 4
KEY_WIDTH = 128
POS_ROWS = 16
LOG2_E = 1.4426950408889634


def _dot(a, b):
    return jnp.dot(a, b, preferred_element_type=F32)


def _sigmoid(x):
    return 1.0 / (1.0 + jnp.exp(-x))


def _layer_norm(z, g, b):
    mu = jnp.mean(z, -1, keepdims=True)
    zc = z - mu
    var = jnp.mean(zc * zc, -1, keepdims=True)
    return zc * lax.rsqrt(var + LN_EPS) * g + b


def _params(*sem):
    return pltpu.CompilerParams(dimension_semantics=sem, vmem_limit_bytes=V7X_VMEM_LIMIT_BYTES)


def _resident(shape):
    return pl.BlockSpec(shape, lambda *_: (0,) * len(shape), pipeline_mode=pl.Buffered(1))


def _resident_slice(stacked, index):
    rest = stacked.shape[len(index):]
    return pl.BlockSpec((None,) * len(index) + rest, lambda *_: tuple(index) + (0,) * len(rest),
                        pipeline_mode=pl.Buffered(1))


def _ada_kernel(c_ref, w_ref, b_ref, o_ref):
    c = c_ref[...]
    c_act = (c * _sigmoid(c)).astype(BF16)
    o_ref[0] = _dot(c_act, w_ref[0].astype(BF16)) + b_ref[0]


def _ada_mod(c, ada_w, ada_b):
    depth, d, n = ada_w.shape
    b = c.shape[0]
    tn = n // 8
    return pl.pallas_call(
        _ada_kernel,
        grid=(depth, n // tn),
        in_specs=[
            pl.BlockSpec((b, d), lambda l, j: (0, 0)),
            pl.BlockSpec((1, d, tn), lambda l, j: (l, 0, j)),
            pl.BlockSpec((1, 1, tn), lambda l, j: (l, 0, j)),
        ],
        out_specs=pl.BlockSpec((1, b, tn), lambda l, j: (l, 0, j)),
        out_shape=jax.ShapeDtypeStruct((depth, b, n), F32),
        compiler_params=_params("parallel", "parallel"),
        name="ada_mod",
    )(c, ada_w, ada_b.reshape(depth, 1, n))


def _swiglu_post_norm(x, mod_ref, row0, wi_ref, wo_ref, lng_ref, lnb_ref, h_ref, act_ref, n_chunks):
    shift = mod_ref[0, row0:row0 + 1, :]
    scale = mod_ref[0, row0 + 1:row0 + 2, :]
    gate = mod_ref[0, row0 + 2:row0 + 3, :]
    h_ref[...] = (x * (1.0 + scale) + shift).astype(BF16)
    f = wo_ref.shape[0]
    fc = f // n_chunks
    for j in range(n_chunks):
        h = h_ref[...]
        a = _dot(h, wi_ref[:, j * fc:(j + 1) * fc])
        u = _dot(h, wi_ref[:, f + j * fc:f + (j + 1) * fc])
        act_ref[:, j * fc:(j + 1) * fc] = (a * _sigmoid(a) * u).astype(BF16)
    y = _dot(act_ref[...], wo_ref[...])
    z = DN_ALPHA * x + FFN_RES * (1.0 + gate) * y
    return _layer_norm(z, lng_ref[...], lnb_ref[...])


def _ffn_kernel(x_ref, mod_ref, wi_ref, wo_ref, lng_ref, lnb_ref, o_ref, h_ref, act_ref, *, row0, n_chunks):
    o_ref[...] = _swiglu_post_norm(x_ref[...], mod_ref, row0, wi_ref, wo_ref, lng_ref, lnb_ref, h_ref, act_ref, n_chunks)


def _mixer_out_ffn_kernel(y_ref, x_ref, mod_ref, wm_ref, lng1_ref, lnb1_ref, wi_ref, wo_ref, lng2_ref, lnb2_ref,
                          o_ref, h_ref, act_ref, *, n_chunks):
    gate = mod_ref[0, 5:6, :]
    z = DN_ALPHA * x_ref[...] + (1.0 + gate) * _dot(y_ref[...], wm_ref[...])
    x1 = _layer_norm(z, lng1_ref[...], lnb1_ref[...])
    o_ref[...] = _swiglu_post_norm(x1, mod_ref, 6, wi_ref, wo_ref, lng2_ref, lnb2_ref, h_ref, act_ref, n_chunks)


def _ffn(x, mod, row0, w_in, w_out, ln_g, ln_b, seq):
    n_tok, d = x.shape
    f = w_out[0].shape[-2]
    tm = min(TOKEN_TILE, seq)
    tiles_per_seq = seq // tm
    return pl.pallas_call(
        functools.partial(_ffn_kernel, row0=row0, n_chunks=f // MXU_WIDTH),
        grid=(n_tok // tm,),
        in_specs=[
            pl.BlockSpec((tm, d), lambda i: (i, 0)),
            pl.BlockSpec((1, 9, d), lambda i: (i // tiles_per_seq, 0, 0)),
            _resident_slice(*w_in),
            _resident_slice(*w_out),
            _resident((1, d)),
            _resident((1, d)),
        ],
        out_specs=pl.BlockSpec((tm, d), lambda i: (i, 0)),
        out_shape=jax.ShapeDtypeStruct((n_tok, d), F32),
        scratch_shapes=[pltpu.VMEM((tm, d), BF16), pltpu.VMEM((tm, f), BF16)],
        compiler_params=_params("parallel"),
        name="ffn",
    )(x, mod, w_in[0], w_out[0], ln_g, ln_b)


def _mixer_out_ffn(y, x, mod, w_mix, ln1, w_in, w_out, ln2, seq):
    n_tok, d = x.shape
    k = y.shape[1]
    f = w_out[0].shape[-2]
    tm = min(TOKEN_TILE, seq)
    tiles_per_seq = seq // tm
    return pl.pallas_call(
        functools.partial(_mixer_out_ffn_kernel, n_chunks=f // MXU_WIDTH),
        grid=(n_tok // tm,),
        in_specs=[
            pl.BlockSpec((tm, k), lambda i: (i, 0)),
            pl.BlockSpec((tm, d), lambda i: (i, 0)),
            pl.BlockSpec((1, 9, d), lambda i: (i // tiles_per_seq, 0, 0)),
            _resident_slice(*w_mix), _resident((1, d)), _resident((1, d)),
            _resident_slice(*w_in), _resident_slice(*w_out), _resident((1, d)), _resident((1, d)),
        ],
        out_specs=pl.BlockSpec((tm, d), lambda i: (i, 0)),
        out_shape=jax.ShapeDtypeStruct((n_tok, d), F32),
        scratch_shapes=[pltpu.VMEM((tm, d), BF16), pltpu.VMEM((tm, f), BF16)],
        compiler_params=_params("parallel"),
        name="mixer_out_ffn",
    )(y, x, mod, w_mix[0], *ln1, w_in[0], w_out[0], *ln2)


def _proj_kernel(x_ref, mod_ref, w_ref, o_ref, *, row0, tn):
    x = x_ref[...]
    shift = mod_ref[0, row0:row0 + 1, :]
    scale = mod_ref[0, row0 + 1:row0 + 2, :]
    h = (x * (1.0 + scale) + shift).astype(BF16)
    for j in range(o_ref.shape[1] // tn):
        o_ref[:, j * tn:(j + 1) * tn] = _dot(h, w_ref[:, j * tn:(j + 1) * tn]).astype(o_ref.dtype)


def _proj(x, mod, row0, w, seq, tn):
    n_tok, d = x.shape
    n = w[0].shape[-1]
    tm = min(TOKEN_TILE, seq)
    tiles_per_seq = seq // tm
    return pl.pallas_call(
        functools.partial(_proj_kernel, row0=row0, tn=tn),
        grid=(n_tok // tm,),
        in_specs=[
            pl.BlockSpec((tm, d), lambda i: (i, 0)),
            pl.BlockSpec((1, 9, d), lambda i: (i // tiles_per_seq, 0, 0)),
            _resident_slice(*w),
        ],
        out_specs=pl.BlockSpec((tm, n), lambda i: (i, 0)),
        out_shape=jax.ShapeDtypeStruct((n_tok, n), BF16),
        compiler_params=_params("parallel"),
        name="mixer_in_proj",
    )(x, mod, w[0])


def _nsa_proj_kernel(x_ref, mod_ref, w_ref, feat_ref, q_ref, kcv_ref, keys_ref, vst_ref, vwt_ref, gates_ref, *, row0, q_scale):
    x = x_ref[...]
    shift = mod_ref[0, row0:row0 + 1, :]
    scale = mod_ref[0, row0 + 1:row0 + 2, :]
    h = (x * (1.0 + scale) + shift).astype(BF16)
    half = keys_ref.shape[1] // 2
    lane_tile = kcv_ref.shape[2]
    col = 0
    for ref in (q_ref, kcv_ref, keys_ref, vst_ref, vwt_ref, gates_ref):
        n = kcv_ref.shape[0] * lane_tile if ref is kcv_ref else ref.shape[1]
        step = min(n, MXU_WIDTH)
        for j in range(0, n, step):
            y = _dot(h, w_ref[:, col + j:col + j + step])
            if ref is q_ref:
                y = y * q_scale
                for r in range(y.shape[0] // Q_BLOCK):
                    q_ref[r, j:j + step, :] = y[r * Q_BLOCK:(r + 1) * Q_BLOCK, :].T.astype(q_ref.dtype)
                continue
            elif ref is keys_ref:
                f = feat_ref[:, 0:KEY_WIDTH] if j < half else feat_ref[:, KEY_WIDTH:2 * KEY_WIDTH]
                y = y + jnp.concatenate([f] * (step // KEY_WIDTH), 1).astype(F32)
            elif ref is gates_ref:
                y = _sigmoid(y)
                for r in range(y.shape[0] // Q_BLOCK):
                    gates_ref[r] = y[r * Q_BLOCK:(r + 1) * Q_BLOCK, :].T
                continue
            elif ref is vst_ref or ref is vwt_ref:
                tile = ref.shape[2]
                for r in range(y.shape[0] // tile):
                    ref[r, j:j + step, :] = y[r * tile:(r + 1) * tile, :].T.astype(ref.dtype)
                continue
            if ref is kcv_ref:
                for c in range(step // lane_tile):
                    kcv_ref[(j + c * lane_tile) // lane_tile] = y[:, c * lane_tile:(c + 1) * lane_tile]
            else:
                ref[:, j:j + step] = y.astype(ref.dtype)
        col += n


def _nsa_proj(x, mod, row0, w, feat, seq, d, kv, q_scale):
    n_tok = x.shape[0]
    groups = NSA_GROUPS
    tm = min(TOKEN_TILE, seq)
    tiles_per_seq = seq // tm
    lane = 128
    widths = (d, 2 * kv, 2 * groups * KEY_WIDTH, kv, kv, w.shape[1] - d - 4 * kv - 2 * groups * KEY_WIDTH)
    dtypes = (BF16, F32, BF16, BF16, BF16, F32)
    transposed = lambda n, tile, dt: (pl.BlockSpec((tm // tile, n, tile), lambda i: (i, 0, 0)),
                                      jax.ShapeDtypeStruct((n_tok // tile, n, tile), dt))
    token_major = lambda n, dt: (pl.BlockSpec((tm, n), lambda i: (i, 0)), jax.ShapeDtypeStruct((n_tok, n), dt))
    outs = [transposed(widths[0], Q_BLOCK, dtypes[0]),
            (pl.BlockSpec((widths[1] // lane, tm, lane), lambda i: (0, i, 0)),
             jax.ShapeDtypeStruct((widths[1] // lane, n_tok, lane), dtypes[1])),
            token_major(widths[2], dtypes[2]),
            transposed(widths[3], SEL_KEY_TILE, dtypes[3]),
            transposed(widths[4], WIN_KEY_TILE, dtypes[4]),
            transposed(widths[5], Q_BLOCK, dtypes[5])]
    return pl.pallas_call(
        functools.partial(_nsa_proj_kernel, row0=row0, q_scale=q_scale),
        grid=(n_tok // tm,),
        in_specs=[
            pl.BlockSpec((tm, d), lambda i: (i, 0)),
            pl.BlockSpec((1, 9, d), lambda i: (i // tiles_per_seq, 0, 0)),
            _resident(w.shape),
            pl.BlockSpec((tm, 2 * KEY_WIDTH), lambda i: (i % tiles_per_seq, 0)),
        ],
        out_specs=[spec for spec, _ in outs],
        out_shape=[sds for _, sds in outs],
        compiler_params=_params("parallel"),
        name="nsa_in_proj",
    )(x, mod, w, feat)


def _ret_kernel(q_ref, k_ref, v_ref, g_ref, o_ref, state_ref, *, chunk, n_chunks, k_scale):
    heads = state_ref.shape[0]
    dk, dv = state_ref.shape[1], state_ref.shape[2]
    row = lax.broadcasted_iota(jnp.int32, (chunk, chunk), 0)
    col = lax.broadcasted_iota(jnp.int32, (chunk, chunk), 1)
    diff = (row - col).astype(F32)
    pos = lax.broadcasted_iota(jnp.int32, (chunk, 1), 0).astype(F32)
    decays = []
    for hh in range(heads):
        head = (pl.program_id(1) * heads + hh).astype(F32)
        log_g = jnp.log(1.0 - jnp.exp2(-5.0 - (jnp.zeros((1, 1), F32) + head)))
        decays.append((jnp.where(diff >= 0, jnp.exp(log_g * jnp.maximum(diff, 0.0)), 0.0),
                       jnp.exp(log_g * (pos + 1.0)),
                       jnp.exp(log_g * (chunk - 1.0 - pos)),
                       jnp.exp(log_g * float(chunk))))
    state_ref[...] = jnp.zeros_like(state_ref)

    def step(c, carry):
        rows = pl.ds(pl.multiple_of(c * chunk, chunk), chunk)
        for hh in range(heads):
            decay_intra, decay_q, decay_k, decay_state = decays[hh]
            q = q_ref[0, rows, hh * dk:(hh + 1) * dk]
            k = k_ref[0, rows, hh * dk:(hh + 1) * dk] * k_scale
            v = v_ref[0, rows, hh * dv:(hh + 1) * dv]
            scores = lax.dot_general(q, k, (((1,), (1,)), ((), ())), preferred_element_type=F32) * decay_intra
            state = state_ref[hh]
            o = _dot(scores.astype(BF16), v) + _dot((q.astype(F32) * decay_q).astype(BF16), state.astype(BF16))
            k_dec_t = (k.astype(F32) * decay_k).T.astype(BF16)
            state_ref[hh] = decay_state * state + _dot(k_dec_t, v)
            mu = jnp.mean(o, -1, keepdims=True)
            oc = o - mu
            var = jnp.mean(oc * oc, -1, keepdims=True)
            o = oc * lax.rsqrt(var + GN_EPS)
            g = g_ref[0, rows, hh * dv:(hh + 1) * dv]
            o_ref[0, rows, hh * dv:(hh + 1) * dv] = o.astype(o_ref.dtype) * (g * _sigmoid(g))
        return carry

    lax.fori_loop(0, n_chunks, step, 0)


def _retention_core(proj, batch, seq, dk, dv):
    heads = RET_HEADS
    hps = RET_HEADS_PER_STEP
    chunk = min(RET_CHUNK, seq)
    k_blk0 = heads * dk // (hps * dk)
    v_blk0 = 2 * heads * dk // (hps * dv)
    g_blk0 = (2 * heads * dk + heads * dv) // (hps * dv)
    return pl.pallas_call(
        functools.partial(_ret_kernel, chunk=chunk, n_chunks=seq // chunk, k_scale=dk ** -0.5),
        grid=(batch, heads // hps),
        in_specs=[
            pl.BlockSpec((1, seq, hps * dk), lambda b, h: (b, 0, h)),
            pl.BlockSpec((1, seq, hps * dk), lambda b, h: (b, 0, k_blk0 + h)),
            pl.BlockSpec((1, seq, hps * dv), lambda b, h: (b, 0, v_blk0 + h)),
            pl.BlockSpec((1, seq, hps * dv), lambda b, h: (b, 0, g_blk0 + h)),
        ],
        out_specs=pl.BlockSpec((1, seq, hps * dv), lambda b, h: (b, 0, h)),
        out_shape=jax.ShapeDtypeStruct((batch, seq, heads * dv), BF16),
        scratch_shapes=[pltpu.VMEM((hps, dk, dv), F32)],
        compiler_params=_params("parallel", "parallel"),
        name="retention_core",
    )(proj, proj, proj, proj)


def _gelu_tanh(x):
    return 0.5 * x * (1.0 + jnp.tanh(0.7978845608028654 * (x + 0.044715 * (x * x * x))))


def _compress_kernel(x_ref, pek_ref, pev_ref, k1_ref, k2_ref, v1_ref, v2_ref, ok_ref, ov_ref, y_ref):
    groups, n_rows, half = y_ref.shape
    dh = half // CMP_STRIDE
    kv = groups * dh
    per_tile = x_ref.shape[2] // dh

    def mlp(lane0, pe_ref, w1_ref, w2_ref, out_ref):
        for r in range(CMP_STRIDE):
            for c in range(groups // per_tile):
                piece = x_ref[lane0 // x_ref.shape[2] + c, pl.ds(r, n_rows, stride=CMP_STRIDE), :]
                for k in range(per_tile):
                    y_ref[c * per_tile + k, :, r * dh:(r + 1) * dh] = piece[:, k * dh:(k + 1) * dh].astype(BF16)
        bias = _dot(pe_ref[...].astype(BF16), w1_ref[...])
        for g in range(groups):
            y = y_ref[g]
            first = _dot(y, w1_ref[0:half, :])
            second = _dot(y, w1_ref[half:2 * half, :])
            hidden = first + pltpu.roll(second, n_rows - 1, 0) + bias
            out_ref[0, g] = _dot(_gelu_tanh(hidden).astype(BF16), w2_ref[...]).astype(out_ref.dtype)

    mlp(0, pek_ref, k1_ref, k2_ref, ok_ref)
    mlp(kv, pev_ref, v1_ref, v2_ref, ov_ref)


def _compress(kcv, pe_k, pe_v, k1, k2, v1, v2, groups, batch):
    lane_tiles, n_tok, lane = kcv.shape
    seq = n_tok // batch
    rows = seq // CMP_STRIDE
    hidden = k1.shape[1]
    dh = k2.shape[1]
    out_blk = pl.BlockSpec((1, groups, rows, dh), lambda b: (b, 0, 0, 0))
    out_sds = jax.ShapeDtypeStruct((batch, groups, rows, dh), BF16)
    return pl.pallas_call(
        _compress_kernel,
        grid=(batch,),
        in_specs=[pl.BlockSpec((lane_tiles, seq, lane), lambda b: (0, b, 0)),
                  _resident(pe_k.shape), _resident(pe_v.shape),
                  _resident(k1.shape), _resident(k2.shape), _resident(v1.shape), _resident(v2.shape)],
        out_specs=[out_blk, out_blk],
        out_shape=[out_sds, out_sds],
        scratch_shapes=[pltpu.VMEM((groups, rows, CMP_STRIDE * dh), BF16)],
        compiler_params=_params("parallel"),
        name="nsa_compress",
    )(kcv, pe_k, pe_v, k1, k2, v1, v2)


def _nsa_attn_kernel(q_ref, kc_ref, vct_ref, keys_ref, vst_ref, vwt_ref, gates_ref, o_ref,
                     qa_ref, s_sel_ref, s_win_ref, acc_ref, l8_ref, *, n_sel, top):
    step = pl.program_id(1)

    def one_q_block(sub, carry):
        q_rows = pl.ds(pl.multiple_of(sub * Q_BLOCK, Q_BLOCK), Q_BLOCK)
        _nsa_q_block(step * Q_BLOCKS_PER_STEP + sub, sub, q_rows, q_ref, kc_ref, vct_ref, keys_ref, gates_ref, o_ref,
                     qa_ref, vst_ref, vwt_ref, s_sel_ref, s_win_ref, acc_ref, l8_ref, n_sel, top)
        return carry

    lax.fori_loop(0, Q_BLOCKS_PER_STEP, one_q_block, 0)


def _nsa_q_block(qb, sub, q_rows, q_ref, kc_ref, vct_ref, keys_ref, gates_ref, o_ref,
                 qa_ref, vst_ref, vwt_ref, s_sel_ref, s_win_ref, acc_ref, l8_ref, n_sel, top):
    q0 = qb * Q_BLOCK
    groups, dh, width = qa_ref.shape[0], vct_ref.shape[2], qa_ref.shape[2]
    sel_row0 = KEY_WIDTH - n_sel
    all_groups = range(groups)
    pair = 2 * dh
    assert pair == Q_BLOCK == KEY_WIDTH

    lane = lax.broadcasted_iota(jnp.int32, (1, width), 1)
    t_row = q0 + lane % Q_BLOCK
    piece_row = lax.broadcasted_iota(jnp.int32, (POS_ROWS, width), 0)

    for g in all_groups:
        head = (g * NSA_HPG + lane // Q_BLOCK).astype(F32)
        slope = jnp.exp2(-8.0 * (head + 1.0) / NSA_HEADS) * LOG2_E
        hi = slope.astype(BF16).astype(F32)
        rest = slope - hi
        mid = rest.astype(BF16).astype(F32)
        lo = (rest - mid).astype(BF16).astype(F32)
        feat = jnp.zeros((POS_ROWS, width), F32)
        for idx, piece in enumerate((SEL_BLOCK * hi, SEL_BLOCK * mid, SEL_BLOCK * lo, hi, mid, lo)):
            feat = jnp.where(piece_row == idx, piece, feat)
        for h in range(NSA_HPG):
            lanes0 = (g * NSA_HPG + h) * dh
            qa_ref[g, 0:dh, h * Q_BLOCK:(h + 1) * Q_BLOCK] = q_ref[0, sub, lanes0:lanes0 + dh, :]
        qa_ref[g, dh:dh + POS_ROWS, :] = feat.astype(BF16)
        qa_ref[g, dh + POS_ROWS:KEY_WIDTH, :] = jnp.zeros((KEY_WIDTH - dh - POS_ROWS, width), BF16)

    n_cmp_pad = kc_ref.shape[2]
    blk = lax.broadcasted_iota(jnp.int32, (n_cmp_pad, width), 0)
    mask_c = blk * CMP_STRIDE + (CMP_BLOCK - 1) <= t_row
    any_visible = t_row >= CMP_BLOCK - 1
    scores_c = [_dot(kc_ref[0, g], qa_ref[g]) for g in all_groups]
    p_cmp = []
    for g in all_groups:
        s = jnp.where(mask_c, scores_c[g], NEG_INF)
        e = jnp.exp2(s - jnp.max(s, 0, keepdims=True))
        p_cmp.append(e * jnp.where(any_visible, 1.0 / jnp.sum(e, 0, keepdims=True), 0.0))
    o_cmp = [_dot(vct_ref[0, g], p_cmp[g].astype(BF16)) for g in all_groups]

    sel_i = lax.broadcasted_iota(jnp.int32, (n_sel, n_cmp_pad), 0) * SEL_BLOCK
    cmp_i = lax.broadcasted_iota(jnp.int32, (n_sel, n_cmp_pad), 1) * CMP_STRIDE
    overlap = jnp.maximum(jnp.minimum(cmp_i + CMP_BLOCK, sel_i + SEL_BLOCK) - jnp.maximum(cmp_i, sel_i), 0)
    overlap = (overlap.astype(F32) * (1.0 / CMP_BLOCK)).astype(BF16)
    j_idx = lax.broadcasted_iota(jnp.int32, (n_sel, Q_BLOCK), 0)
    t_sel = q0 + lax.broadcasted_iota(jnp.int32, (n_sel, Q_BLOCK), 1)
    cur = t_sel // SEL_BLOCK
    forced = (j_idx == 0) | (j_idx == cur) | (j_idx == cur - 1)
    valid = j_idx * SEL_BLOCK <= t_sel
    pieces = []
    for g in all_groups:
        p_sum = p_cmp[g][:, 0:Q_BLOCK]
        for h in range(1, NSA_HPG):
            p_sum = p_sum + p_cmp[g][:, h * Q_BLOCK:(h + 1) * Q_BLOCK]
        p_hi = p_sum.astype(BF16)
        rest = p_sum - p_hi.astype(F32)
        p_mid = rest.astype(BF16)
        pieces.append((p_hi, p_mid, (rest - p_mid.astype(F32)).astype(BF16)))
    imps = [_dot(overlap, hi) + _dot(overlap, mid) + _dot(overlap, lo) for hi, mid, lo in pieces]
    for g in all_groups:
        score = jnp.where(valid, jnp.where(forced, FORCED_SCORE, imps[g]), INVALID_SCORE)
        tiles = [score[r:r + 8, :] for r in range(0, n_sel, 8)]
        ranks = [jnp.zeros((8, Q_BLOCK), F32) for _ in tiles]
        for jp in range(n_sel):
            other = jnp.broadcast_to(score[jp:jp + 1, :], (8, Q_BLOCK))
            for v, tile_scores in enumerate(tiles):
                if jp < 8 * v:
                    ahead = other >= tile_scores
                elif jp >= 8 * v + 7:
                    ahead = other > tile_scores
                else:
                    ahead = (other > tile_scores) | ((other == tile_scores) & (j_idx[0:8, :] > jp - 8 * v))
                ranks[v] = ranks[v] + jnp.where(ahead, 1.0, 0.0)
        rank = jnp.concatenate(ranks, 0)
        block_bias = jnp.where(rank < top, 0.0, NEG_INF)
        qa_ref[g, sel_row0:KEY_WIDTH, :] = jnp.concatenate([block_bias] * NSA_HPG, 1).astype(BF16)

    def row_max(sc):
        return jnp.max(sc.reshape(sc.shape[0] // 8, 8, width), 0)

    def for_tiles(n, body, init):
        carry, start, width_ = init, 0, TILE_UNROLL
        while width_ >= 1:
            def several(j, c, start=start, width_=width_):
                for u in range(width_):
                    c = body(start + width_ * j + u, c)
                return c
            trips = (n - start) // width_
            carry = lax.fori_loop(0, trips, several, carry)
            start = start + trips * width_
            width_ //= 2
        return carry

    def softmax_values(s_ref, vt_ref, first_tile, n_tiles, m8):
        m = [jnp.max(m8[g], 0, keepdims=True) for g in all_groups]
        acc_ref[...] = jnp.zeros_like(acc_ref)
        l8_ref[...] = jnp.zeros_like(l8_ref)

        def step(i, carry):
            pes = [jnp.exp2(s_ref[g, i] - m[g]) for g in all_groups]
            for g in all_groups:
                l8_ref[g] += jnp.sum(pes[g].reshape(pes[g].shape[0] // 8, 8, width), 0)
                acc_ref[g] += _dot(vt_ref[0, first_tile + i, g * dh:(g + 1) * dh, :], pes[g].astype(BF16))
            return carry

        for_tiles(n_tiles, step, 0)
        return [(acc_ref[g], 1.0 / jnp.sum(l8_ref[g], 0, keepdims=True)) for g in all_groups]

    m8_init = tuple(jnp.full((8, width), NEG_INF, F32) for _ in all_groups)

    tile = SEL_KEY_TILE

    def sel_scores(g, kt):
        rows = pl.ds(pl.multiple_of(kt * tile, tile), tile)
        return _dot(keys_ref[0, rows, g * KEY_WIDTH:(g + 1) * KEY_WIDTH], qa_ref[g])

    def sel_pass(kt, m8):
        out = []
        for g in all_groups:
            sc = sel_scores(g, kt)
            s_sel_ref[g, kt] = sc
            out.append(jnp.maximum(m8[g], row_max(sc)))
        return tuple(out)

    last = (q0 + Q_BLOCK - 1) // tile
    m8 = list(for_tiles(last, sel_pass, m8_init))
    visible = last * tile + lax.broadcasted_iota(jnp.int32, (tile, width), 0) <= t_row
    last_scores = [sel_scores(g, last) for g in all_groups]
    for g in all_groups:
        sc = jnp.where(visible, last_scores[g], NEG_INF)
        s_sel_ref[g, last] = sc
        m8[g] = jnp.maximum(m8[g], row_max(sc))
    o_sel = softmax_values(s_sel_ref, vst_ref, 0, last + 1, m8)

    tile = WIN_KEY_TILE
    first = jnp.maximum(q0 - WINDOW, 0) // tile
    n_win = (q0 + Q_BLOCK - 1) // tile - first + 1
    key_row = lax.broadcasted_iota(jnp.int32, (tile, width), 0)

    def win_pass(i, m8):
        kt = first + i
        rows = pl.ds(pl.multiple_of(kt * tile, tile), tile)
        dist = (t_row - kt * tile) - key_row
        in_window = (dist >= 0) & (dist < WINDOW)
        out = []
        for g in all_groups:
            k_tile = keys_ref[0, rows, (groups + g) * KEY_WIDTH:(groups + g + 1) * KEY_WIDTH]
            sc = jnp.where(in_window, _dot(k_tile, qa_ref[g]), NEG_INF)
            s_win_ref[g, i] = sc
            out.append(jnp.maximum(m8[g], row_max(sc)))
        return tuple(out)

    m8 = for_tiles(n_win, win_pass, m8_init)
    o_win = softmax_values(s_win_ref, vwt_ref, first, n_win, m8)

    gates_t = gates_ref[0, sub]
    for g in all_groups:
        per_head = []
        for h in range(NSA_HPG):
            lanes = slice(h * Q_BLOCK, (h + 1) * Q_BLOCK)
            row = (g * NSA_HPG + h) * 3
            (acc_sel, inv_sel), (acc_win, inv_win) = o_sel[g], o_win[g]
            per_head.append(gates_t[row:row + 1, :] * o_cmp[g][:, lanes]
                            + (gates_t[row + 1:row + 2, :] * inv_sel[:, lanes]) * acc_sel[:, lanes]
                            + (gates_t[row + 2:row + 3, :] * inv_win[:, lanes]) * acc_win[:, lanes])
        for p in range(NSA_HPG // 2):
            lanes0 = (g * NSA_HPG + 2 * p) * dh
            two_heads = jnp.concatenate(per_head[2 * p:2 * p + 2], 0)
            o_ref[0, q_rows, lanes0:lanes0 + pair] = two_heads.T.astype(o_ref.dtype)


def _nsa_attention(q, kc, vct, keys, vst, vwt, gates, groups):
    batch, n_qb, d, _ = q.shape
    seq = n_qb * Q_BLOCK
    dh = vct.shape[2]
    width = NSA_HPG * Q_BLOCK
    n_sel = seq // SEL_BLOCK
    top = min(SEL_TOP, n_sel)
    assert dh + POS_ROWS + n_sel <= KEY_WIDTH
    per_b = lambda shape: pl.BlockSpec((1,) + shape, lambda b, i: (b,) + (0,) * len(shape))
    per_q = lambda lanes: pl.BlockSpec((1, Q_BLOCKS_PER_STEP * Q_BLOCK, lanes), lambda b, i: (b, i, 0))
    per_q_t = lambda lanes: pl.BlockSpec((1, Q_BLOCKS_PER_STEP, lanes, Q_BLOCK), lambda b, i: (b, i, 0, 0))
    n_win_tiles = min(WINDOW + Q_BLOCK, seq) // WIN_KEY_TILE
    return pl.pallas_call(
        functools.partial(_nsa_attn_kernel, n_sel=n_sel, top=top),
        grid=(batch, n_qb // Q_BLOCKS_PER_STEP),
        in_specs=[
            per_q_t(d),
            per_b(kc.shape[1:]), per_b(vct.shape[1:]),
            per_b(keys.shape[1:]), per_b(vst.shape[1:]), per_b(vwt.shape[1:]),
            per_q_t(gates.shape[2]),
        ],
        out_specs=per_q(d),
        out_shape=jax.ShapeDtypeStruct((batch, seq, d), BF16),
        scratch_shapes=[pltpu.VMEM((groups, KEY_WIDTH, width), BF16),
                        pltpu.VMEM((groups, seq // SEL_KEY_TILE, SEL_KEY_TILE, width), F32),
                        pltpu.VMEM((groups, n_win_tiles, WIN_KEY_TILE, width), F32),
                        pltpu.VMEM((groups, dh, width), F32),
                        pltpu.VMEM((groups, 8, width), F32)],
        compiler_params=_params("parallel", "arbitrary"),
        name="nsa_attention",
    )(q, kc, vct, keys, vst, vwt, gates)


def _key_features(pos, dh, n_sel):
    block, offset = pos // SEL_BLOCK, pos % SEL_BLOCK
    feat = np.zeros((pos.shape[0], KEY_WIDTH), np.float32)
    feat[:, dh:dh + 3] = block[:, None]
    feat[:, dh + 3:dh + 6] = offset[:, None]
    if n_sel:
        feat[:, KEY_WIDTH - n_sel:] = block[:, None] == np.arange(n_sel)[None, :]
    return feat


def _nsa_weights(w_in, d, kv, groups):
    dh = kv // groups
    part = lambda j: w_in[:, d + j * kv:d + (j + 1) * kv]
    pad_keys = lambda w: jnp.pad(w.reshape(-1, groups, dh), ((0, 0), (0, 0), (0, KEY_WIDTH - dh))).reshape(-1, groups * KEY_WIDTH)
    gate_w = w_in[:, d + 6 * kv:]
    gate_w = jnp.pad(gate_w, ((0, 0), (0, -gate_w.shape[1] % 128)))
    cols = [w_in[:, :d], part(0), part(1), pad_keys(part(2)), pad_keys(part(4)), part(3), part(5), gate_w]
    return jnp.concatenate(cols, 1).astype(BF16)


def _nsa_mixer(x, mod, w_in, pe_k, pe_v, k1, k2, v1, v2, batch, seq):
    d = x.shape[1]
    groups = NSA_GROUPS
    dh = d // NSA_HEADS
    kv = groups * dh
    n_sel = seq // SEL_BLOCK
    positions = np.arange(seq)
    feat = jnp.asarray(np.concatenate([_key_features(positions, dh, n_sel), _key_features(positions, dh, 0)], 1), BF16)
    q, kcv, keys, vst, vwt, gates = _nsa_proj(x, mod, 3, _nsa_weights(w_in, d, kv, groups), feat, seq, d, kv,
                                          dh ** -0.5 * LOG2_E)
    kc, vc = _compress(kcv, pe_k, pe_v, k1, k2, v1, v2, groups, batch)
    cmp_end = np.arange(seq // CMP_STRIDE) * CMP_STRIDE + CMP_BLOCK - 1
    cmp_feat = jnp.asarray(_key_features(cmp_end, dh, 0)[:, dh:], BF16)
    kc = jnp.concatenate([kc, jnp.broadcast_to(cmp_feat, kc.shape[:2] + cmp_feat.shape)], -1)
    n_qb = seq // Q_BLOCK
    o = _nsa_attention(q.reshape(batch, n_qb, d, Q_BLOCK), kc, vc.transpose(0, 1, 3, 2), keys.reshape(batch, seq, -1),
                       vst.reshape(batch, seq // SEL_KEY_TILE, kv, SEL_KEY_TILE),
                       vwt.reshape(batch, seq // WIN_KEY_TILE, kv, WIN_KEY_TILE), gates.reshape(batch, n_qb, -1, Q_BLOCK), groups)
    return o.reshape(batch * seq, d)


def kernel(x, c, ada_w, ada_b, ln_g, ln_b, ffn_w_in, ffn_w_out, ret_w_in, ret_w_out, nsa_w_in, nsa_w_out,
           nsa_pe_k, nsa_pe_v, nsa_ck_w1, nsa_ck_w2, nsa_cv_w1, nsa_cv_w2):
    batch, seq, d = x.shape
    depth = ada_w.shape[0]
    assert depth == DEPTH
    mod_all = _ada_mod(c, ada_w, ada_b).reshape(depth, batch, 9, d)
    xf = x.reshape(batch * seq, d)
    dk = d // RET_HEADS
    dv = 2 * dk
    ffn_wi, ffn_wo = ffn_w_in.astype(BF16), ffn_w_out.astype(BF16)
    ret_wi, ret_wo, nsa_wo = ret_w_in.astype(BF16), ret_w_out.astype(BF16), nsa_w_out.astype(BF16)
    for i in range(depth):
        mod = mod_all[i]
        ln = lambda k: (ln_g[i, k].reshape(1, d), ln_b[i, k].reshape(1, d))
        xf = _ffn(xf, mod, 0, (ffn_wi, (i, 0)), (ffn_wo, (i, 0)), *ln(0), seq)
        j = i // N_MIXERS
        if i % N_MIXERS == 0:
            proj = _proj(xf, mod, 3, (ret_wi, (j,)), seq, 2 * MXU_WIDTH)
            y = _retention_core(proj.reshape(batch, seq, -1), batch, seq, dk, dv).reshape(batch * seq, -1)
            w_mix = (ret_wo, (j,))
        else:
            y = _nsa_mixer(xf, mod, nsa_w_in[j], nsa_pe_k[j].reshape(1, -1), nsa_pe_v[j].reshape(1, -1),
                           nsa_ck_w1[j].astype(BF16), nsa_ck_w2[j].astype(BF16),
                           nsa_cv_w1[j].astype(BF16), nsa_cv_w2[j].astype(BF16), batch, seq)
            w_mix = (nsa_wo, (j,))
        xf = _mixer_out_ffn(y, xf, mod, w_mix, ln(1), (ffn_wi, (i, 1)), (ffn_wo, (i, 1)), ln(2), seq)
    return xf.reshape(batch, seq, d)
```
